```python
import math
import jax, jax.numpy as jnp
from jax import lax
import numpy as np

D_MODEL = 1024
BATCH = 16
SEQ = 256
DEPTH = 2
DEC_BATCH = 2
DEC_SEQ = 2048
PAST_LEN = 256

GRID_W = 64
ROPE_BASE = 10000.0
CONV_W = 512
CONV_K = 31
DIFF_HEADS = 4
DIFF_HD = 64
DIFF_W = DIFF_HEADS * 2 * DIFF_HD
MLA_HEADS = 8
MLA_NOPE = 64
MLA_ROPE = 32
MLA_QK = MLA_NOPE + MLA_ROPE
MLA_V = 64
MLA_Q_RANK = 384
MLA_KV_RANK = 256
MLA_W = MLA_HEADS * MLA_V
N_BRANCH = 3
D_FF = 4 * D_MODEL
IN_COLS = 2 * CONV_W + 3 * DIFF_W + MLA_Q_RANK + MLA_KV_RANK + MLA_ROPE + N_BRANCH * D_MODEL
Q_BLOCK = 128
EPS = 1e-6

kernel_name = 'hybrid_diffusion_conformer_diffattn_mla_step'


def _rms(x, g):
    xf = x.astype(jnp.float32)
    y = xf * lax.rsqrt(jnp.mean(xf * xf, axis=-1, keepdims=True) + EPS)
    return (y * g.astype(jnp.float32)).astype(x.dtype)


def _split_points():
    sizes = (2 * CONV_W, DIFF_W, DIFF_W, DIFF_W, MLA_Q_RANK, MLA_KV_RANK, MLA_ROPE)
    pts, acc = [], 0
    for s in sizes:
        acc += s
        pts.append(acc)
    return pts


def _axial_tables(seq_len, rot_dim):
    rows = seq_len // GRID_W
    row = jnp.repeat(jnp.arange(rows, dtype=jnp.float32), GRID_W)
    col = jnp.tile(jnp.arange(GRID_W, dtype=jnp.float32), rows)
    half = rot_dim // 2
    inv = ROPE_BASE ** (-jnp.arange(0, half, 2, dtype=jnp.float32) / half)
    ang = jnp.stack([row[:, None] * inv, col[:, None] * inv], axis=1)
    return jnp.cos(ang), jnp.sin(ang)


def _rope2d(x, tables):
    cos, sin = tables
    qd = x.shape[-1] // 4
    xr = x.reshape(x.shape[:-1] + (2, 2, qd)).astype(jnp.float32)
    x1, x2 = xr[..., 0, :], xr[..., 1, :]
    shape = (1, cos.shape[0]) + (1,) * (x.ndim - 3) + (2, qd)
    c = cos.reshape(shape)
    s = sin.reshape(shape)
    out = jnp.stack([x1 * c - x2 * s, x2 * c + x1 * s], axis=-2)
    return out.reshape(x.shape).astype(x.dtype)


def _sweep_queries(core, q):
    b, sq = q.shape[:2]
    nb = sq // Q_BLOCK
    qb = jnp.moveaxis(q.reshape((b, nb, Q_BLOCK) + q.shape[2:]), 1, 0)
    out = lax.map(core, qb)
    return jnp.moveaxis(out, 0, 1).reshape((b, sq) + out.shape[3:])


def _diff_attend(q, k, v, lam):
    scale = DIFF_HD ** -0.5
    def core(qb):
        s = jnp.einsum('bqhcd,bkhcd->bchqk', qb, k).astype(jnp.float32) * scale
        a = jax.nn.softmax(s, axis=-1)
        w = a[:, 0] - lam * a[:, 1]
        return jnp.einsum('bhqk,bkhe->bqhe', w.astype(v.dtype), v)
    return _sweep_queries(core, q)


def _mla_attend(q, k, v):
    scale = MLA_QK ** -0.5
    def core(qb):
        s = jnp.einsum('bqhd,bkhd->bhqk', qb, k).astype(jnp.float32) * scale
        a = jax.nn.softmax(s, axis=-1)
        return jnp.einsum('bhqk,bkhe->bqhe', a.astype(v.dtype), v)
    return _sweep_queries(core, q)


def _mla_expand(ckv, kpe, w_ukv, k_norm):
    b, l, _ = ckv.shape
    kv = (ckv @ w_ukv).reshape(b, l, MLA_HEADS, MLA_NOPE + MLA_V)
    k_nope, v = kv[..., :MLA_NOPE], kv[..., MLA_NOPE:]
    k = jnp.concatenate([k_nope, jnp.broadcast_to(kpe[:, :, None, :], (b, l, MLA_HEADS, MLA_ROPE))], axis=-1)
    return _rms(k, k_norm), v


def _conv_module(u, dw, bias, g, w_out):
    a, gate = jnp.split(u, 2, axis=-1)
    u = a * jax.nn.sigmoid(gate)
    y = lax.conv_general_dilated(u, dw[:, None, :], window_strides=(1,),
                                 padding=[(CONV_K // 2, CONV_K // 2)],
                                 dimension_numbers=('NWC', 'WIO', 'NWC'),
                                 feature_group_count=CONV_W) + bias
    return jax.nn.silu(_rms(y, g)) @ w_out


def _rope_tail(x, tables):
    return jnp.concatenate([x[..., :MLA_NOPE], _rope2d(x[..., MLA_NOPE:], tables)], axis=-1)


def _layer(x, cond, P, l, ctx=None, tabs=None):
    b, s, _ = x.shape
    mod = jax.nn.silu(cond) @ P['mod_w'][l] + P['mod_b'][l]
    sh1, sc1, g1, sh2, sc2, g2 = jnp.split(mod[:, None, :], 6, axis=-1)
    h = _rms(x, P['norm1_g'][l]) * (1 + sc1) + sh1
    proj = h @ P['w_in'][l]
    u_conv, dq, dk, dv, cq, ckv, kpe, gl = jnp.split(proj, _split_points(), axis=-1)

    o_conv = _conv_module(u_conv, P['conv_dw'][l], P['conv_b'][l], P['conv_norm_g'][l], P['w_conv_out'][l])

    q = _rms(dq.reshape(b, s, DIFF_HEADS, 2, DIFF_HD), P['diff_q_norm'][l])
    k = _rms(dk.reshape(b, s, DIFF_HEADS, 2, DIFF_HD), P['diff_k_norm'][l])
    v = dv.reshape(b, s, DIFF_HEADS, 2 * DIFF_HD)

    cq = _rms(cq, P['mla_q_a_norm'][l])
    qm = _rms((cq @ P['w_uq'][l]).reshape(b, s, MLA_HEADS, MLA_QK), P['mla_q_norm'][l])
    ckv = _rms(ckv, P['mla_kv_a_norm'][l])
    state = (k.reshape(b, s, DIFF_HEADS, 2 * DIFF_HD), v, ckv, kpe)
    km, vm = _mla_expand(ckv, kpe, P['w_ukv'][l], P['mla_k_norm'][l])

    if ctx is not None:
        ck, cv, cckv, ckpe = ctx
        q = _rope2d(q, tabs[0])
        k = _rope2d(k, tabs[0])
        qm = _rope_tail(qm, tabs[1])
        km = _rope_tail(km, tabs[1])
        lc = ck.shape[1]
        k = jnp.concatenate([ck.reshape(b, lc, DIFF_HEADS, 2, DIFF_HD), k], axis=1)
        v = jnp.concatenate([cv, v], axis=1)
        kmc, vmc = _mla_expand(cckv, ckpe, P['w_ukv'][l], P['mla_k_norm'][l])
        km = jnp.concatenate([kmc, km], axis=1)
        vm = jnp.concatenate([vmc, vm], axis=1)

    lam_init = 0.8 - 0.6 * math.exp(-0.3 * l)
    lp = P['diff_lambda'][l].astype(jnp.float32)
    lam = jnp.exp(jnp.sum(lp[0] * lp[1])) - jnp.exp(jnp.sum(lp[2] * lp[3])) + lam_init
    od = _rms(_diff_attend(q, k, v, lam), P['diff_subln'][l]) * (1 - lam_init)
    o_diff = od.reshape(b, s, DIFF_W) @ P['w_diff_out'][l]
    o_mla = _mla_attend(qm, km, vm).reshape(b, s, MLA_W) @ P['w_mla_out'][l]

    gates = jax.nn.sigmoid(gl).reshape(b, s, N_BRANCH, D_MODEL)
    merged = gates[..., 0, :] * o_conv + gates[..., 1, :] * o_diff + gates[..., 2, :] * o_mla
    x = x + g1 * (merged @ P['w_out'][l])

    h = _rms(x, P['norm2_g'][l]) * (1 + sc2) + sh2
    x = x + g2 * (jnp.square(jax.nn.relu(h @ P['w_up'][l])) @ P['w_down'][l])
    return x, state


def setup_inputs(seed: int = 0) -> dict:
    key = jax.random.key(seed)
    ks = jax.random.split(key, 40)
    f = jnp.float32
    def nrm(i, shape, scale=1.0):
        return jax.random.normal(ks[i], shape, f) * scale
    def gain(i, shape):
        return 1.0 + 0.01 * jax.random.normal(ks[i], shape, f)
    return {
        'x_prompt': nrm(0, (BATCH, SEQ, D_MODEL)),
        'x_sample': nrm(1, (DEC_BATCH, DEC_SEQ, D_MODEL)),
        'cache_diff_k': nrm(2, (DEC_BATCH, DEPTH, PAST_LEN, DIFF_HEADS, 2 * DIFF_HD)),
        'cache_diff_v': nrm(3, (DEC_BATCH, DEPTH, PAST_LEN, DIFF_HEADS, 2 * DIFF_HD)),
        'cache_mla_ckv': nrm(4, (DEC_BATCH, DEPTH, PAST_LEN, MLA_KV_RANK)),
        'cache_mla_kpe': nrm(5, (DEC_BATCH, DEPTH, PAST_LEN, MLA_ROPE)),
        'c': nrm(6, (DEC_BATCH, D_MODEL)),
        'c_ctx': nrm(7, (D_MODEL,)),
        'mod_w': nrm(8, (DEPTH, D_MODEL, 6 * D_MODEL), D_MODEL ** -0.5),
        'mod_b': nrm(9, (DEPTH, 6 * D_MODEL), 0.01),
        'norm1_g': gain(10, (DEPTH, D_MODEL)),
        'w_in': nrm(11, (DEPTH, D_MODEL, IN_COLS), D_MODEL ** -0.5),
        'conv_dw': nrm(12, (DEPTH, CONV_K, CONV_W), CONV_K ** -0.5),
        'conv_b': nrm(13, (DEPTH, CONV_W), 0.01),
        'conv_norm_g': gain(14, (DEPTH, CONV_W)),
        'w_conv_out': nrm(15, (DEPTH, CONV_W, D_MODEL), CONV_W ** -0.5),
        'diff_q_norm': gain(16, (DEPTH, DIFF_HD)),
        'diff_k_norm': gain(17, (DEPTH, DIFF_HD)),
        'diff_lambda': nrm(18, (DEPTH, 4, DIFF_HD), 0.1),
        'diff_subln': gain(19, (DEPTH, 2 * DIFF_HD)),
        'w_diff_out': nrm(20, (DEPTH, DIFF_W, D_MODEL), DIFF_W ** -0.5),
        'mla_q_a_norm': gain(21, (DEPTH, MLA_Q_RANK)),
        'mla_kv_a_norm': gain(22, (DEPTH, MLA_KV_RANK)),
        'w_uq': nrm(23, (DEPTH, MLA_Q_RANK, MLA_HEADS * MLA_QK), MLA_Q_RANK ** -0.5),
        'w_ukv': nrm(24, (DEPTH, MLA_KV_RANK, MLA_HEADS * (MLA_NOPE + MLA_V)), MLA_KV_RANK ** -0.5),
        'mla_q_norm': gain(25, (DEPTH, MLA_QK)),
        'mla_k_norm': gain(26, (DEPTH, MLA_QK)),
        'w_mla_out': nrm(27, (DEPTH, MLA_W, D_MODEL), MLA_W ** -0.5),
        'w_out': nrm(28, (DEPTH, D_MODEL, D_MODEL), D_MODEL ** -0.5),
        'norm2_g': gain(29, (DEPTH, D_MODEL)),
        'w_up': nrm(30, (DEPTH, D_MODEL, D_FF), D_MODEL ** -0.5),
        'w_down': nrm(31, (DEPTH, D_FF, D_MODEL), D_FF ** -0.5),
    }


def reference(x_prompt, x_sample, cache_diff_k, cache_diff_v, cache_mla_ckv, cache_mla_kpe, c, c_ctx,
              mod_w, mod_b, norm1_g, w_in, conv_dw, conv_b, conv_norm_g, w_conv_out,
              diff_q_norm, diff_k_norm, diff_lambda, diff_subln, w_diff_out,
              mla_q_a_norm, mla_kv_a_norm, w_uq, w_ukv, mla_q_norm, mla_k_norm, w_mla_out,
              w_out, norm2_g, w_up, w_down):
    P = dict(mod_w=mod_w, mod_b=mod_b, norm1_g=norm1_g, w_in=w_in, conv_dw=conv_dw, conv_b=conv_b,
             conv_norm_g=conv_norm_g, w_conv_out=w_conv_out, diff_q_norm=diff_q_norm,
             diff_k_norm=diff_k_norm, diff_lambda=diff_lambda, diff_subln=diff_subln,
             w_diff_out=w_diff_out, mla_q_a_norm=mla_q_a_norm, mla_kv_a_norm=mla_kv_a_norm,
             w_uq=w_uq, w_ukv=w_ukv, mla_q_norm=mla_q_norm, mla_k_norm=mla_k_norm,
             w_mla_out=w_mla_out, w_out=w_out, norm2_g=norm2_g, w_up=w_up, w_down=w_down)

    xp = x_prompt
    cond_ctx = c_ctx[None, :]
    st_k, st_v, st_ckv, st_kpe = [], [], [], []
    for l in range(DEPTH):
        xp, (sk, sv, sckv, skpe) = _layer(xp, cond_ctx, P, l)
        st_k.append(sk)
        st_v.append(sv)
        st_ckv.append(sckv)
        st_kpe.append(skpe)
    y_prompt = xp
    new_diff_k = jnp.stack(st_k, axis=1)
    new_diff_v = jnp.stack(st_v, axis=1)
    new_mla_ckv = jnp.stack(st_ckv, axis=1)
    new_mla_kpe = jnp.stack(st_kpe, axis=1)

    n_lat = x_sample.shape[1]
    tabs = (_axial_tables(n_lat, DIFF_HD), _axial_tables(n_lat, MLA_ROPE))
    xs = x_sample
    for l in range(DEPTH):
        ctx = (cache_diff_k[:, l], cache_diff_v[:, l], cache_mla_ckv[:, l], cache_mla_kpe[:, l])
        xs, _ = _layer(xs, c, P, l, ctx=ctx, tabs=tabs)
    y_sample = xs
    return (y_prompt, y_sample, new_diff_k, new_diff_v, new_mla_ckv, new_mla_kpe)
```

```python
import functools
import math

import numpy as np
import jax
import jax.numpy as jnp
from jax import lax
from jax.experimental import pallas as pl
from jax.experimental.pallas import tpu as pltpu

D_MODEL = 1024
DEPTH = 2
GRID_W = 64
ROPE_BASE = 10000.0
CONV_W = 512
CONV_K = 31
DIFF_HEADS = 4
DIFF_HD = 64
DIFF_W = DIFF_HEADS * 2 * DIFF_HD
MLA_HEADS = 8
MLA_NOPE = 64
MLA_ROPE = 32
MLA_QK = MLA_NOPE + MLA_ROPE
MLA_V = 64
MLA_Q_RANK = 384
MLA_KV_RANK = 256
MLA_W = MLA_HEADS * MLA_V
D_FF = 4 * D_MODEL
EPS = 1e-6

LANES = 128
MLA_PAD = MLA_HEADS * LANES
TOK_TILE = 256
HALO = 16
N_MAIN = 2 * CONV_W + 3 * DIFF_W + MLA_Q_RANK + MLA_KV_RANK
VMEM_LIMIT = 56 * 1024 * 1024

F32 = jnp.float32
BF16 = jnp.bfloat16


def _silu(x):
    return x * jax.nn.sigmoid(x)


def _rms_full(x, g):
    ms = jnp.mean(x * x, axis=-1, keepdims=True)
    return x * lax.rsqrt(ms + EPS) * g


def _rms_half_groups(x, g):
    lo = lax.broadcasted_iota(jnp.int32, (1, LANES), 1) < DIFF_HD
    outs = []
    for h in range(x.shape[-1] // LANES):
        xh = x[:, h * LANES:(h + 1) * LANES]
        x2 = xh * xh
        s_lo = jnp.sum(jnp.where(lo, x2, 0.0), axis=-1, keepdims=True)
        s_hi = jnp.sum(jnp.where(lo, 0.0, x2), axis=-1, keepdims=True)
        r = jnp.where(lo, lax.rsqrt(s_lo * (1.0 / DIFF_HD) + EPS), lax.rsqrt(s_hi * (1.0 / DIFF_HD) + EPS))
        outs.append(xh * r)
    return jnp.concatenate(outs, axis=-1) * g


def _rms_padded_heads(x, g):
    outs = []
    for h in range(x.shape[-1] // LANES):
        xh = x[:, h * LANES:(h + 1) * LANES]
        ss = jnp.sum(xh * xh, axis=-1, keepdims=True)
        outs.append(xh * lax.rsqrt(ss * (1.0 / MLA_QK) + EPS))
    return jnp.concatenate(outs, axis=-1) * g


def _rope(x, cos, sin_signed, half):
    w = x.shape[-1]
    lane = lax.broadcasted_iota(jnp.int32, (1, w), 1)
    first = (lane & (2 * half - 1)) < half
    sw = jnp.where(first, pltpu.roll(x, w - half, axis=1), pltpu.roll(x, half, axis=1))
    return x * cos + sw * sin_signed


def _tile_lanes(t, n):
    return jnp.concatenate([t] * n, axis=-1)


def _modulated_norm(x, g, shift, scale):
    return _rms_full(x, g) * (1.0 + scale) + shift


def _mod_kernel(s_ref, w_ref, b_ref, o_ref):
    s = _silu(s_ref[...]).astype(BF16)
    o_ref[...] = jnp.dot(s, w_ref[...].astype(BF16), preferred_element_type=F32) + b_ref[...]


def _modulation(cond_rows, mod_w, mod_b):
    tn = 1536
    n = 6 * D_MODEL
    return pl.pallas_call(
        _mod_kernel,
        grid=(DEPTH, n // tn),
        in_specs=[
            pl.BlockSpec((8, D_MODEL), lambda l, j: (0, 0)),
            pl.BlockSpec((None, D_MODEL, tn), lambda l, j: (l, 0, j)),
            pl.BlockSpec((None, 1, tn), lambda l, j: (l, 0, j)),
        ],
        out_specs=pl.BlockSpec((None, 8, tn), lambda l, j: (l, 0, j)),
        out_shape=jax.ShapeDtypeStruct((DEPTH, 8, n), F32),
        compiler_params=pltpu.CompilerParams(
            dimension_semantics=("parallel", "parallel"), vmem_limit_bytes=VMEM_LIMIT),
        name="modulation",
    )(cond_rows, mod_w, mod_b.reshape(DEPTH, 1, n))


def _inproj_kernel(*refs, rope, states):
    it = iter(refs)
    x_ref, mod_ref, g1_ref, w_ref = next(it), next(it), next(it), next(it)
    gq_ref, gk_ref, gcq_ref, gckv_ref = next(it), next(it), next(it), next(it)
    wuq_ref, wukv_ref, gmq_ref, gmk_ref = next(it), next(it), next(it), next(it)
    if rope:
        cd_ref, sd_ref, cm_ref, sm_ref = next(it), next(it), next(it), next(it)
    u_ref, qd_ref, kd_ref, vd_ref, qm_ref, km_ref, vm_ref = (next(it) for _ in range(7))
    if states:
        sk_ref, sv_ref, sckv_ref, skpe_ref = next(it), next(it), next(it), next(it)

    mod = mod_ref[...]
    h = _modulated_norm(x_ref[...], g1_ref[...], mod[0:1], mod[1:2])
    proj = jnp.dot(h.astype(BF16), w_ref[...], preferred_element_type=F32)

    o = 0
    u_a = proj[:, o:o + CONV_W]; o += CONV_W
    u_g = proj[:, o:o + CONV_W]; o += CONV_W
    dq = proj[:, o:o + DIFF_W]; o += DIFF_W
    dk = proj[:, o:o + DIFF_W]; o += DIFF_W
    dv = proj[:, o:o + DIFF_W]; o += DIFF_W
    cq = proj[:, o:o + MLA_Q_RANK]; o += MLA_Q_RANK
    ckv = proj[:, o:o + MLA_KV_RANK]; o += MLA_KV_RANK
    kpe2 = proj[:, o:o + LANES]

    u_ref[...] = u_a * jax.nn.sigmoid(u_g)

    q = _rms_half_groups(dq, gq_ref[...])
    k = _rms_half_groups(dk, gk_ref[...])
    if states:
        sk_ref[...] = k
        sv_ref[...] = dv
    if rope:
        cd = _tile_lanes(cd_ref[...], DIFF_W // LANES)
        sd = _tile_lanes(sd_ref[...], DIFF_W // LANES)
        q = _rope(q, cd, sd, DIFF_HD // 4)
        k = _rope(k, cd, sd, DIFF_HD // 4)
    qd_ref[...] = (q * (DIFF_HD ** -0.5)).astype(BF16)
    kd_ref[...] = k.astype(BF16)
    vd_ref[...] = dv.astype(BF16)

    cqn = _rms_full(cq, gcq_ref[...])
    qm = jnp.dot(cqn.astype(BF16), wuq_ref[...], preferred_element_type=F32)
    qm = _rms_padded_heads(qm, gmq_ref[...])
    ckvn = _rms_full(ckv, gckv_ref[...])
    if states:
        sckv_ref[...] = ckvn
        skpe_ref[...] = kpe2[:, :MLA_ROPE]
    kv = jnp.dot(ckvn.astype(BF16), wukv_ref[...], preferred_element_type=F32)
    lane = lax.broadcasted_iota(jnp.int32, (1, LANES), 1)
    kpe_hi = jnp.where(lane >= MLA_NOPE, kpe2, 0.0)
    km = kv[:, :MLA_PAD] + _tile_lanes(kpe_hi, MLA_HEADS)
    km = _rms_padded_heads(km, gmk_ref[...])
    if rope:
        cm = _tile_lanes(cm_ref[...], MLA_HEADS)
        sm = _tile_lanes(sm_ref[...], MLA_HEADS)
        qm = _rope(qm, cm, sm, MLA_ROPE // 4)
        km = _rope(km, cm, sm, MLA_ROPE // 4)
    qm_ref[...] = (qm * (MLA_QK ** -0.5)).astype(BF16)
    km_ref[...] = km.astype(BF16)
    vm_ref[...] = kv[:, MLA_PAD:].astype(BF16)


def _inproj(x, mod_l, W, rope_tabs, *, tiles_per_batch, row0, states):
    t = x.shape[0]
    tm = TOK_TILE
    rope = rope_tabs is not None
    row = lambda i: (i, 0)
    const = lambda i: (0, 0)
    wcols = W["w1"].shape[1]
    in_specs = [
        pl.BlockSpec((tm, D_MODEL), row),
        pl.BlockSpec((None, 6, D_MODEL), lambda i: (row0 + i // tiles_per_batch, 0, 0)),
        pl.BlockSpec((1, D_MODEL), const),
        pl.BlockSpec((D_MODEL, wcols), const),
        pl.BlockSpec((1, DIFF_W), const),
        pl.BlockSpec((1, DIFF_W), const),
        pl.BlockSpec((1, MLA_Q_RANK), const),
        pl.BlockSpec((1, MLA_KV_RANK), const),
        pl.BlockSpec((MLA_Q_RANK, MLA_PAD), const),
        pl.BlockSpec((MLA_KV_RANK, MLA_PAD + MLA_W), const),
        pl.BlockSpec((1, MLA_PAD), const),
        pl.BlockSpec((1, MLA_PAD), const),
    ]
    args = [x, mod_l, W["norm1_g"], W["w1"], W["gq"], W["gk"], W["gcq"], W["gckv"],
            W["wuq"], W["wukv"], W["gmq"], W["gmk"]]
    if rope:
        tab = lambda i: (i % tiles_per_batch, 0)
        in_specs += [pl.BlockSpec((tm, LANES), tab)] * 4
        args += list(rope_tabs)
    out_shape = [
        jax.ShapeDtypeStruct((t, CONV_W), F32),
        jax.ShapeDtypeStruct((t, DIFF_W), BF16),
        jax.ShapeDtypeStruct((t, DIFF_W), BF16),
        jax.ShapeDtypeStruct((t, DIFF_W), BF16),
        jax.ShapeDtypeStruct((t, MLA_PAD), BF16),
        jax.ShapeDtypeStruct((t, MLA_PAD), BF16),
        jax.ShapeDtypeStruct((t, MLA_W), BF16),
    ]
    if states:
        out_shape += [
            jax.ShapeDtypeStruct((t, DIFF_W), F32),
            jax.ShapeDtypeStruct((t, DIFF_W), F32),
            jax.ShapeDtypeStruct((t, MLA_KV_RANK), F32),
            jax.ShapeDtypeStruct((t, MLA_ROPE), F32),
        ]
    out_specs = [pl.BlockSpec((tm, s.shape[1]), row) for s in out_shape]
    return pl.pallas_call(
        functools.partial(_inproj_kernel, rope=rope, states=states),
        grid=(t // tm,),
        in_specs=in_specs,
        out_specs=out_specs,
        out_shape=out_shape,
        compiler_params=pltpu.CompilerParams(
            dimension_semantics=("parallel",), vmem_limit_bytes=VMEM_LIMIT),
        name="inproj_rope" if rope else "inproj",
    )(*args)


def _cache_kernel(ck_ref, cv_ref, ckv_ref, kpe_ref, wukv_ref, gmk_ref, kd_ref, vd_ref, km_ref, vm_ref):
    kd_ref[...] = ck_ref[...].astype(BF16)
    vd_ref[...] = cv_ref[...].astype(BF16)
    kv = jnp.dot(ckv_ref[...].astype(BF16), wukv_ref[...], preferred_element_type=F32)
    km = kv[:, :MLA_PAD] + _tile_lanes(kpe_ref[...], MLA_HEADS)
    km_ref[...] = _rms_padded_heads(km, gmk_ref[...]).astype(BF16)
    vm_ref[...] = kv[:, MLA_PAD:].astype(BF16)


def _cache_prep(ck, cv, cckv, ckpe_pad, W):
    t = ck.shape[0]
    tm = TOK_TILE
    row = lambda i: (i, 0)
    const = lambda i: (0, 0)
    out_shape = [
        jax.ShapeDtypeStruct((t, DIFF_W), BF16),
        jax.ShapeDtypeStruct((t, DIFF_W), BF16),
        jax.ShapeDtypeStruct((t, MLA_PAD), BF16),
        jax.ShapeDtypeStruct((t, MLA_W), BF16),
    ]
    return pl.pallas_call(
        _cache_kernel,
        grid=(t // tm,),
        in_specs=[
            pl.BlockSpec((tm, DIFF_W), row),
            pl.BlockSpec((tm, DIFF_W), row),
            pl.BlockSpec((tm, MLA_KV_RANK), row),
            pl.BlockSpec((tm, LANES), row),
            pl.BlockSpec((MLA_KV_RANK, MLA_PAD + MLA_W), const),
            pl.BlockSpec((1, MLA_PAD), const),
        ],
        out_specs=[pl.BlockSpec((tm, s.shape[1]), row) for s in out_shape],
        out_shape=out_shape,
        compiler_params=pltpu.CompilerParams(
            dimension_semantics=("parallel",), vmem_limit_bytes=VMEM_LIMIT),
        name="cache_prep",
    )(ck, cv, cckv, ckpe_pad, W["wukv"], W["gmk"])


def _conv_kernel(*refs, tiles_per_seq):
    if tiles_per_seq > 1:
        u_ref, up_ref, un_ref, dw_ref, b_ref, g_ref, o_ref, buf_ref = refs
    else:
        u_ref, dw_ref, b_ref, g_ref, o_ref, buf_ref = refs
    tm = u_ref.shape[0]
    zeros = jnp.zeros((HALO, CONV_W), F32)
    if tiles_per_seq > 1:
        j = pl.program_id(0) % tiles_per_seq
        prev = jnp.where(j == 0, zeros, up_ref[...])
        nxt = jnp.where(j == tiles_per_seq - 1, zeros, un_ref[...])
    else:
        prev, nxt = zeros, zeros
    buf_ref[0:HALO, :] = prev
    buf_ref[HALO:HALO + tm, :] = u_ref[...]
    buf_ref[HALO + tm:HALO + tm + HALO, :] = nxt
    base = HALO - CONV_K // 2
    acc = jnp.zeros((tm, CONV_W), F32)
    for k in range(CONV_K):
        acc = acc + buf_ref[base + k:base + k + tm, :] * dw_ref[k:k + 1, :]
    y = acc + b_ref[...]
    o_ref[...] = _silu(_rms_full(y, g_ref[...])).astype(BF16)


def _conv_module(u, W, *, tiles_per_seq):
    t = u.shape[0]
    tm = TOK_TILE
    nb = t // HALO
    per = tm // HALO
    row = lambda i: (i, 0)
    const = lambda i: (0, 0)
    in_specs = [pl.BlockSpec((tm, CONV_W), row)]
    args = [u]
    if tiles_per_seq > 1:
        in_specs += [
            pl.BlockSpec((HALO, CONV_W), lambda i: (jnp.maximum(i * per - 1, 0), 0)),
            pl.BlockSpec((HALO, CONV_W), lambda i: (jnp.minimum((i + 1) * per, nb - 1), 0)),
        ]
        args += [u, u]
    in_specs += [
        pl.BlockSpec((CONV_K + 1, CONV_W), const),
        pl.BlockSpec((1, CONV_W), const),
        pl.BlockSpec((1, CONV_W), const),
    ]
    args += [W["conv_dw"], W["conv_b"], W["conv_g"]]
    return pl.pallas_call(
        functools.partial(_conv_kernel, tiles_per_seq=tiles_per_seq),
        grid=(t // tm,),
        in_specs=in_specs,
        out_specs=pl.BlockSpec((tm, CONV_W), row),
        out_shape=jax.ShapeDtypeStruct((t, CONV_W), BF16),
        scratch_shapes=[pltpu.VMEM((tm + 2 * HALO, CONV_W), F32)],
        compiler_params=pltpu.CompilerParams(
            dimension_semantics=("parallel",), vmem_limit_bytes=VMEM_LIMIT),
        name="conv_module",
    )(*args)


def _softmax_pv(q, k, v):
    s = lax.dot_general(q, k, (((1,), (1,)), ((), ())), preferred_element_type=F32)
    m = jnp.max(s, axis=-1, keepdims=True)
    p = jnp.exp(s - m)
    l = jnp.sum(p, axis=-1, keepdims=True)
    return jnp.dot(p.astype(BF16), v, preferred_element_type=F32) / l


def _attn_kernel(qd_ref, qm_ref, kd_ref, vd_ref, km_ref, vm_ref, lam_ref, sub_ref, od_ref, om_ref, *, lam_init):
    lp = lam_ref[...]
    lam = (jnp.exp(jnp.sum(lp[0:1] * lp[1:2], axis=-1, keepdims=True))
           - jnp.exp(jnp.sum(lp[2:3] * lp[3:4], axis=-1, keepdims=True)) + lam_init)
    lo = lax.broadcasted_iota(jnp.int32, (1, LANES), 1) < DIFF_HD
    subg = sub_ref[...]
    for h in range(DIFF_HEADS):
        sl = slice(h * LANES, (h + 1) * LANES)
        q = qd_ref[:, sl].astype(F32)
        k = kd_ref[:, sl]
        v = vd_ref[:, sl]
        a1 = _softmax_pv(jnp.where(lo, q, 0.0).astype(BF16), k, v)
        a2 = _softmax_pv(jnp.where(lo, 0.0, q).astype(BF16), k, v)
        a = a1 - lam * a2
        od = _rms_full(a, subg) * (1.0 - lam_init)
        od_ref[:, sl] = od.astype(BF16)
    for hp in range(MLA_HEADS // 2):
        sl = slice(hp * LANES, (hp + 1) * LANES)
        vpair = vm_ref[:, sl]
        o = []
        for j in range(2):
            hs = slice((2 * hp + j) * LANES, (2 * hp + j + 1) * LANES)
            o.append(_softmax_pv(qm_ref[:, hs], km_ref[:, hs], vpair))
        om_ref[:, sl] = jnp.where(lo, o[0], o[1]).astype(BF16)


def _attention(qd, qm, kd, vd, km, vm, lam_p, subg, *, batch, lam_init):
    sq = qd.shape[0] // batch
    sk = kd.shape[0] // batch
    tq = TOK_TILE
    nq = sq // tq
    qrow = lambda b, i: (b * nq + i, 0)
    krow = lambda b, i: (b, 0)
    const = lambda b, i: (0, 0)
    return pl.pallas_call(
        functools.partial(_attn_kernel, lam_init=lam_init),
        grid=(batch, nq),
        in_specs=[
            pl.BlockSpec((tq, DIFF_W), qrow),
            pl.BlockSpec((tq, MLA_PAD), qrow),
            pl.BlockSpec((sk, DIFF_W), krow),
            pl.BlockSpec((sk, DIFF_W), krow),
            pl.BlockSpec((sk, MLA_PAD), krow),
            pl.BlockSpec((sk, MLA_W), krow),
            pl.BlockSpec((4, DIFF_HD), const),
            pl.BlockSpec((1, 2 * DIFF_HD), const),
        ],
        out_specs=[pl.BlockSpec((tq, DIFF_W), qrow), pl.BlockSpec((tq, MLA_W), qrow)],
        out_shape=[jax.ShapeDtypeStruct((batch * sq, DIFF_W), BF16),
                   jax.ShapeDtypeStruct((batch * sq, MLA_W), BF16)],
        compiler_params=pltpu.CompilerParams(
            dimension_semantics=("parallel", "arbitrary"), vmem_limit_bytes=VMEM_LIMIT),
        name="attention",
    )(qd, qm, kd, vd, km, vm, lam_p, subg)


def _merge_mlp_kernel(x_ref, mod_ref, g1_ref, g2_ref, c_ref, od_ref, om_ref, wg_ref, wc_ref, wd_ref, wm_ref,
                      wo_ref, wup_ref, wdn_ref, y_ref):
    mod = mod_ref[...]
    x = x_ref[...]
    h = _modulated_norm(x, g1_ref[...], mod[0:1], mod[1:2])
    gates = jax.nn.sigmoid(jnp.dot(h.astype(BF16), wg_ref[...], preferred_element_type=F32))
    o_conv = jnp.dot(c_ref[...], wc_ref[...], preferred_element_type=F32)
    o_diff = jnp.dot(od_ref[...], wd_ref[...], preferred_element_type=F32)
    o_mla = jnp.dot(om_ref[...], wm_ref[...], preferred_element_type=F32)
    merged = (gates[:, 0:D_MODEL] * o_conv + gates[:, D_MODEL:2 * D_MODEL] * o_diff
              + gates[:, 2 * D_MODEL:3 * D_MODEL] * o_mla)
    x = x + mod[2:3] * jnp.dot(merged.astype(BF16), wo_ref[...], preferred_element_type=F32)
    h2 = _modulated_norm(x, g2_ref[...], mod[3:4], mod[4:5])
    up = jnp.dot(h2.astype(BF16), wup_ref[...], preferred_element_type=F32)
    act = jnp.square(jnp.maximum(up, 0.0)).astype(BF16)
    y_ref[...] = x + mod[5:6] * jnp.dot(act, wdn_ref[...], preferred_element_type=F32)


def _merge_mlp(x, mod_l, c, od, om, W, *, tiles_per_batch, row0):
    t = x.shape[0]
    tm = TOK_TILE
    row = lambda i: (i, 0)
    const = lambda i: (0, 0)

    def wspec(shape):
        return pl.BlockSpec(shape, const, pipeline_mode=pl.Buffered(1))

    return pl.pallas_call(
        _merge_mlp_kernel,
        grid=(t // tm,),
        in_specs=[
            pl.BlockSpec((tm, D_MODEL), row),
            pl.BlockSpec((None, 6, D_MODEL), lambda i: (row0 + i // tiles_per_batch, 0, 0)),
            pl.BlockSpec((1, D_MODEL), const),
            pl.BlockSpec((1, D_MODEL), const),
            pl.BlockSpec((tm, CONV_W), row),
            pl.BlockSpec((tm, DIFF_W), row),
            pl.BlockSpec((tm, MLA_W), row),
            wspec((D_MODEL, 3 * D_MODEL)),
            wspec((CONV_W, D_MODEL)),
            wspec((DIFF_W, D_MODEL)),
            wspec((MLA_W, D_MODEL)),
            wspec((D_MODEL, D_MODEL)),
            wspec((D_MODEL, D_FF)),
            wspec((D_FF, D_MODEL)),
        ],
        out_specs=pl.BlockSpec((tm, D_MODEL), row),
        out_shape=jax.ShapeDtypeStruct((t, D_MODEL), F32),
        compiler_params=pltpu.CompilerParams(
            dimension_semantics=("parallel",), vmem_limit_bytes=VMEM_LIMIT),
        name="merge_mlp",
    )(x, mod_l, W["norm1_g"], W["norm2_g"], c, od, om, W["wg"], W["w_conv_out"], W["w_diff_out"],
      W["w_mla_out"], W["w_out"], W["w_up"], W["w_down"])


def _rope_tables(seq_len):
    rows = seq_len // GRID_W
    row = jnp.repeat(jnp.arange(rows, dtype=F32), GRID_W)
    col = jnp.tile(jnp.arange(GRID_W, dtype=F32), rows)

    def tables(rot_dim):
        half = rot_dim // 2
        qd = rot_dim // 4
        inv = ROPE_BASE ** (-jnp.arange(0, half, 2, dtype=F32) / half)
        a0 = row[:, None] * inv
        a1 = col[:, None] * inv
        cos = jnp.concatenate([jnp.cos(a0), jnp.cos(a0), jnp.cos(a1), jnp.cos(a1)], axis=1)
        sin = jnp.concatenate([-jnp.sin(a0), jnp.sin(a0), -jnp.sin(a1), jnp.sin(a1)], axis=1)
        del qd
        return cos, sin

    cd, sd = tables(DIFF_HD)
    cd, sd = jnp.tile(cd, (1, LANES // DIFF_HD)), jnp.tile(sd, (1, LANES // DIFF_HD))
    cm, sm = tables(MLA_ROPE)
    ones = jnp.ones((seq_len, MLA_NOPE), F32)
    zeros = jnp.zeros((seq_len, MLA_NOPE), F32)
    pad = LANES - MLA_QK
    cm = jnp.concatenate([ones, cm, jnp.ones((seq_len, pad), F32)], axis=1)
    sm = jnp.concatenate([zeros, sm, jnp.zeros((seq_len, pad), F32)], axis=1)
    return cd, sd, cm, sm


def _pad_heads(v):
    return jnp.pad(v, [(0, 0)] * (v.ndim - 1) + [(0, LANES - MLA_QK)])


def _layer_weights(l, norm1_g, w_in, conv_dw, conv_b, conv_norm_g, w_conv_out, diff_q_norm, diff_k_norm,
                   diff_lambda, diff_subln, w_diff_out, mla_q_a_norm, mla_kv_a_norm, w_uq, w_ukv,
                   mla_q_norm, mla_k_norm, w_mla_out, w_out, norm2_g, w_up, w_down):
    wi = w_in[l]
    kpe_cols = wi[:, N_MAIN:N_MAIN + MLA_ROPE]
    zc = jnp.zeros((D_MODEL, MLA_ROPE), F32)
    w1 = jnp.concatenate([wi[:, :N_MAIN], kpe_cols, zc, kpe_cols, zc], axis=1).astype(BF16)
    wuq = _pad_heads(w_uq[l].reshape(MLA_Q_RANK, MLA_HEADS, MLA_QK)).reshape(MLA_Q_RANK, MLA_PAD)
    wukv3 = w_ukv[l].reshape(MLA_KV_RANK, MLA_HEADS, MLA_NOPE + MLA_V)
    wkn = jnp.pad(wukv3[:, :, :MLA_NOPE], ((0, 0), (0, 0), (0, LANES - MLA_NOPE))).reshape(MLA_KV_RANK, MLA_PAD)
    wv = wukv3[:, :, MLA_NOPE:].reshape(MLA_KV_RANK, MLA_W)
    return dict(
        norm1_g=norm1_g[l][None], norm2_g=norm2_g[l][None],
        w1=w1, wg=wi[:, N_MAIN + MLA_ROPE:].astype(BF16),
        gq=jnp.tile(diff_q_norm[l], DIFF_W // DIFF_HD)[None],
        gk=jnp.tile(diff_k_norm[l], DIFF_W // DIFF_HD)[None],
        gcq=mla_q_a_norm[l][None], gckv=mla_kv_a_norm[l][None],
        wuq=wuq.astype(BF16), wukv=jnp.concatenate([wkn, wv], axis=1).astype(BF16),
        gmq=jnp.tile(_pad_heads(mla_q_norm[l]), MLA_HEADS)[None],
        gmk=jnp.tile(_pad_heads(mla_k_norm[l]), MLA_HEADS)[None],
        conv_dw=jnp.pad(conv_dw[l], ((0, 1), (0, 0))), conv_b=conv_b[l][None], conv_g=conv_norm_g[l][None],
        lam=diff_lambda[l], subg=diff_subln[l][None],
        w_conv_out=w_conv_out[l].astype(BF16), w_diff_out=w_diff_out[l].astype(BF16),
        w_mla_out=w_mla_out[l].astype(BF16), w_out=w_out[l].astype(BF16),
        w_up=w_up[l].astype(BF16), w_down=w_down[l].astype(BF16),
    )


def kernel(x_prompt, x_sample, cache_diff_k, cache_diff_v, cache_mla_ckv, cache_mla_kpe, c, c_ctx, mod_w, mod_b, norm1_g, w_in, conv_dw, conv_b, conv_norm_g, w_conv_out, diff_q_norm, diff_k_norm, diff_lambda, diff_subln, w_diff_out, mla_q_a_norm, mla_kv_a_norm, w_uq, w_ukv, mla_q_norm, mla_k_norm, w_mla_out, w_out, norm2_g, w_up, w_down):
    batch, seq, _ = x_prompt.shape
    dec_batch, dec_seq, _ = x_sample.shape
    past = cache_diff_k.shape[2]
    assert seq == TOK_TILE and dec_seq % TOK_TILE == 0 and past % TOK_TILE == 0 and 1 + dec_batch <= 8

    cond_rows = jnp.concatenate([c_ctx[None], c, jnp.zeros((8 - 1 - dec_batch, D_MODEL), F32)], axis=0)
    mod = _modulation(cond_rows, mod_w, mod_b).reshape(DEPTH, 8, 6, D_MODEL)
    tabs = _rope_tables(dec_seq)

    xp = x_prompt.reshape(batch * seq, D_MODEL)
    xs = x_sample.reshape(dec_batch * dec_seq, D_MODEL)
    st_k, st_v, st_ckv, st_kpe = [], [], [], []
    for l in range(DEPTH):
        W = _layer_weights(l, norm1_g, w_in, conv_dw, conv_b, conv_norm_g, w_conv_out, diff_q_norm,
                           diff_k_norm, diff_lambda, diff_subln, w_diff_out, mla_q_a_norm, mla_kv_a_norm,
                           w_uq, w_ukv, mla_q_norm, mla_k_norm, w_mla_out, w_out, norm2_g, w_up, w_down)
        lam_init = 0.8 - 0.6 * math.exp(-0.3 * l)

        u, qd, kd, vd, qm, km, vm, sk, sv, sckv, skpe = _inproj(
            xp, mod[l], W, None, tiles_per_batch=batch * seq // TOK_TILE, row0=0, states=True)
        st_k.append(sk.reshape(batch, seq, DIFF_HEADS, 2 * DIFF_HD))
        st_v.append(sv.reshape(batch, seq, DIFF_HEADS, 2 * DIFF_HD))
        st_ckv.append(sckv.reshape(batch, seq, MLA_KV_RANK))
        st_kpe.append(skpe.reshape(batch, seq, MLA_ROPE))
        cc = _conv_module(u, W, tiles_per_seq=1)
        od, om = _attention(qd, qm, kd, vd, km, vm, W["lam"], W["subg"], batch=batch, lam_init=lam_init)
        xp = _merge_mlp(xp, mod[l], cc, od, om, W, tiles_per_batch=batch * seq // TOK_TILE, row0=0)

        u, qd, kd, vd, qm, km, vm = _inproj(
            xs, mod[l], W, tabs, tiles_per_batch=dec_seq // TOK_TILE, row0=1, states=False)
        ckpe_pad = jnp.pad(cache_mla_kpe[:, l].reshape(dec_batch * past, MLA_ROPE),
                           ((0, 0), (MLA_NOPE, LANES - MLA_QK)))
        ckd, cvd, ckm, cvm = _cache_prep(
            cache_diff_k[:, l].reshape(dec_batch * past, DIFF_W),
            cache_diff_v[:, l].reshape(dec_batch * past, DIFF_W),
            cache_mla_ckv[:, l].reshape(dec_batch * past, MLA_KV_RANK), ckpe_pad, W)

        def with_cache(cached, new):
            w = new.shape[-1]
            return jnp.concatenate([cached.reshape(dec_batch, past, w), new.reshape(dec_batch, dec_seq, w)],
                                   axis=1).reshape(dec_batch * (past + dec_seq), w)

        cc = _conv_module(u, W, tiles_per_seq=dec_seq // TOK_TILE)
        od, om = _attention(qd, qm, with_cache(ckd, kd), with_cache(cvd, vd), with_cache(ckm, km),
                            with_cache(cvm, vm), W["lam"], W["subg"], batch=dec_batch, lam_init=lam_init)
        xs = _merge_mlp(xs, mod[l], cc, od, om, W, tiles_per_batch=dec_seq // TOK_TILE, row0=1)

    return (xp.reshape(batch, seq, D_MODEL), xs.reshape(dec_batch, dec_seq, D_MODEL),
            jnp.stack(st_k, axis=1), jnp.stack(st_v, axis=1), jnp.stack(st_ckv, axis=1), jnp.stack(st_kpe, axis=1))
```

```python
import functools
import math

import numpy as np
import jax
import jax.numpy as jnp
from jax import lax
from jax.experimental import pallas as pl
from jax.experimental.pallas import tpu as pltpu

D_MODEL = 1024
DEPTH = 2
GRID_W = 64
ROPE_BASE = 10000.0
CONV_W = 512
CONV_K = 31
DIFF_HEADS = 4
DIFF_HD = 64
DIFF_W = DIFF_HEADS * 2 * DIFF_HD
MLA_HEADS = 8
MLA_NOPE = 64
MLA_ROPE = 32
MLA_QK = MLA_NOPE + MLA_ROPE
MLA_V = 64
MLA_Q_RANK = 384
MLA_KV_RANK = 256
MLA_W = MLA_HEADS * MLA_V
D_FF = 4 * D_MODEL
EPS = 1e-6

LANES = 128
SUBLANES = 8
MLA_PAD = MLA_HEADS * LANES
TOK_TILE = 256
HALO = 16
N_MAIN = 2 * CONV_W + 3 * DIFF_W + MLA_Q_RANK + MLA_KV_RANK
VMEM_LIMIT = 56 * 1024 * 1024
LOG2E = math.log2(math.e)

F32 = jnp.float32
BF16 = jnp.bfloat16


def _silu(x):
    return x * jax.nn.sigmoid(x)


def _rms_full(x, g):
    ms = jnp.mean(x * x, axis=-1, keepdims=True)
    return x * lax.rsqrt(ms + EPS) * g


def _rms_half_groups(x, g):
    lo = lax.broadcasted_iota(jnp.int32, (1, LANES), 1) < DIFF_HD
    outs = []
    for h in range(x.shape[-1] // LANES):
        xh = x[:, h * LANES:(h + 1) * LANES]
        x2 = xh * xh
        s_lo = jnp.sum(jnp.where(lo, x2, 0.0), axis=-1, keepdims=True)
        s_hi = jnp.sum(jnp.where(lo, 0.0, x2), axis=-1, keepdims=True)
        r = jnp.where(lo, lax.rsqrt(s_lo * (1.0 / DIFF_HD) + EPS), lax.rsqrt(s_hi * (1.0 / DIFF_HD) + EPS))
        outs.append(xh * r)
    return jnp.concatenate(outs, axis=-1) * g


def _rms_padded_heads(x, g):
    outs = []
    for h in range(x.shape[-1] // LANES):
        xh = x[:, h * LANES:(h + 1) * LANES]
        ss = jnp.sum(xh * xh, axis=-1, keepdims=True)
        outs.append(xh * lax.rsqrt(ss * (1.0 / MLA_QK) + EPS))
    return jnp.concatenate(outs, axis=-1) * g


def _rope(x, cos, sin_signed, half):
    w = x.shape[-1]
    lane = lax.broadcasted_iota(jnp.int32, (1, w), 1)
    first = (lane & (2 * half - 1)) < half
    sw = jnp.where(first, pltpu.roll(x, w - half, axis=1), pltpu.roll(x, half, axis=1))
    return x * cos + sw * sin_signed


def _tile_lanes(t, n):
    return jnp.concatenate([t] * n, axis=-1)


def _modulated_norm(x, g, shift, scale):
    return _rms_full(x, g) * (1.0 + scale) + shift


def _mod_kernel(s_ref, w_ref, b_ref, o_ref):
    s = _silu(s_ref[...]).astype(BF16)
    o_ref[...] = jnp.dot(s, w_ref[...].astype(BF16), preferred_element_type=F32) + b_ref[...]


def _modulation(cond_rows, mod_w, mod_b):
    tn = 1536
    n = 6 * D_MODEL
    return pl.pallas_call(
        _mod_kernel,
        grid=(DEPTH, n // tn),
        in_specs=[
            pl.BlockSpec((8, D_MODEL), lambda l, j: (0, 0)),
            pl.BlockSpec((None, D_MODEL, tn), lambda l, j: (l, 0, j)),
            pl.BlockSpec((None, 1, tn), lambda l, j: (l, 0, j)),
        ],
        out_specs=pl.BlockSpec((None, 8, tn), lambda l, j: (l, 0, j)),
        out_shape=jax.ShapeDtypeStruct((DEPTH, 8, n), F32),
        compiler_params=pltpu.CompilerParams(
            dimension_semantics=("parallel", "parallel"), vmem_limit_bytes=VMEM_LIMIT),
        name="modulation",
    )(cond_rows, mod_w, mod_b.reshape(DEPTH, 1, n))


def _mla_keys_values(ckvn, kpe_hi, wukv_ref, gmk_ref):
    kv = jnp.dot(ckvn.astype(BF16), wukv_ref[...], preferred_element_type=F32)
    km = kv[:, :MLA_PAD] + _tile_lanes(kpe_hi, MLA_HEADS)
    return _rms_padded_heads(km, gmk_ref[...]), kv[:, MLA_PAD:]


def _inproj_kernel(*refs, rope, states, cache):
    it = iter(refs)
    x_ref, mod_ref, g1_ref, w_ref = next(it), next(it), next(it), next(it)
    gq_ref, gk_ref, gcq_ref, gckv_ref = next(it), next(it), next(it), next(it)
    wuq_ref, wukv_ref, gmq_ref, gmk_ref = next(it), next(it), next(it), next(it)
    if rope:
        cd_ref, sd_ref, cm_ref, sm_ref = next(it), next(it), next(it), next(it)
    if cache:
        ck_ref, cv_ref, cckv_ref, ckpe_ref = next(it), next(it), next(it), next(it)
    u_ref, qd_ref, qm_ref, kd_ref, km_ref, vdt_ref, vmt_ref = (next(it) for _ in range(7))
    if states:
        sk_ref, sv_ref, sckv_ref, skpe_ref = next(it), next(it), next(it), next(it)

    def tokens():
        mod = mod_ref[...]
        h = _modulated_norm(x_ref[...], g1_ref[...], mod[0:1], mod[1:2])
        proj = jnp.dot(h.astype(BF16), w_ref[...], preferred_element_type=F32)

        o = 0
        u_a = proj[:, o:o + CONV_W]; o += CONV_W
        u_g = proj[:, o:o + CONV_W]; o += CONV_W
        dq = proj[:, o:o + DIFF_W]; o += DIFF_W
        dk = proj[:, o:o + DIFF_W]; o += DIFF_W
        dv = proj[:, o:o + DIFF_W]; o += DIFF_W
        cq = proj[:, o:o + MLA_Q_RANK]; o += MLA_Q_RANK
        ckv = proj[:, o:o + MLA_KV_RANK]; o += MLA_KV_RANK
        kpe2 = proj[:, o:o + LANES]

        u_ref[...] = u_a * jax.nn.sigmoid(u_g)

        q = _rms_half_groups(dq, gq_ref[...])
        k = _rms_half_groups(dk, gk_ref[...])
        if states:
            sk_ref[...] = k
            sv_ref[...] = dv
        if rope:
            cd = _tile_lanes(cd_ref[...], DIFF_W // LANES)
            sd = _tile_lanes(sd_ref[...], DIFF_W // LANES)
            q = _rope(q, cd, sd, DIFF_HD // 4)
            k = _rope(k, cd, sd, DIFF_HD // 4)
        qd_ref[...] = (q * (DIFF_HD ** -0.5 * LOG2E)).astype(BF16)
        kd_ref[...] = k.astype(BF16)
        vdt_ref[...] = dv.T.astype(BF16)

        cqn = _rms_full(cq, gcq_ref[...])
        qm = jnp.dot(cqn.astype(BF16), wuq_ref[...], preferred_element_type=F32)
        qm = _rms_padded_heads(qm, gmq_ref[...])
        ckvn = _rms_full(ckv, gckv_ref[...])
        if states:
            sckv_ref[...] = ckvn
            skpe_ref[...] = kpe2[:, :MLA_ROPE]
        lane = lax.broadcasted_iota(jnp.int32, (1, LANES), 1)
        km, vm = _mla_keys_values(ckvn, jnp.where(lane >= MLA_NOPE, kpe2, 0.0), wukv_ref, gmk_ref)
        if rope:
            cm = _tile_lanes(cm_ref[...], MLA_HEADS)
            sm = _tile_lanes(sm_ref[...], MLA_HEADS)
            qm = _rope(qm, cm, sm, MLA_ROPE // 4)
            km = _rope(km, cm, sm, MLA_ROPE // 4)
        qm_ref[...] = (qm * (MLA_QK ** -0.5 * LOG2E)).astype(BF16)
        km_ref[...] = km.astype(BF16)
        vmt_ref[...] = vm.T.astype(BF16)

    def cached_context():
        kd_ref[...] = ck_ref[...].astype(BF16)
        vdt_ref[...] = cv_ref[...].T.astype(BF16)
        km, vm = _mla_keys_values(cckv_ref[...], ckpe_ref[...], wukv_ref, gmk_ref)
        km_ref[...] = km.astype(BF16)
        vmt_ref[...] = vm.T.astype(BF16)

    if cache:
        j = pl.program_id(1)
        pl.when(j == 0)(cached_context)
        pl.when(j > 0)(tokens)
    else:
        tokens()


def _inproj(x, mod_l, W, rope_tabs, cache, *, batch, row0, states):
    t = x.shape[0]
    tm = TOK_TILE
    n = t // batch // tm
    rope = rope_tabs is not None
    has_cache = cache is not None
    nk = n + 1 if has_cache else n
    if has_cache:
        grid = (batch, nk)
        tok = lambda b, j: (b * n + jnp.maximum(j - 1, 0), 0)
        key = lambda b, j: (b * nk + j, 0)
        keyt = lambda b, j: (b, 0, j)
        const = lambda b, j: (0, 0)
        modrow = lambda b, j: (row0 + b, 0, 0)
        tab = lambda b, j: (jnp.maximum(j - 1, 0), 0)
        cblk = lambda b, j: (b, 0)
    else:
        grid = (batch * n,)
        tok = lambda i: (i, 0)
        key = tok
        keyt = lambda i: (i // n, 0, i % n)
        const = lambda i: (0, 0)
        modrow = lambda i: (row0 + (i // n if row0 else 0), 0, 0)
        tab = lambda i: (i % n, 0)
    wcols = W["w1"].shape[1]
    in_specs = [
        pl.BlockSpec((tm, D_MODEL), tok),
        pl.BlockSpec((None, 6, D_MODEL), modrow),
        pl.BlockSpec((1, D_MODEL), const),
        pl.BlockSpec((D_MODEL, wcols), const),
        pl.BlockSpec((1, DIFF_W), const),
        pl.BlockSpec((1, DIFF_W), const),
        pl.BlockSpec((1, MLA_Q_RANK), const),
        pl.BlockSpec((1, MLA_KV_RANK), const),
        pl.BlockSpec((MLA_Q_RANK, MLA_PAD), const),
        pl.BlockSpec((MLA_KV_RANK, MLA_PAD + MLA_W), const),
        pl.BlockSpec((1, MLA_PAD), const),
        pl.BlockSpec((1, MLA_PAD), const),
    ]
    args = [x, mod_l, W["norm1_g"], W["w1"], W["gq"], W["gk"], W["gcq"], W["gckv"],
            W["wuq"], W["wukv"], W["gmq"], W["gmk"]]
    if rope:
        in_specs += [pl.BlockSpec((tm, LANES), tab)] * 4
        args += list(rope_tabs)
    if has_cache:
        in_specs += [pl.BlockSpec((tm, DIFF_W), cblk), pl.BlockSpec((tm, DIFF_W), cblk),
                     pl.BlockSpec((tm, MLA_KV_RANK), cblk), pl.BlockSpec((tm, LANES), cblk)]
        args += list(cache)
    sk = nk * tm
    out_shape = [
        jax.ShapeDtypeStruct((t, CONV_W), F32),
        jax.ShapeDtypeStruct((t, DIFF_W), BF16),
        jax.ShapeDtypeStruct((t, MLA_PAD), BF16),
        jax.ShapeDtypeStruct((batch * sk, DIFF_W), BF16),
        jax.ShapeDtypeStruct((batch * sk, MLA_PAD), BF16),
        jax.ShapeDtypeStruct((batch, DIFF_W, sk), BF16),
        jax.ShapeDtypeStruct((batch, MLA_W, sk), BF16),
    ]
    out_specs = [
        pl.BlockSpec((tm, CONV_W), tok),
        pl.BlockSpec((tm, DIFF_W), tok),
        pl.BlockSpec((tm, MLA_PAD), tok),
        pl.BlockSpec((tm, DIFF_W), key),
        pl.BlockSpec((tm, MLA_PAD), key),
        pl.BlockSpec((None, DIFF_W, tm), keyt),
        pl.BlockSpec((None, MLA_W, tm), keyt),
    ]
    if states:
        widths = (DIFF_W, DIFF_W, MLA_KV_RANK, MLA_ROPE)
        out_shape += [jax.ShapeDtypeStruct((t, w), F32) for w in widths]
        out_specs += [pl.BlockSpec((tm, w), tok) for w in widths]
    return pl.pallas_call(
        functools.partial(_inproj_kernel, rope=rope, states=states, cache=has_cache),
        grid=grid,
        in_specs=in_specs,
        out_specs=out_specs,
        out_shape=out_shape,
        compiler_params=pltpu.CompilerParams(
            dimension_semantics=("arbitrary",) * len(grid), vmem_limit_bytes=VMEM_LIMIT),
        name="inproj_latent" if has_cache else "inproj_context",
    )(*args)


def _fold_rows(x, op):
    n = x.shape[0]
    while n % (2 * SUBLANES) == 0 and n > 4 * SUBLANES:
        n //= 2
        x = op(x[:n], x[n:])
    return x


KEY_CHUNK = 256


def _attn_kernel(qd_ref, qm_ref, kd_ref, km_ref, vdt_ref, vmt_ref, lam_ref, sub_ref, od_ref, om_ref, s_ref, p_ref,
                 *, lam_init):
    lp = lam_ref[...]
    lam = (jnp.exp(jnp.sum(lp[0:1] * lp[1:2], axis=-1, keepdims=True))
           - jnp.exp(jnp.sum(lp[2:3] * lp[3:4], axis=-1, keepdims=True)) + lam_init)
    lo = lax.broadcasted_iota(jnp.int32, (1, LANES), 1) < DIFF_HD
    top = lax.broadcasted_iota(jnp.int32, (LANES, 1), 0) < MLA_V
    subg = sub_ref[...]
    sk = kd_ref.shape[0]
    kc = min(KEY_CHUNK, sk)
    n_chunks = sk // kc

    jobs = []
    for h in range(DIFF_HEADS):
        sl = slice(h * LANES, (h + 1) * LANES)
        jobs += [(kd_ref, sl, qd_ref, 0, vdt_ref, sl), (kd_ref, sl, qd_ref, 1, vdt_ref, sl)]
    for h in range(MLA_HEADS):
        hs = slice(h * LANES, (h + 1) * LANES)
        pair = slice((h // 2) * LANES, (h // 2 + 1) * LANES)
        jobs.append((km_ref, hs, qm_ref, None, vmt_ref, pair))
    n = len(jobs)

    def query(i):
        _, sl, q_ref, half, _, _ = jobs[i]
        if half is None:
            return q_ref[:, sl]
        q = q_ref[:, sl].astype(F32)
        return (jnp.where(lo, q, 0.0) if half == 0 else jnp.where(lo, 0.0, q)).astype(BF16)

    qs, mcol, mfin, lcol, acc = {}, {}, {}, {}, {}
    outs = [None] * n
    for i in range(-2, n + 1):
        a, b, c = i + 2, i, i - 1
        if 0 <= a < n:
            qs[a] = query(a)
        if 0 <= b < n:
            mfin[b] = jnp.max(mcol.pop(b), axis=0, keepdims=True)
        for r in range(n_chunks):
            rows = slice(r * kc, (r + 1) * kc)
            if 0 <= a < n:
                k_ref, ksl = jobs[a][0], jobs[a][1]
                s = lax.dot_general(k_ref[rows, ksl], qs[a], (((1,), (1,)), ((), ())),
                                    preferred_element_type=F32)
                s_ref[a % 3, rows, :] = s
                f = _fold_rows(s, jnp.maximum)
                mcol[a] = f if r == 0 else jnp.maximum(mcol[a], f)
            if 0 <= b < n:
                p = jnp.exp2(s_ref[b % 3, rows, :] - mfin[b])
                f = _fold_rows(p, jnp.add)
                lcol[b] = f if r == 0 else lcol[b] + f
                p_ref[b % 2, rows, :] = p.astype(BF16)
            if 0 <= c < n:
                vt_ref, vrows = jobs[c][4], jobs[c][5]
                d = jnp.dot(vt_ref[vrows, rows], p_ref[c % 2, rows, :], preferred_element_type=F32)
                acc[c] = d if r == 0 else acc[c] + d
        if 0 <= c < n:
            outs[c] = acc.pop(c) / jnp.sum(lcol.pop(c), axis=0, keepdims=True)

    for h in range(DIFF_HEADS):
        sl = slice(h * LANES, (h + 1) * LANES)
        a = outs[2 * h] - lam * outs[2 * h + 1]
        ms = jnp.mean(a * a, axis=0, keepdims=True)
        od = a * lax.rsqrt(ms + EPS) * subg * (1.0 - lam_init)
        od_ref[:, sl] = od.T.astype(BF16)
    base = 2 * DIFF_HEADS
    for hp in range(MLA_HEADS // 2):
        sl = slice(hp * LANES, (hp + 1) * LANES)
        om_ref[:, sl] = jnp.where(top, outs[base + 2 * hp], outs[base + 2 * hp + 1]).T.astype(BF16)


def _attention(qd, qm, kd, km, vdt, vmt, lam_p, subg, *, batch, lam_init):
    sq = qd.shape[0] // batch
    sk = kd.shape[0] // batch
    tq = TOK_TILE
    nq = sq // tq
    qrow = lambda b, i: (b * nq + i, 0)
    krow = lambda b, i: (b, 0)
    kcol = lambda b, i: (b, 0, 0)
    const = lambda b, i: (0, 0)
    return pl.pallas_call(
        functools.partial(_attn_kernel, lam_init=lam_init),
        grid=(batch, nq),
        in_specs=[
            pl.BlockSpec((tq, DIFF_W), qrow),
            pl.BlockSpec((tq, MLA_PAD), qrow),
            pl.BlockSpec((sk, DIFF_W), krow),
            pl.BlockSpec((sk, MLA_PAD), krow),
            pl.BlockSpec((None, DIFF_W, sk), kcol),
            pl.BlockSpec((None, MLA_W, sk), kcol),
            pl.BlockSpec((4, DIFF_HD), const),
            pl.BlockSpec((2 * DIFF_HD, 1), const),
        ],
        out_specs=[pl.BlockSpec((tq, DIFF_W), qrow), pl.BlockSpec((tq, MLA_W), qrow)],
        out_shape=[jax.ShapeDtypeStruct((batch * sq, DIFF_W), BF16),
                   jax.ShapeDtypeStruct((batch * sq, MLA_W), BF16)],
        scratch_shapes=[pltpu.VMEM((3, sk, tq), F32), pltpu.VMEM((2, sk, tq), BF16)],
        compiler_params=pltpu.CompilerParams(
            dimension_semantics=("parallel", "arbitrary"), vmem_limit_bytes=VMEM_LIMIT),
        name="attention",
    )(qd, qm, kd, km, vdt, vmt, lam_p, subg)


CONV_SPAN = (CONV_K + SUBLANES - 1) // SUBLANES * SUBLANES - SUBLANES


def _depthwise_conv(buf_ref, phase_ref, dw_ref, tm):
    base = HALO - CONV_K // 2
    acc = None
    for r in range(SUBLANES):
        offs = [o for o in range(base, base + CONV_K) if o % SUBLANES == r]
        phase_ref[r] = buf_ref[r:r + tm + CONV_SPAN, :]
        for o in offs:
            term = phase_ref[r, o - r:o - r + tm, :] * dw_ref[o - base:o - base + 1, :]
            acc = term if acc is None else acc + term
    return acc


def _merge_mlp_kernel(*refs, tiles_per_seq):
    it = iter(refs)
    x_ref, mod_ref, g1_ref, g2_ref, u_ref = (next(it) for _ in range(5))
    if tiles_per_seq > 1:
        up_ref, un_ref = next(it), next(it)
    dw_ref, cb_ref, cg_ref, od_ref, om_ref = (next(it) for _ in range(5))
    wg_ref, wc_ref, wd_ref, wm_ref, wo_ref, wup_ref, wdn_ref, y_ref, buf_ref, phase_ref = (next(it) for _ in range(10))
    tm = x_ref.shape[0]

    mod = mod_ref[...]
    x = x_ref[...]
    h = _modulated_norm(x, g1_ref[...], mod[0:1], mod[1:2])
    gates = jax.nn.sigmoid(jnp.dot(h.astype(BF16), wg_ref[...], preferred_element_type=F32))
    o_diff = jnp.dot(od_ref[...], wd_ref[...], preferred_element_type=F32)
    o_mla = jnp.dot(om_ref[...], wm_ref[...], preferred_element_type=F32)

    zeros = jnp.zeros((HALO, CONV_W), F32)
    if tiles_per_seq > 1:
        j = pl.program_id(0) % tiles_per_seq
        prev = jnp.where(j == 0, zeros, up_ref[...])
        nxt = jnp.where(j == tiles_per_seq - 1, zeros, un_ref[...])
    else:
        prev, nxt = zeros, zeros
    buf_ref[0:HALO, :] = prev
    buf_ref[HALO:HALO + tm, :] = u_ref[...]
    buf_ref[HALO + tm:HALO + tm + HALO, :] = nxt
    conv = _depthwise_conv(buf_ref, phase_ref, dw_ref, tm) + cb_ref[...]
    c = _silu(_rms_full(conv, cg_ref[...])).astype(BF16)
    o_conv = jnp.dot(c, wc_ref[...], preferred_element_type=F32)
    merged = (gates[:, 0:D_MODEL] * o_conv + gates[:, D_MODEL:2 * D_MODEL] * o_diff
              + gates[:, 2 * D_MODEL:3 * D_MODEL] * o_mla)
    x = x + mod[2:3] * jnp.dot(merged.astype(BF16), wo_ref[...], preferred_element_type=F32)
    h2 = _modulated_norm(x, g2_ref[...], mod[3:4], mod[4:5])
    up = jnp.dot(h2.astype(BF16), wup_ref[...], preferred_element_type=F32)
    act = jnp.square(jnp.maximum(up, 0.0)).astype(BF16)
    y_ref[...] = x + mod[5:6] * jnp.dot(act, wdn_ref[...], preferred_element_type=F32)


def _merge_mlp(x, mod_l, u, od, om, W, *, tiles_per_seq, row0):
    t = x.shape[0]
    tm = TOK_TILE
    nb = t // HALO
    per = tm // HALO
    row = lambda i: (i, 0)
    const = lambda i: (0, 0)

    def wspec(shape):
        return pl.BlockSpec(shape, const, pipeline_mode=pl.Buffered(1))

    in_specs = [
        pl.BlockSpec((tm, D_MODEL), row),
        pl.BlockSpec((None, 6, D_MODEL), lambda i: (row0 + (i // tiles_per_seq if row0 else 0), 0, 0)),
        pl.BlockSpec((1, D_MODEL), const),
        pl.BlockSpec((1, D_MODEL), const),
        pl.BlockSpec((tm, CONV_W), row),
    ]
    args = [x, mod_l, W["norm1_g"], W["norm2_g"], u]
    if tiles_per_seq > 1:
        in_specs += [
            pl.BlockSpec((HALO, CONV_W), lambda i: (jnp.maximum(i * per - 1, 0), 0)),
            pl.BlockSpec((HALO, CONV_W), lambda i: (jnp.minimum((i + 1) * per, nb - 1), 0)),
        ]
        args += [u, u]
    in_specs += [
        pl.BlockSpec((CONV_K + 1, CONV_W), const),
        pl.BlockSpec((1, CONV_W), const),
        pl.BlockSpec((1, CONV_W), const),
        pl.BlockSpec((tm, DIFF_W), row),
        pl.BlockSpec((tm, MLA_W), row),
        wspec((D_MODEL, 3 * D_MODEL)),
        wspec((CONV_W, D_MODEL)),
        wspec((DIFF_W, D_MODEL)),
        wspec((MLA_W, D_MODEL)),
        wspec((D_MODEL, D_MODEL)),
        wspec((D_MODEL, D_FF)),
        wspec((D_FF, D_MODEL)),
    ]
    args += [W["conv_dw"], W["conv_b"], W["conv_g"], od, om, W["wg"], W["w_conv_out"], W["w_diff_out"],
             W["w_mla_out"], W["w_out"], W["w_up"], W["w_down"]]
    return pl.pallas_call(
        functools.partial(_merge_mlp_kernel, tiles_per_seq=tiles_per_seq),
        grid=(t // tm,),
        in_specs=in_specs,
        out_specs=pl.BlockSpec((tm, D_MODEL), row),
        out_shape=jax.ShapeDtypeStruct((t, D_MODEL), F32),
        scratch_shapes=[pltpu.VMEM((tm + 2 * HALO, CONV_W), F32),
                        pltpu.VMEM((SUBLANES, tm + CONV_SPAN, CONV_W), F32)],
        compiler_params=pltpu.CompilerParams(
            dimension_semantics=("parallel",), vmem_limit_bytes=VMEM_LIMIT),
        name="merge_mlp",
    )(*args)


def _rope_tables(seq_len):
    rows = seq_len // GRID_W
    row = np.repeat(np.arange(rows, dtype=np.float64), GRID_W)
    col = np.tile(np.arange(GRID_W, dtype=np.float64), rows)

    def tables(rot_dim):
        half = rot_dim // 2
        inv = ROPE_BASE ** (-np.arange(0, half, 2, dtype=np.float64) / half)
        a0 = row[:, None] * inv
        a1 = col[:, None] * inv
        cos = np.concatenate([np.cos(a0), np.cos(a0), np.cos(a1), np.cos(a1)], axis=1)
        sin = np.concatenate([-np.sin(a0), np.sin(a0), -np.sin(a1), np.sin(a1)], axis=1)
        return cos, sin

    cd, sd = tables(DIFF_HD)
    cd, sd = np.tile(cd, (1, LANES // DIFF_HD)), np.tile(sd, (1, LANES // DIFF_HD))
    cm, sm = tables(MLA_ROPE)
    pad = LANES - MLA_QK
    cm = np.concatenate([np.ones((seq_len, MLA_NOPE)), cm, np.ones((seq_len, pad))], axis=1)
    sm = np.concatenate([np.zeros((seq_len, MLA_NOPE)), sm, np.zeros((seq_len, pad))], axis=1)
    return tuple(jnp.asarray(t, dtype=F32) for t in (cd, sd, cm, sm))


def _pad_heads(v):
    return jnp.pad(v, [(0, 0)] * (v.ndim - 1) + [(0, LANES - MLA_QK)])


def _layer_weights(l, norm1_g, w_in, conv_dw, conv_b, conv_norm_g, w_conv_out, diff_q_norm, diff_k_norm,
                   diff_lambda, diff_subln, w_diff_out, mla_q_a_norm, mla_kv_a_norm, w_uq, w_ukv,
                   mla_q_norm, mla_k_norm, w_mla_out, w_out, norm2_g, w_up, w_down):
    wi = w_in[l]
    kpe_cols = wi[:, N_MAIN:N_MAIN + MLA_ROPE]
    zc = jnp.zeros((D_MODEL, MLA_ROPE), F32)
    w1 = jnp.concatenate([wi[:, :N_MAIN], kpe_cols, zc, kpe_cols, zc], axis=1).astype(BF16)
    wuq = _pad_heads(w_uq[l].reshape(MLA_Q_RANK, MLA_HEADS, MLA_QK)).reshape(MLA_Q_RANK, MLA_PAD)
    wukv3 = w_ukv[l].reshape(MLA_KV_RANK, MLA_HEADS, MLA_NOPE + MLA_V)
    wkn = jnp.pad(wukv3[:, :, :MLA_NOPE], ((0, 0), (0, 0), (0, LANES - MLA_NOPE))).reshape(MLA_KV_RANK, MLA_PAD)
    wv = wukv3[:, :, MLA_NOPE:].reshape(MLA_KV_RANK, MLA_W)
    return dict(
        norm1_g=norm1_g[l][None], norm2_g=norm2_g[l][None],
        w1=w1, wg=wi[:, N_MAIN + MLA_ROPE:].astype(BF16),
        gq=jnp.tile(diff_q_norm[l], DIFF_W // DIFF_HD)[None],
        gk=jnp.tile(diff_k_norm[l], DIFF_W // DIFF_HD)[None],
        gcq=mla_q_a_norm[l][None], gckv=mla_kv_a_norm[l][None],
        wuq=wuq.astype(BF16), wukv=jnp.concatenate([wkn, wv], axis=1).astype(BF16),
        gmq=jnp.tile(_pad_heads(mla_q_norm[l]), MLA_HEADS)[None],
        gmk=jnp.tile(_pad_heads(mla_k_norm[l]), MLA_HEADS)[None],
        conv_dw=jnp.pad(conv_dw[l], ((0, 1), (0, 0))), conv_b=conv_b[l][None], conv_g=conv_norm_g[l][None],
        lam=diff_lambda[l], subg=diff_subln[l][:, None],
        w_conv_out=w_conv_out[l].astype(BF16), w_diff_out=w_diff_out[l].astype(BF16),
        w_mla_out=w_mla_out[l].astype(BF16), w_out=w_out[l].astype(BF16),
        w_up=w_up[l].astype(BF16), w_down=w_down[l].astype(BF16),
    )


def kernel(x_prompt, x_sample, cache_diff_k, cache_diff_v, cache_mla_ckv, cache_mla_kpe, c, c_ctx, mod_w, mod_b, norm1_g, w_in, conv_dw, conv_b, conv_norm_g, w_conv_out, diff_q_norm, diff_k_norm, diff_lambda, diff_subln, w_diff_out, mla_q_a_norm, mla_kv_a_norm, w_uq, w_ukv, mla_q_norm, mla_k_norm, w_mla_out, w_out, norm2_g, w_up, w_down):
    batch, seq, _ = x_prompt.shape
    dec_batch, dec_seq, _ = x_sample.shape
    past = cache_diff_k.shape[2]
    assert seq == TOK_TILE and past == TOK_TILE and dec_seq % TOK_TILE == 0 and 1 + dec_batch <= 8

    cond_rows = jnp.concatenate([c_ctx[None], c, jnp.zeros((8 - 1 - dec_batch, D_MODEL), F32)], axis=0)
    mod = _modulation(cond_rows, mod_w, mod_b).reshape(DEPTH, 8, 6, D_MODEL)
    tabs = _rope_tables(dec_seq)

    xp = x_prompt.reshape(batch * seq, D_MODEL)
    xs = x_sample.reshape(dec_batch * dec_seq, D_MODEL)
    st_k, st_v, st_ckv, st_kpe = [], [], [], []
    for l in range(DEPTH):
        W = _layer_weights(l, norm1_g, w_in, conv_dw, conv_b, conv_norm_g, w_conv_out, diff_q_norm,
                           diff_k_norm, diff_lambda, diff_subln, w_diff_out, mla_q_a_norm, mla_kv_a_norm,
                           w_uq, w_ukv, mla_q_norm, mla_k_norm, w_mla_out, w_out, norm2_g, w_up, w_down)
        lam_init = 0.8 - 0.6 * math.exp(-0.3 * l)

        u, qd, qm, kd, km, vdt, vmt, sk, sv, sckv, skpe = _inproj(
            xp, mod[l], W, None, None, batch=batch, row0=0, states=True)
        st_k.append(sk.reshape(batch, seq, DIFF_HEADS, 2 * DIFF_HD))
        st_v.append(sv.reshape(batch, seq, DIFF_HEADS, 2 * DIFF_HD))
        st_ckv.append(sckv.reshape(batch, seq, MLA_KV_RANK))
        st_kpe.append(skpe.reshape(batch, seq, MLA_ROPE))
        od, om = _attention(qd, qm, kd, km, vdt, vmt, W["lam"], W["subg"], batch=batch, lam_init=lam_init)
        xp = _merge_mlp(xp, mod[l], u, od, om, W, tiles_per_seq=1, row0=0)

        cache = (cache_diff_k[:, l].reshape(dec_batch * past, DIFF_W),
                 cache_diff_v[:, l].reshape(dec_batch * past, DIFF_W),
                 cache_mla_ckv[:, l].reshape(dec_batch * past, MLA_KV_RANK),
                 jnp.pad(cache_mla_kpe[:, l].reshape(dec_batch * past, MLA_ROPE),
                         ((0, 0), (MLA_NOPE, LANES - MLA_QK))))
        u, qd, qm, kd, km, vdt, vmt = _inproj(xs, mod[l], W, tabs, cache, batch=dec_batch, row0=1, states=False)
        od, om = _attention(qd, qm, kd, km, vdt, vmt, W["lam"], W["subg"], batch=dec_batch, lam_init=lam_init)
        xs = _merge_mlp(xs, mod[l], u, od, om, W, tiles_per_seq=dec_seq // TOK_TILE, row0=1)

    return (xp.reshape(batch, seq, D_MODEL), xs.reshape(dec_batch, dec_seq, D_MODEL),
            jnp.stack(st_k, axis=1), jnp.stack(st_v, axis=1), jnp.stack(st_ckv, axis=1), jnp.stack(st_kpe, axis=1))
```

```python
import functools
import math

import numpy as np
import jax
import jax.numpy as jnp
from jax import lax
from jax.experimental import pallas as pl
from jax.experimental.pallas import tpu as pltpu

D_MODEL = 1024
DEPTH = 2
GRID_W = 64
ROPE_BASE = 10000.0
CONV_W = 512
CONV_K = 31
DIFF_HEADS = 4
DIFF_HD = 64
DIFF_W = DIFF_HEADS * 2 * DIFF_HD
MLA_HEADS = 8
MLA_NOPE = 64
MLA_ROPE = 32
MLA_QK = MLA_NOPE + MLA_ROPE
MLA_V = 64
MLA_Q_RANK = 384
MLA_KV_RANK = 256
MLA_W = MLA_HEADS * MLA_V
D_FF = 4 * D_MODEL
EPS = 1e-6

LANES = 128
SUBLANES = 8
MXU_W = 256
MLA_PAD = MLA_HEADS * LANES
TOK_TILE = 256
HALO = 16
N_MAIN = 2 * CONV_W + 3 * DIFF_W + MLA_Q_RANK + MLA_KV_RANK
VMEM_LIMIT = 56 * 1024 * 1024
LOG2E = math.log2(math.e)

F32 = jnp.float32
BF16 = jnp.bfloat16


def _silu(x):
    return x * jax.nn.sigmoid(x)


def _rms_full(x, g):
    ms = jnp.mean(x * x, axis=-1, keepdims=True)
    return x * lax.rsqrt(ms + EPS) * g


def _lane_tiles(x):
    return jnp.concatenate([x[:, t:t + MXU_W] for t in range(0, x.shape[-1], MXU_W)], axis=0)


def _from_lane_tiles(y, rows):
    return jnp.concatenate([y[r:r + rows] for r in range(0, y.shape[0], rows)], axis=-1)


def _group_rms(x, seg_ref, n, g):
    rows = x.shape[0]
    xt = _lane_tiles(x)
    ss = jnp.dot((xt * xt).astype(BF16), seg_ref[...], preferred_element_type=F32)
    return _from_lane_tiles(xt * lax.rsqrt(ss * (1.0 / n) + EPS), rows) * g


def _rope(x, perm_ref, cos, sin_signed):
    rows = x.shape[0]
    sw = jnp.dot(_lane_tiles(x).astype(BF16), perm_ref[...], preferred_element_type=F32)
    return x * cos + _from_lane_tiles(sw, rows) * sin_signed


def _tile_lanes(t, n):
    return jnp.concatenate([t] * n, axis=-1)


def _modulated_norm(x, g, shift, scale):
    return _rms_full(x, g) * (1.0 + scale) + shift


def _mod_kernel(s_ref, w_ref, b_ref, o_ref):
    s = _silu(s_ref[...]).astype(BF16)
    o_ref[...] = jnp.dot(s, w_ref[...].astype(BF16), preferred_element_type=F32) + b_ref[...]


def _modulation(cond_rows, mod_w, mod_b):
    tn = 1536
    n = 6 * D_MODEL
    return pl.pallas_call(
        _mod_kernel,
        grid=(DEPTH, n // tn),
        in_specs=[
            pl.BlockSpec((8, D_MODEL), lambda l, j: (0, 0)),
            pl.BlockSpec((None, D_MODEL, tn), lambda l, j: (l, 0, j)),
            pl.BlockSpec((None, 1, tn), lambda l, j: (l, 0, j)),
        ],
        out_specs=pl.BlockSpec((None, 8, tn), lambda l, j: (l, 0, j)),
        out_shape=jax.ShapeDtypeStruct((DEPTH, 8, n), F32),
        compiler_params=pltpu.CompilerParams(
            dimension_semantics=("parallel", "parallel"), vmem_limit_bytes=VMEM_LIMIT),
        name="modulation",
    )(cond_rows, mod_w, mod_b.reshape(DEPTH, 1, n))


def _mla_keys_values(ckvn, kpe_hi, wukv_ref):
    kv = jnp.dot(ckvn.astype(BF16), wukv_ref[...], preferred_element_type=F32)
    return kv[:, :MLA_PAD] + _tile_lanes(kpe_hi, MLA_HEADS), kv[:, MLA_PAD:]


def _inproj_kernel(*refs, rope, states, cache):
    it = iter(refs)
    x_ref, mod_ref, g1_ref, w_ref = next(it), next(it), next(it), next(it)
    gq_ref, gk_ref, gcq_ref, gckv_ref = next(it), next(it), next(it), next(it)
    wuq_ref, wukv_ref, gmq_ref, gmk_ref = next(it), next(it), next(it), next(it)
    segd_ref, segm_ref = next(it), next(it)
    if rope:
        permd_ref, permm_ref = next(it), next(it)
        cd_ref, sd_ref, cm_ref, sm_ref = next(it), next(it), next(it), next(it)
    if cache:
        ck_ref, cv_ref, cckv_ref, ckpe_ref = next(it), next(it), next(it), next(it)
    u_ref, qd_ref, qm_ref, kd_ref, km_ref, vdt_ref, vmt_ref = (next(it) for _ in range(7))
    if states:
        sk_ref, sv_ref, sckv_ref, skpe_ref = next(it), next(it), next(it), next(it)

    def tokens():
        mod = mod_ref[...]
        h = _modulated_norm(x_ref[...], g1_ref[...], mod[0:1], mod[1:2])
        proj = jnp.dot(h.astype(BF16), w_ref[...], preferred_element_type=F32)

        o = 0
        u_a = proj[:, o:o + CONV_W]; o += CONV_W
        u_g = proj[:, o:o + CONV_W]; o += CONV_W
        dq = proj[:, o:o + DIFF_W]; o += DIFF_W
        dk = proj[:, o:o + DIFF_W]; o += DIFF_W
        dv = proj[:, o:o + DIFF_W]; o += DIFF_W
        cq = proj[:, o:o + MLA_Q_RANK]; o += MLA_Q_RANK
        ckv = proj[:, o:o + MLA_KV_RANK]; o += MLA_KV_RANK
        kpe2 = proj[:, o:o + LANES]

        cqn = _rms_full(cq, gcq_ref[...])
        ckvn = _rms_full(ckv, gckv_ref[...])
        q = _group_rms(dq, segd_ref, DIFF_HD, gq_ref[...])
        k = _group_rms(dk, segd_ref, DIFF_HD, gk_ref[...])
        qm = jnp.dot(cqn.astype(BF16), wuq_ref[...], preferred_element_type=F32)
        lane = lax.broadcasted_iota(jnp.int32, (1, LANES), 1)
        km, vm = _mla_keys_values(ckvn, jnp.where(lane >= MLA_NOPE, kpe2, 0.0), wukv_ref)

        u_ref[...] = u_a * jax.nn.sigmoid(u_g)
        vdt_ref[...] = dv.T.astype(BF16)
        if states:
            sk_ref[...] = k
            sv_ref[...] = dv
            sckv_ref[...] = ckvn
            skpe_ref[...] = kpe2[:, :MLA_ROPE]

        qm = _group_rms(qm, segm_ref, MLA_QK, gmq_ref[...])
        km = _group_rms(km, segm_ref, MLA_QK, gmk_ref[...])
        if rope:
            cd = _tile_lanes(cd_ref[...], DIFF_W // LANES)
            sd = _tile_lanes(sd_ref[...], DIFF_W // LANES)
            q = _rope(q, permd_ref, cd, sd)
            k = _rope(k, permd_ref, cd, sd)
        qd_ref[...] = (q * (DIFF_HD ** -0.5 * LOG2E)).astype(BF16)
        kd_ref[...] = k.astype(BF16)
        vmt_ref[...] = vm.T.astype(BF16)
        if rope:
            cm = _tile_lanes(cm_ref[...], MLA_HEADS)
            sm = _tile_lanes(sm_ref[...], MLA_HEADS)
            qm = _rope(qm, permm_ref, cm, sm)
            km = _rope(km, permm_ref, cm, sm)
        qm_ref[...] = (qm * (MLA_QK ** -0.5 * LOG2E)).astype(BF16)
        km_ref[...] = km.astype(BF16)

    def cached_context():
        kd_ref[...] = ck_ref[...].astype(BF16)
        vdt_ref[...] = cv_ref[...].T.astype(BF16)
        km, vm = _mla_keys_values(cckv_ref[...], ckpe_ref[...], wukv_ref)
        km_ref[...] = _group_rms(km, segm_ref, MLA_QK, gmk_ref[...]).astype(BF16)
        vmt_ref[...] = vm.T.astype(BF16)

    if cache:
        j = pl.program_id(1)
        pl.when(j == 0)(cached_context)
        pl.when(j > 0)(tokens)
    else:
        tokens()


def _inproj(x, mod_l, W, rope_tabs, cache, *, batch, row0, states):
    t = x.shape[0]
    tm = TOK_TILE
    n = t // batch // tm
    rope = rope_tabs is not None
    has_cache = cache is not None
    nk = n + 1 if has_cache else n
    if has_cache:
        grid = (batch, nk)
        tok = lambda b, j: (b * n + jnp.maximum(j - 1, 0), 0)
        key = lambda b, j: (b * nk + j, 0)
        keyt = lambda b, j: (b, 0, j)
        const = lambda b, j: (0, 0)
        modrow = lambda b, j: (row0 + b, 0, 0)
        tab = lambda b, j: (jnp.maximum(j - 1, 0), 0)
        cblk = lambda b, j: (b, 0)
    else:
        grid = (batch * n,)
        tok = lambda i: (i, 0)
        key = tok
        keyt = lambda i: (i // n, 0, i % n)
        const = lambda i: (0, 0)
        modrow = lambda i: (row0 + (i // n if row0 else 0), 0, 0)
        tab = lambda i: (i % n, 0)
    wcols = W["w1"].shape[1]
    in_specs = [
        pl.BlockSpec((tm, D_MODEL), tok),
        pl.BlockSpec((None, 6, D_MODEL), modrow),
        pl.BlockSpec((1, D_MODEL), const),
        pl.BlockSpec((D_MODEL, wcols), const),
        pl.BlockSpec((1, DIFF_W), const),
        pl.BlockSpec((1, DIFF_W), const),
        pl.BlockSpec((1, MLA_Q_RANK), const),
        pl.BlockSpec((1, MLA_KV_RANK), const),
        pl.BlockSpec((MLA_Q_RANK, MLA_PAD), const),
        pl.BlockSpec((MLA_KV_RANK, MLA_PAD + MLA_W), const),
        pl.BlockSpec((1, MLA_PAD), const),
        pl.BlockSpec((1, MLA_PAD), const),
        pl.BlockSpec((MXU_W, MXU_W), const),
        pl.BlockSpec((MXU_W, MXU_W), const),
    ]
    seg_d, seg_m, perm_d, perm_m = _group_matrices()
    args = [x, mod_l, W["norm1_g"], W["w1"], W["gq"], W["gk"], W["gcq"], W["gckv"],
            W["wuq"], W["wukv"], W["gmq"], W["gmk"], seg_d, seg_m]
    if rope:
        in_specs += [pl.BlockSpec((MXU_W, MXU_W), const)] * 2 + [pl.BlockSpec((tm, LANES), tab)] * 4
        args += [perm_d, perm_m] + list(rope_tabs)
    if has_cache:
        in_specs += [pl.BlockSpec((tm, DIFF_W), cblk), pl.BlockSpec((tm, DIFF_W), cblk),
                     pl.BlockSpec((tm, MLA_KV_RANK), cblk), pl.BlockSpec((tm, LANES), cblk)]
        args += list(cache)
    sk = nk * tm
    out_shape = [
        jax.ShapeDtypeStruct((t, CONV_W), F32),
        jax.ShapeDtypeStruct((t, DIFF_W), BF16),
        jax.ShapeDtypeStruct((t, MLA_PAD), BF16),
        jax.ShapeDtypeStruct((batch * sk, DIFF_W), BF16),
        jax.ShapeDtypeStruct((batch * sk, MLA_PAD), BF16),
        jax.ShapeDtypeStruct((batch, DIFF_W, sk), BF16),
        jax.ShapeDtypeStruct((batch, MLA_W, sk), BF16),
    ]
    out_specs = [
        pl.BlockSpec((tm, CONV_W), tok),
        pl.BlockSpec((tm, DIFF_W), tok),
        pl.BlockSpec((tm, MLA_PAD), tok),
        pl.BlockSpec((tm, DIFF_W), key),
        pl.BlockSpec((tm, MLA_PAD), key),
        pl.BlockSpec((None, DIFF_W, tm), keyt),
        pl.BlockSpec((None, MLA_W, tm), keyt),
    ]
    if states:
        widths = (DIFF_W, DIFF_W, MLA_KV_RANK, MLA_ROPE)
        out_shape += [jax.ShapeDtypeStruct((t, w), F32) for w in widths]
        out_specs += [pl.BlockSpec((tm, w), tok) for w in widths]
    return pl.pallas_call(
        functools.partial(_inproj_kernel, rope=rope, states=states, cache=has_cache),
        grid=grid,
        in_specs=in_specs,
        out_specs=out_specs,
        out_shape=out_shape,
        compiler_params=pltpu.CompilerParams(
            dimension_semantics=("arbitrary",) * len(grid), vmem_limit_bytes=VMEM_LIMIT),
        name="inproj_latent" if has_cache else "inproj_context",
    )(*args)


def _fold_rows(x, op):
    n = x.shape[0]
    while n % (2 * SUBLANES) == 0 and n > 4 * SUBLANES:
        n //= 2
        x = op(x[:n], x[n:])
    return x


KEY_CHUNK = 256


def _attn_kernel(qd_ref, qm_ref, kd_ref, km_ref, vdt_ref, vmt_ref, lam_ref, sub_ref, od_ref, om_ref, s_ref, p_ref,
                 *, lam_init):
    lp = lam_ref[...]
    lam = (jnp.exp(jnp.sum(lp[0:1] * lp[1:2], axis=-1, keepdims=True))
           - jnp.exp(jnp.sum(lp[2:3] * lp[3:4], axis=-1, keepdims=True)) + lam_init)
    lo = lax.broadcasted_iota(jnp.int32, (1, LANES), 1) < DIFF_HD
    top = lax.broadcasted_iota(jnp.int32, (LANES, 1), 0) < MLA_V
    subg = sub_ref[...]
    sk = kd_ref.shape[0]
    kc = min(KEY_CHUNK, sk)
    n_chunks = sk // kc

    jobs = []
    for h in range(DIFF_HEADS):
        sl = slice(h * LANES, (h + 1) * LANES)
        jobs += [(kd_ref, sl, qd_ref, 0, vdt_ref, sl), (kd_ref, sl, qd_ref, 1, vdt_ref, sl)]
    for h in range(MLA_HEADS):
        hs = slice(h * LANES, (h + 1) * LANES)
        pair = slice((h // 2) * LANES, (h // 2 + 1) * LANES)
        jobs.append((km_ref, hs, qm_ref, None, vmt_ref, pair))
    n = len(jobs)

    def query(i):
        _, sl, q_ref, half, _, _ = jobs[i]
        if half is None:
            return q_ref[:, sl]
        q = q_ref[:, sl].astype(F32)
        return (jnp.where(lo, q, 0.0) if half == 0 else jnp.where(lo, 0.0, q)).astype(BF16)

    qs, mcol, mfin, lcol, acc = {}, {}, {}, {}, {}
    outs = [None] * n
    for i in range(-2, n + 1):
        a, b, c = i + 2, i, i - 1
        if 0 <= a < n:
            qs[a] = query(a)
        if 0 <= b < n:
            mfin[b] = jnp.max(mcol.pop(b), axis=0, keepdims=True)
        for r in range(n_chunks):
            rows = slice(r * kc, (r + 1) * kc)
            if 0 <= a < n:
                k_ref, ksl = jobs[a][0], jobs[a][1]
                s = lax.dot_general(k_ref[rows, ksl], qs[a], (((1,), (1,)), ((), ())),
                                    preferred_element_type=F32)
                s_ref[a % 3, rows, :] = s
                f = _fold_rows(s, jnp.maximum)
                mcol[a] = f if r == 0 else jnp.maximum(mcol[a], f)
            if 0 <= b < n:
                p = jnp.exp2(s_ref[b % 3, rows, :] - mfin[b])
                f = _fold_rows(p, jnp.add)
                lcol[b] = f if r == 0 else lcol[b] + f
                p_ref[b % 2, rows, :] = p.astype(BF16)
            if 0 <= c < n:
                vt_ref, vrows = jobs[c][4], jobs[c][5]
                d = jnp.dot(vt_ref[vrows, rows], p_ref[c % 2, rows, :], preferred_element_type=F32)
                acc[c] = d if r == 0 else acc[c] + d
        if 0 <= c < n:
            outs[c] = acc.pop(c) / jnp.sum(lcol.pop(c), axis=0, keepdims=True)

    for h in range(DIFF_HEADS):
        sl = slice(h * LANES, (h + 1) * LANES)
        a = outs[2 * h] - lam * outs[2 * h + 1]
        ms = jnp.mean(a * a, axis=0, keepdims=True)
        od = a * lax.rsqrt(ms + EPS) * subg * (1.0 - lam_init)
        od_ref[:, sl] = od.T.astype(BF16)
    base = 2 * DIFF_HEADS
    for hp in range(MLA_HEADS // 2):
        sl = slice(hp * LANES, (hp + 1) * LANES)
        om_ref[:, sl] = jnp.where(top, outs[base + 2 * hp], outs[base + 2 * hp + 1]).T.astype(BF16)


def _attention(qd, qm, kd, km, vdt, vmt, lam_p, subg, *, batch, lam_init):
    sq = qd.shape[0] // batch
    sk = kd.shape[0] // batch
    tq = TOK_TILE
    nq = sq // tq
    qrow = lambda b, i: (b * nq + i, 0)
    krow = lambda b, i: (b, 0)
    kcol = lambda b, i: (b, 0, 0)
    const = lambda b, i: (0, 0)
    return pl.pallas_call(
        functools.partial(_attn_kernel, lam_init=lam_init),
        grid=(batch, nq),
        in_specs=[
            pl.BlockSpec((tq, DIFF_W), qrow),
            pl.BlockSpec((tq, MLA_PAD), qrow),
            pl.BlockSpec((sk, DIFF_W), krow),
            pl.BlockSpec((sk, MLA_PAD), krow),
            pl.BlockSpec((None, DIFF_W, sk), kcol),
            pl.BlockSpec((None, MLA_W, sk), kcol),
            pl.BlockSpec((4, DIFF_HD), const),
            pl.BlockSpec((2 * DIFF_HD, 1), const),
        ],
        out_specs=[pl.BlockSpec((tq, DIFF_W), qrow), pl.BlockSpec((tq, MLA_W), qrow)],
        out_shape=[jax.ShapeDtypeStruct((batch * sq, DIFF_W), BF16),
                   jax.ShapeDtypeStruct((batch * sq, MLA_W), BF16)],
        scratch_shapes=[pltpu.VMEM((3, sk, tq), F32), pltpu.VMEM((2, sk, tq), BF16)],
        compiler_params=pltpu.CompilerParams(
            dimension_semantics=("parallel", "arbitrary"), vmem_limit_bytes=VMEM_LIMIT),
        name="attention",
    )(qd, qm, kd, km, vdt, vmt, lam_p, subg)


CONV_SPAN = (CONV_K + SUBLANES - 1) // SUBLANES * SUBLANES - SUBLANES


def _depthwise_conv(buf_ref, phase_ref, dw_ref, tm):
    base = HALO - CONV_K // 2
    acc = None
    for r in range(SUBLANES):
        offs = [o for o in range(base, base + CONV_K) if o % SUBLANES == r]
        phase_ref[r] = buf_ref[r:r + tm + CONV_SPAN, :]
        for o in offs:
            term = phase_ref[r, o - r:o - r + tm, :] * dw_ref[o - base:o - base + 1, :]
            acc = term if acc is None else acc + term
    return acc


def _merge_mlp_kernel(*refs, tiles_per_seq):
    it = iter(refs)
    x_ref, mod_ref, g1_ref, g2_ref, u_ref = (next(it) for _ in range(5))
    if tiles_per_seq > 1:
        up_ref, un_ref = next(it), next(it)
    dw_ref, cb_ref, cg_ref, od_ref, om_ref = (next(it) for _ in range(5))
    wg_ref, wc_ref, wd_ref, wm_ref, wo_ref, wup_ref, wdn_ref, y_ref, buf_ref, phase_ref = (next(it) for _ in range(10))
    tm = x_ref.shape[0]

    mod = mod_ref[...]
    x = x_ref[...]
    h = _modulated_norm(x, g1_ref[...], mod[0:1], mod[1:2])
    gates = jax.nn.sigmoid(jnp.dot(h.astype(BF16), wg_ref[...], preferred_element_type=F32))
    o_diff = jnp.dot(od_ref[...], wd_ref[...], preferred_element_type=F32)
    o_mla = jnp.dot(om_ref[...], wm_ref[...], preferred_element_type=F32)

    zeros = jnp.zeros((HALO, CONV_W), F32)
    if tiles_per_seq > 1:
        j = pl.program_id(0) % tiles_per_seq
        prev = jnp.where(j == 0, zeros, up_ref[...])
        nxt = jnp.where(j == tiles_per_seq - 1, zeros, un_ref[...])
    else:
        prev, nxt = zeros, zeros
    buf_ref[0:HALO, :] = prev
    buf_ref[HALO:HALO + tm, :] = u_ref[...]
    buf_ref[HALO + tm:HALO + tm + HALO, :] = nxt
    conv = _depthwise_conv(buf_ref, phase_ref, dw_ref, tm) + cb_ref[...]
    c = _silu(_rms_full(conv, cg_ref[...])).astype(BF16)
    o_conv = jnp.dot(c, wc_ref[...], preferred_element_type=F32)
    merged = (gates[:, 0:D_MODEL] * o_conv + gates[:, D_MODEL:2 * D_MODEL] * o_diff
              + gates[:, 2 * D_MODEL:3 * D_MODEL] * o_mla)
    x = x + mod[2:3] * jnp.dot(merged.astype(BF16), wo_ref[...], preferred_element_type=F32)
    h2 = _modulated_norm(x, g2_ref[...], mod[3:4], mod[4:5])
    up = jnp.dot(h2.astype(BF16), wup_ref[...], preferred_element_type=F32)
    act = jnp.square(jnp.maximum(up, 0.0)).astype(BF16)
    y_ref[...] = x + mod[5:6] * jnp.dot(act, wdn_ref[...], preferred_element_type=F32)


def _merge_mlp(x, mod_l, u, od, om, W, *, tiles_per_seq, row0):
    t = x.shape[0]
    tm = TOK_TILE
    nb = t // HALO
    per = tm // HALO
    row = lambda i: (i, 0)
    const = lambda i: (0, 0)

    def wspec(shape):
        return pl.BlockSpec(shape, const, pipeline_mode=pl.Buffered(1))

    in_specs = [
        pl.BlockSpec((tm, D_MODEL), row),
        pl.BlockSpec((None, 6, D_MODEL), lambda i: (row0 + (i // tiles_per_seq if row0 else 0), 0, 0)),
        pl.BlockSpec((1, D_MODEL), const),
        pl.BlockSpec((1, D_MODEL), const),
        pl.BlockSpec((tm, CONV_W), row),
    ]
    args = [x, mod_l, W["norm1_g"], W["norm2_g"], u]
    if tiles_per_seq > 1:
        in_specs += [
            pl.BlockSpec((HALO, CONV_W), lambda i: (jnp.maximum(i * per - 1, 0), 0)),
            pl.BlockSpec((HALO, CONV_W), lambda i: (jnp.minimum((i + 1) * per, nb - 1), 0)),
        ]
        args += [u, u]
    in_specs += [
        pl.BlockSpec((CONV_K + 1, CONV_W), const),
        pl.BlockSpec((1, CONV_W), const),
        pl.BlockSpec((1, CONV_W), const),
        pl.BlockSpec((tm, DIFF_W), row),
        pl.BlockSpec((tm, MLA_W), row),
        wspec((D_MODEL, 3 * D_MODEL)),
        wspec((CONV_W, D_MODEL)),
        wspec((DIFF_W, D_MODEL)),
        wspec((MLA_W, D_MODEL)),
        wspec((D_MODEL, D_MODEL)),
        wspec((D_MODEL, D_FF)),
        wspec((D_FF, D_MODEL)),
    ]
    args += [W["conv_dw"], W["conv_b"], W["conv_g"], od, om, W["wg"], W["w_conv_out"], W["w_diff_out"],
             W["w_mla_out"], W["w_out"], W["w_up"], W["w_down"]]
    return pl.pallas_call(
        functools.partial(_merge_mlp_kernel, tiles_per_seq=tiles_per_seq),
        grid=(t // tm,),
        in_specs=in_specs,
        out_specs=pl.BlockSpec((tm, D_MODEL), row),
        out_shape=jax.ShapeDtypeStruct((t, D_MODEL), F32),
        scratch_shapes=[pltpu.VMEM((tm + 2 * HALO, CONV_W), F32),
                        pltpu.VMEM((SUBLANES, tm + CONV_SPAN, CONV_W), F32)],
        compiler_params=pltpu.CompilerParams(
            dimension_semantics=("parallel",), vmem_limit_bytes=VMEM_LIMIT),
        name="merge_mlp",
    )(*args)


def _rope_tables(seq_len):
    rows = seq_len // GRID_W
    row = np.repeat(np.arange(rows, dtype=np.float64), GRID_W)
    col = np.tile(np.arange(GRID_W, dtype=np.float64), rows)

    def tables(rot_dim):
        half = rot_dim // 2
        inv = ROPE_BASE ** (-np.arange(0, half, 2, dtype=np.float64) / half)
        a0 = row[:, None] * inv
        a1 = col[:, None] * inv
        cos = np.concatenate([np.cos(a0), np.cos(a0), np.cos(a1), np.cos(a1)], axis=1)
        sin = np.concatenate([-np.sin(a0), np.sin(a0), -np.sin(a1), np.sin(a1)], axis=1)
        return cos, sin

    cd, sd = tables(DIFF_HD)
    cd, sd = np.tile(cd, (1, LANES // DIFF_HD)), np.tile(sd, (1, LANES // DIFF_HD))
    cm, sm = tables(MLA_ROPE)
    pad = LANES - MLA_QK
    cm = np.concatenate([np.ones((seq_len, MLA_NOPE)), cm, np.ones((seq_len, pad))], axis=1)
    sm = np.concatenate([np.zeros((seq_len, MLA_NOPE)), sm, np.zeros((seq_len, pad))], axis=1)
    return tuple(jnp.asarray(t, dtype=F32) for t in (cd, sd, cm, sm))


def _group_matrices():
    lane = np.arange(MXU_W)
    seg_d = (lane[:, None] // DIFF_HD == lane[None, :] // DIFF_HD)
    seg_m = (lane[:, None] // LANES == lane[None, :] // LANES)

    def swap(rotary, half):
        src = np.where(lane % (2 * half) < half, lane + half, lane - half)
        return (lane[:, None] == src[None, :]) & rotary[None, :]

    perm_d = swap(np.ones(MXU_W, bool), DIFF_HD // 4)
    in_head = lane % LANES
    perm_m = swap((in_head >= MLA_NOPE) & (in_head < MLA_QK), MLA_ROPE // 4)
    return tuple(jnp.asarray(m, dtype=BF16) for m in (seg_d, seg_m, perm_d, perm_m))


def _pad_heads(v):
    return jnp.pad(v, [(0, 0)] * (v.ndim - 1) + [(0, LANES - MLA_QK)])


def _layer_weights(l, norm1_g, w_in, conv_dw, conv_b, conv_norm_g, w_conv_out, diff_q_norm, diff_k_norm,
                   diff_lambda, diff_subln, w_diff_out, mla_q_a_norm, mla_kv_a_norm, w_uq, w_ukv,
                   mla_q_norm, mla_k_norm, w_mla_out, w_out, norm2_g, w_up, w_down):
    wi = w_in[l]
    kpe_cols = wi[:, N_MAIN:N_MAIN + MLA_ROPE]
    zc = jnp.zeros((D_MODEL, MLA_ROPE), F32)
    w1 = jnp.concatenate([wi[:, :N_MAIN], kpe_cols, zc, kpe_cols, zc], axis=1).astype(BF16)
    wuq = _pad_heads(w_uq[l].reshape(MLA_Q_RANK, MLA_HEADS, MLA_QK)).reshape(MLA_Q_RANK, MLA_PAD)
    wukv3 = w_ukv[l].reshape(MLA_KV_RANK, MLA_HEADS, MLA_NOPE + MLA_V)
    wkn = jnp.pad(wukv3[:, :, :MLA_NOPE], ((0, 0), (0, 0), (0, LANES - MLA_NOPE))).reshape(MLA_KV_RANK, MLA_PAD)
    wv = wukv3[:, :, MLA_NOPE:].reshape(MLA_KV_RANK, MLA_W)
    return dict(
        norm1_g=norm1_g[l][None], norm2_g=norm2_g[l][None],
        w1=w1, wg=wi[:, N_MAIN + MLA_ROPE:].astype(BF16),
        gq=jnp.tile(diff_q_norm[l], DIFF_W // DIFF_HD)[None],
        gk=jnp.tile(diff_k_norm[l], DIFF_W // DIFF_HD)[None],
        gcq=mla_q_a_norm[l][None], gckv=mla_kv_a_norm[l][None],
        wuq=wuq.astype(BF16), wukv=jnp.concatenate([wkn, wv], axis=1).astype(BF16),
        gmq=jnp.tile(_pad_heads(mla_q_norm[l]), MLA_HEADS)[None],
        gmk=jnp.tile(_pad_heads(mla_k_norm[l]), MLA_HEADS)[None],
        conv_dw=jnp.pad(conv_dw[l], ((0, 1), (0, 0))), conv_b=conv_b[l][None], conv_g=conv_norm_g[l][None],
        lam=diff_lambda[l], subg=diff_subln[l][:, None],
        w_conv_out=w_conv_out[l].astype(BF16), w_diff_out=w_diff_out[l].astype(BF16),
        w_mla_out=w_mla_out[l].astype(BF16), w_out=w_out[l].astype(BF16),
        w_up=w_up[l].astype(BF16), w_down=w_down[l].astype(BF16),
    )


def kernel(x_prompt, x_sample, cache_diff_k, cache_diff_v, cache_mla_ckv, cache_mla_kpe, c, c_ctx, mod_w, mod_b, norm1_g, w_in, conv_dw, conv_b, conv_norm_g, w_conv_out, diff_q_norm, diff_k_norm, diff_lambda, diff_subln, w_diff_out, mla_q_a_norm, mla_kv_a_norm, w_uq, w_ukv, mla_q_norm, mla_k_norm, w_mla_out, w_out, norm2_g, w_up, w_down):
    batch, seq, _ = x_prompt.shape
    dec_batch, dec_seq, _ = x_sample.shape
    past = cache_diff_k.shape[2]
    assert seq == TOK_TILE and past == TOK_TILE and dec_seq % TOK_TILE == 0 and 1 + dec_batch <= 8

    cond_rows = jnp.concatenate([c_ctx[None], c, jnp.zeros((8 - 1 - dec_batch, D_MODEL), F32)], axis=0)
    mod = _modulation(cond_rows, mod_w, mod_b).reshape(DEPTH, 8, 6, D_MODEL)
    tabs = _rope_tables(dec_seq)

    xp = x_prompt.reshape(batch * seq, D_MODEL)
    xs = x_sample.reshape(dec_batch * dec_seq, D_MODEL)
    st_k, st_v, st_ckv, st_kpe = [], [], [], []
    for l in range(DEPTH):
        W = _layer_weights(l, norm1_g, w_in, conv_dw, conv_b, conv_norm_g, w_conv_out, diff_q_norm,
                           diff_k_norm, diff_lambda, diff_subln, w_diff_out, mla_q_a_norm, mla_kv_a_norm,
                           w_uq, w_ukv, mla_q_norm, mla_k_norm, w_mla_out, w_out, norm2_g, w_up, w_down)
        lam_init = 0.8 - 0.6 * math.exp(-0.3 * l)

        u, qd, qm, kd, km, vdt, vmt, sk, sv, sckv, skpe = _inproj(
            xp, mod[l], W, None, None, batch=batch, row0=0, states=True)
        st_k.append(sk.reshape(batch, seq, DIFF_HEADS, 2 * DIFF_HD))
        st_v.append(sv.reshape(batch, seq, DIFF_HEADS, 2 * DIFF_HD))
        st_ckv.append(sckv.reshape(batch, seq, MLA_KV_RANK))
        st_kpe.append(skpe.reshape(batch, seq, MLA_ROPE))
        od, om = _attention(qd, qm, kd, km, vdt, vmt, W["lam"], W["subg"], batch=batch, lam_init=lam_init)
        xp = _merge_mlp(xp, mod[l], u, od, om, W, tiles_per_seq=1, row0=0)

        cache = (cache_diff_k[:, l].reshape(dec_batch * past, DIFF_W),
                 cache_diff_v[:, l].reshape(dec_batch * past, DIFF_W),
                 cache_mla_ckv[:, l].reshape(dec_batch * past, MLA_KV_RANK),
                 jnp.pad(cache_mla_kpe[:, l].reshape(dec_batch * past, MLA_ROPE),
                         ((0, 0), (MLA_NOPE, LANES - MLA_QK))))
        u, qd, qm, kd, km, vdt, vmt = _inproj(xs, mod[l], W, tabs, cache, batch=dec_batch, row0=1, states=False)
        od, om = _attention(qd, qm, kd, km, vdt, vmt, W["lam"], W["subg"], batch=dec_batch, lam_init=lam_init)
        xs = _merge_mlp(xs, mod[l], u, od, om, W, tiles_per_seq=dec_seq // TOK_TILE, row0=1)

    return (xp.reshape(batch, seq, D_MODEL), xs.reshape(dec_batch, dec_seq, D_MODEL),
            jnp.stack(st_k, axis=1), jnp.stack(st_v, axis=1), jnp.stack(st_ckv, axis=1), jnp.stack(st_kpe, axis=1))
```

```python
import functools
import math

import numpy as np
import jax
import jax.numpy as jnp
from jax import lax
from jax.experimental import pallas as pl
from jax.experimental.pallas import tpu as pltpu

D_MODEL = 1024
DEPTH = 2
GRID_W = 64
ROPE_BASE = 10000.0
CONV_W = 512
CONV_K = 31
DIFF_HEADS = 4
DIFF_HD = 64
DIFF_W = DIFF_HEADS * 2 * DIFF_HD
MLA_HEADS = 8
MLA_NOPE = 64
MLA_ROPE = 32
MLA_QK = MLA_NOPE + MLA_ROPE
MLA_V = 64
MLA_Q_RANK = 384
MLA_KV_RANK = 256
MLA_W = MLA_HEADS * MLA_V
D_FF = 4 * D_MODEL
EPS = 1e-6

LANES = 128
SUBLANES = 8
MXU_W = 256
MLA_PAD = MLA_HEADS * LANES
TOK_TILE = 256
HALO = 16
N_MAIN = 2 * CONV_W + 3 * DIFF_W + MLA_Q_RANK + MLA_KV_RANK
VMEM_LIMIT = 56 * 1024 * 1024
LOG2E = math.log2(math.e)
F32 = jnp.float32
BF16 = jnp.bfloat16


def _silu(x):
    return x * jax.nn.sigmoid(x)


def _rms_full(x, g):
    ms = jnp.mean(x * x, axis=-1, keepdims=True)
    return x * lax.rsqrt(ms + EPS) * g


def _lane_tiles(x):
    return jnp.concatenate([x[:, t:t + MXU_W] for t in range(0, x.shape[-1], MXU_W)], axis=0)


def _from_lane_tiles(y, rows):
    return jnp.concatenate([y[r:r + rows] for r in range(0, y.shape[0], rows)], axis=-1)


def _group_rms(x, seg_ref, n, g):
    rows = x.shape[0]
    xt = _lane_tiles(x)
    ss = jnp.dot((xt * xt).astype(BF16), seg_ref[...], preferred_element_type=F32)
    return _from_lane_tiles(xt * lax.rsqrt(ss * (1.0 / n) + EPS), rows) * g


def _rope(x, perm_ref, cos, sin_signed):
    rows = x.shape[0]
    sw = jnp.dot(_lane_tiles(x).astype(BF16), perm_ref[...], preferred_element_type=F32)
    return x * cos + _from_lane_tiles(sw, rows) * sin_signed


def _tile_lanes(t, n):
    return jnp.concatenate([t] * n, axis=-1)


def _modulated_norm(x, g, shift, scale):
    return _rms_full(x, g) * (1.0 + scale) + shift


def _mod_kernel(s_ref, w_ref, b_ref, o_ref):
    s = _silu(s_ref[...]).astype(BF16)
    o_ref[...] = jnp.dot(s, w_ref[...].astype(BF16), preferred_element_type=F32) + b_ref[...]


def _modulation(cond_rows, mod_w, mod_b):
    tn = 1536
    n = 6 * D_MODEL
    return pl.pallas_call(
        _mod_kernel,
        grid=(DEPTH, n // tn),
        in_specs=[
            pl.BlockSpec((8, D_MODEL), lambda l, j: (0, 0)),
            pl.BlockSpec((None, D_MODEL, tn), lambda l, j: (l, 0, j)),
            pl.BlockSpec((None, 1, tn), lambda l, j: (l, 0, j)),
        ],
        out_specs=pl.BlockSpec((None, 8, tn), lambda l, j: (l, 0, j)),
        out_shape=jax.ShapeDtypeStruct((DEPTH, 8, n), F32),
        compiler_params=pltpu.CompilerParams(
            dimension_semantics=("parallel", "parallel"), vmem_limit_bytes=VMEM_LIMIT),
        name="modulation",
    )(cond_rows, mod_w, mod_b.reshape(DEPTH, 1, n))


def _mla_keys_values(ckvn, kpe_hi, wukv_ref):
    kv = jnp.dot(ckvn.astype(BF16), wukv_ref[...], preferred_element_type=F32)
    return kv[:, :MLA_PAD] + _tile_lanes(kpe_hi, MLA_HEADS), kv[:, MLA_PAD:]


def _inproj_kernel(*refs, rope, states, cache, n_prev):
    it = iter(refs)
    x_ref, mod_ref, g1_ref, w_ref, wkpe_ref = next(it), next(it), next(it), next(it), next(it)
    gq_ref, gk_ref, gcq_ref, gckv_ref = next(it), next(it), next(it), next(it)
    wuq_ref, wukv_ref, gmq_ref, gmk_ref = next(it), next(it), next(it), next(it)
    segd_ref, segm_ref = next(it), next(it)
    if rope:
        permd_ref, permm_ref = next(it), next(it)
        cd_ref, sd_ref, cm_ref, sm_ref = next(it), next(it), next(it), next(it)
    if cache:
        ck_ref, cv_ref, cckv_ref, ckpe_ref = next(it), next(it), next(it), next(it)
    prev_refs = [next(it) for _ in range(4)] if n_prev else []
    u_ref, qd_ref, qm_ref, kd_ref, km_ref, vdt_ref, vmt_ref = (next(it) for _ in range(7))
    if states:
        state_refs = sk_ref, sv_ref, sckv_ref, skpe_ref = next(it), next(it), next(it), next(it)
    tm = x_ref.shape[0]

    def tokens():
        mod = mod_ref[...]
        h = _modulated_norm(x_ref[...], g1_ref[...], mod[0:1], mod[1:2]).astype(BF16)
        proj = jnp.dot(h, w_ref[...], preferred_element_type=F32)

        o = 0
        u_a = proj[:, o:o + CONV_W]; o += CONV_W
        u_g = proj[:, o:o + CONV_W]; o += CONV_W
        dq = proj[:, o:o + DIFF_W]; o += DIFF_W
        dk = proj[:, o:o + DIFF_W]; o += DIFF_W
        dv = proj[:, o:o + DIFF_W]; o += DIFF_W
        cq = proj[:, o:o + MLA_Q_RANK]; o += MLA_Q_RANK
        ckv = proj[:, o:o + MLA_KV_RANK]; o += MLA_KV_RANK
        kpe2 = jnp.dot(h, wkpe_ref[...], preferred_element_type=F32)

        cqn = _rms_full(cq, gcq_ref[...])
        ckvn = _rms_full(ckv, gckv_ref[...])
        q = _group_rms(dq, segd_ref, DIFF_HD, gq_ref[...])
        k = _group_rms(dk, segd_ref, DIFF_HD, gk_ref[...])
        qm = jnp.dot(cqn.astype(BF16), wuq_ref[...], preferred_element_type=F32)
        lane = lax.broadcasted_iota(jnp.int32, (1, LANES), 1)
        km, vm = _mla_keys_values(ckvn, jnp.where(lane >= MLA_NOPE, kpe2, 0.0), wukv_ref)

        u_ref[...] = u_a * jax.nn.sigmoid(u_g)
        vdt_ref[...] = dv.T.astype(BF16)
        if states:
            for dst, src in zip(state_refs, prev_refs):
                dst[0:n_prev] = src[...]
            for hd in range(DIFF_HEADS):
                sl = slice(hd * LANES, (hd + 1) * LANES)
                sk_ref[n_prev, pl.ds(hd, tm, stride=DIFF_HEADS), :] = k[:, sl]
                sv_ref[n_prev, pl.ds(hd, tm, stride=DIFF_HEADS), :] = dv[:, sl]
            sckv_ref[n_prev] = ckvn
            skpe_ref[n_prev] = kpe2[:, :MLA_ROPE]

        qm = _group_rms(qm, segm_ref, MLA_QK, gmq_ref[...])
        km = _group_rms(km, segm_ref, MLA_QK, gmk_ref[...])
        if rope:
            cd = _tile_lanes(cd_ref[...], DIFF_W // LANES)
            sd = _tile_lanes(sd_ref[...], DIFF_W // LANES)
            q = _rope(q, permd_ref, cd, sd)
            k = _rope(k, permd_ref, cd, sd)
        qd_ref[...] = (q * (DIFF_HD ** -0.5 * LOG2E)).astype(BF16)
        kd_ref[...] = k.astype(BF16)
        vmt_ref[...] = vm.T.astype(BF16)
        if rope:
            cm = _tile_lanes(cm_ref[...], MLA_HEADS)
            sm = _tile_lanes(sm_ref[...], MLA_HEADS)
            qm = _rope(qm, permm_ref, cm, sm)
            km = _rope(km, permm_ref, cm, sm)
        qm_ref[...] = (qm * (MLA_QK ** -0.5 * LOG2E)).astype(BF16)
        km_ref[...] = km.astype(BF16)

    def cached_context():
        heads = lambda ref: jnp.concatenate([ref[:, hd, :] for hd in range(DIFF_HEADS)], axis=-1)
        kd_ref[...] = heads(ck_ref).astype(BF16)
        vdt_ref[...] = heads(cv_ref).T.astype(BF16)
        km, vm = _mla_keys_values(cckv_ref[...], ckpe_ref[...], wukv_ref)
        km_ref[...] = _group_rms(km, segm_ref, MLA_QK, gmk_ref[...]).astype(BF16)
        vmt_ref[...] = vm.T.astype(BF16)

    if cache:
        j = pl.program_id(1)
        pl.when(j == 0)(cached_context)
        pl.when(j > 0)(tokens)
    else:
        tokens()


def _layer_spec(w, l, grid_rank, **kw):
    index = (lambda i: (l, 0, 0)) if grid_rank == 1 else (lambda b, j: (l, 0, 0))
    return pl.BlockSpec((None,) + w.shape[1:], index, **kw)


def _inproj(x, mod_l, W, l, rope_tabs, cache, *, batch, row0, states, prev_states=None):
    t = x.shape[0]
    tm = TOK_TILE
    n = t // batch // tm
    rope = rope_tabs is not None
    has_cache = cache is not None
    nk = n + 1 if has_cache else n
    if has_cache:
        grid = (batch, nk)
        tok = lambda b, j: (b * n + jnp.maximum(j - 1, 0), 0)
        key = lambda b, j: (b * nk + j, 0)
        keyt = lambda b, j: (b, 0, j)
        const = lambda b, j: (0, 0)
        modrow = lambda b, j: (row0 + b, 0, 0)
        tab = lambda b, j: (jnp.maximum(j - 1, 0), 0)
    else:
        grid = (batch * n,)
        tok = lambda i: (i, 0)
        key = tok
        keyt = lambda i: (i // n, 0, i % n)
        const = lambda i: (0, 0)
        modrow = lambda i: (row0 + (i // n if row0 else 0), 0, 0)
        tab = lambda i: (i % n, 0)
    names = ("norm1_g", "w_in", "wkpe", "gq", "gk", "gcq", "gckv", "wuq", "wukv", "gmq", "gmk")
    in_specs = [pl.BlockSpec((tm, D_MODEL), tok), pl.BlockSpec((None, 6, D_MODEL), modrow)]
    in_specs += [_layer_spec(W[k], l, len(grid)) for k in names]
    in_specs[2 + names.index("w_in")] = pl.BlockSpec((None, D_MODEL, N_MAIN), in_specs[2].index_map)
    in_specs += [pl.BlockSpec((MXU_W, MXU_W), const)] * 2
    seg_d, seg_m, perm_d, perm_m = _group_matrices()
    args = [x, mod_l] + [W[k] for k in names] + [seg_d, seg_m]
    if rope:
        in_specs += [pl.BlockSpec((MXU_W, MXU_W), const)] * 2 + [pl.BlockSpec((tm, LANES), tab)] * 4
        args += [perm_d, perm_m] + list(rope_tabs)
    if has_cache:
        in_specs += [pl.BlockSpec((None, None) + a.shape[2:], lambda b, j, r=a.ndim - 2: (b, l) + (0,) * r)
                     for a in cache]
        args += list(cache)
    sk = nk * tm
    out_shape = [
        jax.ShapeDtypeStruct((t, CONV_W), F32),
        jax.ShapeDtypeStruct((t, DIFF_W), BF16),
        jax.ShapeDtypeStruct((t, MLA_PAD), BF16),
        jax.ShapeDtypeStruct((batch * sk, DIFF_W), BF16),
        jax.ShapeDtypeStruct((batch * sk, MLA_PAD), BF16),
        jax.ShapeDtypeStruct((batch, DIFF_W, sk), BF16),
        jax.ShapeDtypeStruct((batch, MLA_W, sk), BF16),
    ]
    out_specs = [
        pl.BlockSpec((tm, CONV_W), tok),
        pl.BlockSpec((tm, DIFF_W), tok),
        pl.BlockSpec((tm, MLA_PAD), tok),
        pl.BlockSpec((tm, DIFF_W), key),
        pl.BlockSpec((tm, MLA_PAD), key),
        pl.BlockSpec((None, DIFF_W, tm), keyt),
        pl.BlockSpec((None, MLA_W, tm), keyt),
    ]
    n_prev = 0
    if states:
        assert n == 1
        n_prev = l
        st_shapes = ((tm * DIFF_HEADS, LANES), (tm * DIFF_HEADS, LANES), (tm, MLA_KV_RANK), (tm, MLA_ROPE))
        seq_block = lambda layers, s: pl.BlockSpec((None, layers) + s, lambda i: (i, 0, 0, 0))
        out_shape += [jax.ShapeDtypeStruct((batch, l + 1) + s, F32) for s in st_shapes]
        out_specs += [seq_block(l + 1, s) for s in st_shapes]
        if n_prev:
            in_specs += [seq_block(l, s) for s in st_shapes]
            args += list(prev_states)
    return pl.pallas_call(
        functools.partial(_inproj_kernel, rope=rope, states=states, cache=has_cache, n_prev=n_prev),
        grid=grid,
        in_specs=in_specs,
        out_specs=out_specs,
        out_shape=out_shape,
        compiler_params=pltpu.CompilerParams(
            dimension_semantics=("arbitrary",) * len(grid), vmem_limit_bytes=VMEM_LIMIT),
        name="inproj_latent" if has_cache else "inproj_context",
    )(*args)


def _fold_rows(x, op):
    n = x.shape[0]
    while n % (2 * SUBLANES) == 0 and n > 4 * SUBLANES:
        n //= 2
        x = op(x[:n], x[n:])
    return x


KEY_CHUNK = 256


def _attn_kernel(qd_ref, qm_ref, kd_ref, km_ref, vdt_ref, vmt_ref, lam_ref, sub_ref, od_ref, om_ref, s_ref, p_ref,
                 *, lam_init):
    lp = lam_ref[...]
    lam = (jnp.exp(jnp.sum(lp[0:1] * lp[1:2], axis=-1, keepdims=True))
           - jnp.exp(jnp.sum(lp[2:3] * lp[3:4], axis=-1, keepdims=True)) + lam_init)
    lo = lax.broadcasted_iota(jnp.int32, (1, LANES), 1) < DIFF_HD
    top = lax.broadcasted_iota(jnp.int32, (LANES, 1), 0) < MLA_V
    subg = sub_ref[...]
    sk = kd_ref.shape[0]
    kc = min(KEY_CHUNK, sk)
    n_chunks = sk // kc

    jobs = []
    for h in range(DIFF_HEADS):
        sl = slice(h * LANES, (h + 1) * LANES)
        jobs += [(kd_ref, sl, qd_ref, 0, vdt_ref, sl), (kd_ref, sl, qd_ref, 1, vdt_ref, sl)]
    for h in range(MLA_HEADS):
        hs = slice(h * LANES, (h + 1) * LANES)
        pair = slice((h // 2) * LANES, (h // 2 + 1) * LANES)
        jobs.append((km_ref, hs, qm_ref, None, vmt_ref, pair))
    n = len(jobs)

    def query(i):
        _, sl, q_ref, half, _, _ = jobs[i]
        if half is None:
            return q_ref[:, sl]
        q = q_ref[:, sl].astype(F32)
        return (jnp.where(lo, q, 0.0) if half == 0 else jnp.where(lo, 0.0, q)).astype(BF16)

    qs, mcol, mfin, lcol, acc = {}, {}, {}, {}, {}
    outs = [None] * n
    for i in range(-2, n + 1):
        a, b, c = i + 2, i, i - 1
        if 0 <= a < n:
            qs[a] = query(a)
        if 0 <= b < n:
            mfin[b] = jnp.max(mcol.pop(b), axis=0, keepdims=True)
        for r in range(n_chunks):
            rows = slice(r * kc, (r + 1) * kc)
            if 0 <= a < n:
                k_ref, ksl = jobs[a][0], jobs[a][1]
                s = lax.dot_general(k_ref[rows, ksl], qs[a], (((1,), (1,)), ((), ())),
                                    preferred_element_type=F32)
                s_ref[a % 3, rows, :] = s
                f = _fold_rows(s, jnp.maximum)
                mcol[a] = f if r == 0 else jnp.maximum(mcol[a], f)
            if 0 <= b < n:
                p = jnp.exp2(s_ref[b % 3, rows, :] - mfin[b])
                f = _fold_rows(p, jnp.add)
                lcol[b] = f if r == 0 else lcol[b] + f
                p_ref[b % 2, rows, :] = p.astype(BF16)
            if 0 <= c < n:
                vt_ref, vrows = jobs[c][4], jobs[c][5]
                d = jnp.dot(vt_ref[vrows, rows], p_ref[c % 2, rows, :], preferred_element_type=F32)
                acc[c] = d if r == 0 else acc[c] + d
        if 0 <= c < n:
            outs[c] = acc.pop(c) / jnp.sum(lcol.pop(c), axis=0, keepdims=True)

    for h in range(DIFF_HEADS):
        sl = slice(h * LANES, (h + 1) * LANES)
        a = outs[2 * h] - lam * outs[2 * h + 1]
        ms = jnp.mean(a * a, axis=0, keepdims=True)
        od = a * lax.rsqrt(ms + EPS) * subg * (1.0 - lam_init)
        od_ref[:, sl] = od.T.astype(BF16)
    base = 2 * DIFF_HEADS
    for hp in range(MLA_HEADS // 2):
        sl = slice(hp * LANES, (hp + 1) * LANES)
        om_ref[:, sl] = jnp.where(top, outs[base + 2 * hp], outs[base + 2 * hp + 1]).T.astype(BF16)


def _attention(qd, qm, kd, km, vdt, vmt, W, l, *, batch, lam_init):
    sq = qd.shape[0] // batch
    sk = kd.shape[0] // batch
    tq = TOK_TILE
    nq = sq // tq
    qrow = lambda b, i: (b * nq + i, 0)
    krow = lambda b, i: (b, 0)
    kcol = lambda b, i: (b, 0, 0)
    const = lambda b, i: (0, 0)
    return pl.pallas_call(
        functools.partial(_attn_kernel, lam_init=lam_init),
        grid=(batch, nq),
        in_specs=[
            pl.BlockSpec((tq, DIFF_W), qrow),
            pl.BlockSpec((tq, MLA_PAD), qrow),
            pl.BlockSpec((sk, DIFF_W), krow),
            pl.BlockSpec((sk, MLA_PAD), krow),
            pl.BlockSpec((None, DIFF_W, sk), kcol),
            pl.BlockSpec((None, MLA_W, sk), kcol),
            _layer_spec(W["lam"], l, 2),
            _layer_spec(W["subg"], l, 2),
        ],
        out_specs=[pl.BlockSpec((tq, DIFF_W), qrow), pl.BlockSpec((tq, MLA_W), qrow)],
        out_shape=[jax.ShapeDtypeStruct((batch * sq, DIFF_W), BF16),
                   jax.ShapeDtypeStruct((batch * sq, MLA_W), BF16)],
        scratch_shapes=[pltpu.VMEM((3, sk, tq), F32), pltpu.VMEM((2, sk, tq), BF16)],
        compiler_params=pltpu.CompilerParams(
            dimension_semantics=("parallel", "arbitrary"), vmem_limit_bytes=VMEM_LIMIT),
        name="attention",
    )(qd, qm, kd, km, vdt, vmt, W["lam"], W["subg"])


CONV_SPAN = (CONV_K + SUBLANES - 1) // SUBLANES * SUBLANES - SUBLANES


def _depthwise_conv(buf_ref, phase_ref, dw_ref, tm):
    base = HALO - CONV_K // 2
    acc = None
    for r in range(SUBLANES):
        offs = [o for o in range(base, base + CONV_K) if o % SUBLANES == r]
        phase_ref[r] = buf_ref[r:r + tm + CONV_SPAN, :]
        for o in offs:
            term = phase_ref[r, o - r:o - r + tm, :] * dw_ref[o - base:o - base + 1, :]
            acc = term if acc is None else acc + term
    return acc


def _merge_mlp_kernel(*refs, tiles_per_seq):
    it = iter(refs)
    x_ref, mod_ref, g1_ref, g2_ref, u_ref = (next(it) for _ in range(5))
    if tiles_per_seq > 1:
        up_ref, un_ref = next(it), next(it)
    dw_ref, cb_ref, cg_ref, od_ref, om_ref = (next(it) for _ in range(5))
    wg_ref, wc_ref, wd_ref, wm_ref, wo_ref, wup_ref, wdn_ref, y_ref, buf_ref, phase_ref = (next(it) for _ in range(10))
    tm = x_ref.shape[0]

    mod = mod_ref[...]
    x = x_ref[...]
    h = _modulated_norm(x, g1_ref[...], mod[0:1], mod[1:2])
    gates = jax.nn.sigmoid(jnp.dot(h.astype(BF16), wg_ref[...], preferred_element_type=F32))
    o_diff = jnp.dot(od_ref[...], wd_ref[...], preferred_element_type=F32)
    o_mla = jnp.dot(om_ref[...], wm_ref[...], preferred_element_type=F32)

    zeros = jnp.zeros((HALO, CONV_W), F32)
    if tiles_per_seq > 1:
        j = pl.program_id(0) % tiles_per_seq
        prev = jnp.where(j == 0, zeros, up_ref[...])
        nxt = jnp.where(j == tiles_per_seq - 1, zeros, un_ref[...])
    else:
        prev, nxt = zeros, zeros
    buf_ref[0:HALO, :] = prev
    buf_ref[HALO:HALO + tm, :] = u_ref[...]
    buf_ref[HALO + tm:HALO + tm + HALO, :] = nxt
    conv = _depthwise_conv(buf_ref, phase_ref, dw_ref, tm) + cb_ref[...]
    c = _silu(_rms_full(conv, cg_ref[...])).astype(BF16)
    o_conv = jnp.dot(c, wc_ref[...], preferred_element_type=F32)
    merged = (gates[:, 0:D_MODEL] * o_conv + gates[:, D_MODEL:2 * D_MODEL] * o_diff
              + gates[:, 2 * D_MODEL:3 * D_MODEL] * o_mla)
    x = x + mod[2:3] * jnp.dot(merged.astype(BF16), wo_ref[...], preferred_element_type=F32)
    h2 = _modulated_norm(x, g2_ref[...], mod[3:4], mod[4:5])
    up = jnp.dot(h2.astype(BF16), wup_ref[...], preferred_element_type=F32)
    act = jnp.square(jnp.maximum(up, 0.0)).astype(BF16)
    y_ref[...] = x + mod[5:6] * jnp.dot(act, wdn_ref[...], preferred_element_type=F32)


def _merge_mlp(x, mod_l, u, od, om, W, l, *, tiles_per_seq, row0):
    t = x.shape[0]
    tm = TOK_TILE
    nb = t // HALO
    per = tm // HALO
    row = lambda i: (i, 0)

    in_specs = [
        pl.BlockSpec((tm, D_MODEL), row),
        pl.BlockSpec((None, 6, D_MODEL), lambda i: (row0 + (i // tiles_per_seq if row0 else 0), 0, 0)),
        _layer_spec(W["norm1_g"], l, 1),
        _layer_spec(W["norm2_g"], l, 1),
        pl.BlockSpec((tm, CONV_W), row),
    ]
    args = [x, mod_l, W["norm1_g"], W["norm2_g"], u]
    if tiles_per_seq > 1:
        in_specs += [
            pl.BlockSpec((HALO, CONV_W), lambda i: (jnp.maximum(i * per - 1, 0), 0)),
            pl.BlockSpec((HALO, CONV_W), lambda i: (jnp.minimum((i + 1) * per, nb - 1), 0)),
        ]
        args += [u, u]
    big = ("wg", "w_conv_out", "w_diff_out", "w_mla_out", "w_out", "w_up", "w_down")
    in_specs += [_layer_spec(W[k], l, 1) for k in ("conv_dw", "conv_b", "conv_g")]
    in_specs += [pl.BlockSpec((tm, DIFF_W), row), pl.BlockSpec((tm, MLA_W), row)]
    in_specs += [_layer_spec(W[k], l, 1, pipeline_mode=pl.Buffered(1)) for k in big]
    args += [W["conv_dw"], W["conv_b"], W["conv_g"], od, om] + [W[k] for k in big]
    return pl.pallas_call(
        functools.partial(_merge_mlp_kernel, tiles_per_seq=tiles_per_seq),
        grid=(t // tm,),
        in_specs=in_specs,
        out_specs=pl.BlockSpec((tm, D_MODEL), row),
        out_shape=jax.ShapeDtypeStruct((t, D_MODEL), F32),
        scratch_shapes=[pltpu.VMEM((tm + 2 * HALO, CONV_W), F32),
                        pltpu.VMEM((SUBLANES, tm + CONV_SPAN, CONV_W), F32)],
        compiler_params=pltpu.CompilerParams(
            dimension_semantics=("parallel",), vmem_limit_bytes=VMEM_LIMIT),
        name="merge_mlp",
    )(*args)


def _rope_tables(seq_len):
    rows = seq_len // GRID_W
    row = np.repeat(np.arange(rows, dtype=np.float64), GRID_W)
    col = np.tile(np.arange(GRID_W, dtype=np.float64), rows)

    def tables(rot_dim):
        half = rot_dim // 2
        inv = ROPE_BASE ** (-np.arange(0, half, 2, dtype=np.float64) / half)
        a0 = row[:, None] * inv
        a1 = col[:, None] * inv
        cos = np.concatenate([np.cos(a0), np.cos(a0), np.cos(a1), np.cos(a1)], axis=1)
        sin = np.concatenate([-np.sin(a0), np.sin(a0), -np.sin(a1), np.sin(a1)], axis=1)
        return cos, sin

    cd, sd = tables(DIFF_HD)
    cd, sd = np.tile(cd, (1, LANES // DIFF_HD)), np.tile(sd, (1, LANES // DIFF_HD))
    cm, sm = tables(MLA_ROPE)
    pad = LANES - MLA_QK
    cm = np.concatenate([np.ones((seq_len, MLA_NOPE)), cm, np.ones((seq_len, pad))], axis=1)
    sm = np.concatenate([np.zeros((seq_len, MLA_NOPE)), sm, np.zeros((seq_len, pad))], axis=1)
    return tuple(jnp.asarray(t, dtype=F32) for t in (cd, sd, cm, sm))


def _group_matrices():
    lane = np.arange(MXU_W)
    seg_d = (lane[:, None] // DIFF_HD == lane[None, :] // DIFF_HD)
    seg_m = (lane[:, None] // LANES == lane[None, :] // LANES)

    def swap(rotary, half):
        src = np.where(lane % (2 * half) < half, lane + half, lane - half)
        return (lane[:, None] == src[None, :]) & rotary[None, :]

    perm_d = swap(np.ones(MXU_W, bool), DIFF_HD // 4)
    in_head = lane % LANES
    perm_m = swap((in_head >= MLA_NOPE) & (in_head < MLA_QK), MLA_ROPE // 4)
    return tuple(jnp.asarray(m, dtype=BF16) for m in (seg_d, seg_m, perm_d, perm_m))


def _pad_heads(v):
    return jnp.pad(v, [(0, 0)] * (v.ndim - 1) + [(0, LANES - MLA_QK)])


def _prepare_weights(norm1_g, w_in, conv_dw, conv_b, conv_norm_g, w_conv_out, diff_q_norm, diff_k_norm,
                     diff_lambda, diff_subln, w_diff_out, mla_q_a_norm, mla_kv_a_norm, w_uq, w_ukv,
                     mla_q_norm, mla_k_norm, w_mla_out, w_out, norm2_g, w_up, w_down):
    row = lambda v: v[:, None, :]
    w_in = w_in.astype(BF16)
    kpe_cols = w_in[:, :, N_MAIN:N_MAIN + MLA_ROPE]
    zc = jnp.zeros_like(kpe_cols)
    wuq = _pad_heads(w_uq.reshape(DEPTH, MLA_Q_RANK, MLA_HEADS, MLA_QK)).reshape(DEPTH, MLA_Q_RANK, MLA_PAD)
    wukv4 = w_ukv.reshape(DEPTH, MLA_KV_RANK, MLA_HEADS, MLA_NOPE + MLA_V)
    wkn = jnp.pad(wukv4[..., :MLA_NOPE], ((0, 0), (0, 0), (0, 0), (0, LANES - MLA_NOPE)))
    wkn = wkn.reshape(DEPTH, MLA_KV_RANK, MLA_PAD)
    wv = wukv4[..., MLA_NOPE:].reshape(DEPTH, MLA_KV_RANK, MLA_W)
    return dict(
        norm1_g=row(norm1_g), norm2_g=row(norm2_g),
        w_in=w_in,
        wkpe=jnp.concatenate([kpe_cols, zc, kpe_cols, zc], axis=-1),
        wg=w_in[:, :, N_MAIN + MLA_ROPE:],
        gq=row(jnp.tile(diff_q_norm, (1, DIFF_W // DIFF_HD))),
        gk=row(jnp.tile(diff_k_norm, (1, DIFF_W // DIFF_HD))),
        gcq=row(mla_q_a_norm), gckv=row(mla_kv_a_norm),
        wuq=wuq.astype(BF16), wukv=jnp.concatenate([wkn, wv], axis=-1).astype(BF16),
        gmq=row(jnp.tile(_pad_heads(mla_q_norm), (1, MLA_HEADS))),
        gmk=row(jnp.tile(_pad_heads(mla_k_norm), (1, MLA_HEADS))),
        conv_dw=jnp.pad(conv_dw, ((0, 0), (0, 1), (0, 0))), conv_b=row(conv_b), conv_g=row(conv_norm_g),
        lam=diff_lambda, subg=diff_subln[:, :, None],
        w_conv_out=w_conv_out.astype(BF16), w_diff_out=w_diff_out.astype(BF16),
        w_mla_out=w_mla_out.astype(BF16), w_out=w_out.astype(BF16),
        w_up=w_up.astype(BF16), w_down=w_down.astype(BF16),
    )


def kernel(x_prompt, x_sample, cache_diff_k, cache_diff_v, cache_mla_ckv, cache_mla_kpe, c, c_ctx, mod_w, mod_b, norm1_g, w_in, conv_dw, conv_b, conv_norm_g, w_conv_out, diff_q_norm, diff_k_norm, diff_lambda, diff_subln, w_diff_out, mla_q_a_norm, mla_kv_a_norm, w_uq, w_ukv, mla_q_norm, mla_k_norm, w_mla_out, w_out, norm2_g, w_up, w_down):
    batch, seq, _ = x_prompt.shape
    dec_batch, dec_seq, _ = x_sample.shape
    past = cache_diff_k.shape[2]
    assert seq == TOK_TILE and past == TOK_TILE and dec_seq % TOK_TILE == 0 and 1 + dec_batch <= 8

    cond_rows = jnp.concatenate([c_ctx[None], c, jnp.zeros((8 - 1 - dec_batch, D_MODEL), F32)], axis=0)
    mod = _modulation(cond_rows, mod_w, mod_b).reshape(DEPTH, 8, 6, D_MODEL)
    tabs = _rope_tables(dec_seq)

    W = _prepare_weights(norm1_g, w_in, conv_dw, conv_b, conv_norm_g, w_conv_out, diff_q_norm, diff_k_norm,
                         diff_lambda, diff_subln, w_diff_out, mla_q_a_norm, mla_kv_a_norm, w_uq, w_ukv,
                         mla_q_norm, mla_k_norm, w_mla_out, w_out, norm2_g, w_up, w_down)
    cache = (cache_diff_k, cache_diff_v, cache_mla_ckv,
             jnp.pad(cache_mla_kpe, ((0, 0), (0, 0), (0, 0), (MLA_NOPE, LANES - MLA_QK))))

    xp = x_prompt.reshape(batch * seq, D_MODEL)
    xs = x_sample.reshape(dec_batch * dec_seq, D_MODEL)
    new_state = None
    for l in range(DEPTH):
        lam_init = 0.8 - 0.6 * math.exp(-0.3 * l)

        u, qd, qm, kd, km, vdt, vmt, *new_state = _inproj(
            xp, mod[l], W, l, None, None, batch=batch, row0=0, states=True, prev_states=new_state)
        od, om = _attention(qd, qm, kd, km, vdt, vmt, W, l, batch=batch, lam_init=lam_init)
        xp = _merge_mlp(xp, mod[l], u, od, om, W, l, tiles_per_seq=1, row0=0)

        u, qd, qm, kd, km, vdt, vmt = _inproj(xs, mod[l], W, l, tabs, cache, batch=dec_batch, row0=1, states=False)
        od, om = _attention(qd, qm, kd, km, vdt, vmt, W, l, batch=dec_batch, lam_init=lam_init)
        xs = _merge_mlp(xs, mod[l], u, od, om, W, l, tiles_per_seq=dec_seq // TOK_TILE, row0=1)

    sk, sv, sckv, skpe = new_state
    return (xp.reshape(batch, seq, D_MODEL), xs.reshape(dec_batch, dec_seq, D_MODEL),
            sk.reshape(batch, DEPTH, seq, DIFF_HEADS, 2 * DIFF_HD),
            sv.reshape(batch, DEPTH, seq, DIFF_HEADS, 2 * DIFF_HD), sckv, skpe)
```

```python
import functools
import math

import numpy as np
import jax
import jax.numpy as jnp
from jax import lax
from jax.experimental import pallas as pl
from jax.experimental.pallas import tpu as pltpu

D_MODEL = 1024
DEPTH = 2
GRID_W = 64
ROPE_BASE = 10000.0
CONV_W = 512
CONV_K = 31
DIFF_HEADS = 4
DIFF_HD = 64
DIFF_W = DIFF_HEADS * 2 * DIFF_HD
MLA_HEADS = 8
MLA_NOPE = 64
MLA_ROPE = 32
MLA_QK = MLA_NOPE + MLA_ROPE
MLA_V = 64
MLA_Q_RANK = 384
MLA_KV_RANK = 256
MLA_W = MLA_HEADS * MLA_V
D_FF = 4 * D_MODEL
EPS = 1e-6

LANES = 128
SUBLANES = 8
MXU_W = 256
MLA_PAD = MLA_HEADS * LANES
TOK_TILE = 256
HALO = 16
N_MAIN = 2 * CONV_W + 3 * DIFF_W + MLA_Q_RANK + MLA_KV_RANK
VMEM_LIMIT = 56 * 1024 * 1024
LOG2E = math.log2(math.e)
F32 = jnp.float32
BF16 = jnp.bfloat16


def _silu(x):
    return x * jax.nn.sigmoid(x)


def _rms_full(x, g):
    ms = jnp.mean(x * x, axis=-1, keepdims=True)
    return x * lax.rsqrt(ms + EPS) * g


def _lane_tiles(x):
    return jnp.concatenate([x[:, t:t + MXU_W] for t in range(0, x.shape[-1], MXU_W)], axis=0)


def _from_lane_tiles(y, rows):
    return jnp.concatenate([y[r:r + rows] for r in range(0, y.shape[0], rows)], axis=-1)


def _group_rms(x, seg_ref, n, g):
    rows = x.shape[0]
    xt = _lane_tiles(x)
    ss = jnp.dot((xt * xt).astype(BF16), seg_ref[...], preferred_element_type=F32)
    return _from_lane_tiles(xt * lax.rsqrt(ss * (1.0 / n) + EPS), rows) * g


def _rope(x, perm_ref, cos, sin_signed):
    rows = x.shape[0]
    sw = jnp.dot(_lane_tiles(x).astype(BF16), perm_ref[...], preferred_element_type=F32)
    return x * cos + _from_lane_tiles(sw, rows) * sin_signed


def _dot_nt(x, w_t):
    return lax.dot_general(x, w_t, (((1,), (1,)), ((), ())), preferred_element_type=F32)


def _tile_lanes(t, n):
    return jnp.concatenate([t] * n, axis=-1)


def _modulated_norm(x, g, shift, scale):
    return _rms_full(x, g) * (1.0 + scale) + shift


def _mod_kernel(s_ref, w_ref, b_ref, o_ref):
    s = _silu(s_ref[...]).astype(BF16)
    o_ref[...] = jnp.dot(s, w_ref[...].astype(BF16), preferred_element_type=F32) + b_ref[...]


def _modulation(cond_rows, mod_w, mod_b):
    tn = 1536
    n = 6 * D_MODEL
    return pl.pallas_call(
        _mod_kernel,
        grid=(DEPTH, n // tn),
        in_specs=[
            pl.BlockSpec((8, D_MODEL), lambda l, j: (0, 0)),
            pl.BlockSpec((None, D_MODEL, tn), lambda l, j: (l, 0, j)),
            pl.BlockSpec((None, 1, tn), lambda l, j: (l, 0, j)),
        ],
        out_specs=pl.BlockSpec((None, 8, tn), lambda l, j: (l, 0, j)),
        out_shape=jax.ShapeDtypeStruct((DEPTH, 8, n), F32),
        compiler_params=pltpu.CompilerParams(
            dimension_semantics=("parallel", "parallel"), vmem_limit_bytes=VMEM_LIMIT),
        name="modulation",
    )(cond_rows, mod_w, mod_b.reshape(DEPTH, 1, n))


def _mla_keys_values(ckvn, kpe_hi, wukv_ref):
    kv = jnp.dot(ckvn.astype(BF16), wukv_ref[...], preferred_element_type=F32)
    return kv[:, :MLA_PAD] + _tile_lanes(kpe_hi, MLA_HEADS), kv[:, MLA_PAD:]


def _inproj_kernel(*refs, rope, states, cache, n_prev):
    it = iter(refs)
    x_ref, mod_ref, g1_ref, w_ref, wkpe_ref = next(it), next(it), next(it), next(it), next(it)
    gq_ref, gk_ref, gcq_ref, gckv_ref = next(it), next(it), next(it), next(it)
    wuq_ref, wukv_ref, gmq_ref, gmk_ref = next(it), next(it), next(it), next(it)
    segd_ref, segm_ref = next(it), next(it)
    if rope:
        permd_ref, permm_ref = next(it), next(it)
        cd_ref, sd_ref, cm_ref, sm_ref = next(it), next(it), next(it), next(it)
    if cache:
        ck_ref, cv_ref, cckv_ref, ckpe_ref = next(it), next(it), next(it), next(it)
    prev_refs = [next(it) for _ in range(4)] if n_prev else []
    u_ref, qd_ref, qm_ref, kd_ref, km_ref, vdt_ref, vmt_ref = (next(it) for _ in range(7))
    if states:
        state_refs = sk_ref, sv_ref, sckv_ref, skpe_ref = next(it), next(it), next(it), next(it)
    tm = x_ref.shape[0]

    def tokens():
        mod = mod_ref[...]
        h = _modulated_norm(x_ref[...], g1_ref[...], mod[0:1], mod[1:2]).astype(BF16)
        proj = jnp.dot(h, w_ref[...], preferred_element_type=F32)

        o = 0
        u_a = proj[:, o:o + CONV_W]; o += CONV_W
        u_g = proj[:, o:o + CONV_W]; o += CONV_W
        dq = proj[:, o:o + DIFF_W]; o += DIFF_W
        dk = proj[:, o:o + DIFF_W]; o += DIFF_W
        dv = proj[:, o:o + DIFF_W]; o += DIFF_W
        cq = proj[:, o:o + MLA_Q_RANK]; o += MLA_Q_RANK
        ckv = proj[:, o:o + MLA_KV_RANK]; o += MLA_KV_RANK
        kpe2 = jnp.dot(h, wkpe_ref[...], preferred_element_type=F32)

        cqn = _rms_full(cq, gcq_ref[...])
        ckvn = _rms_full(ckv, gckv_ref[...])
        q = _group_rms(dq, segd_ref, DIFF_HD, gq_ref[...])
        k = _group_rms(dk, segd_ref, DIFF_HD, gk_ref[...])
        qm = jnp.dot(cqn.astype(BF16), wuq_ref[...], preferred_element_type=F32)
        lane = lax.broadcasted_iota(jnp.int32, (1, LANES), 1)
        km, vm = _mla_keys_values(ckvn, jnp.where(lane >= MLA_NOPE, kpe2, 0.0), wukv_ref)

        u_ref[...] = u_a * jax.nn.sigmoid(u_g)
        vdt_ref[...] = dv.T.astype(BF16)
        if states:
            for dst, src in zip(state_refs, prev_refs):
                dst[0:n_prev] = src[...]
            for hd in range(DIFF_HEADS):
                sl = slice(hd * LANES, (hd + 1) * LANES)
                sk_ref[n_prev, pl.ds(hd, tm, stride=DIFF_HEADS), :] = k[:, sl]
                sv_ref[n_prev, pl.ds(hd, tm, stride=DIFF_HEADS), :] = dv[:, sl]
            sckv_ref[n_prev] = ckvn
            skpe_ref[n_prev] = kpe2[:, :MLA_ROPE]

        qm = _group_rms(qm, segm_ref, MLA_QK, gmq_ref[...])
        km = _group_rms(km, segm_ref, MLA_QK, gmk_ref[...])
        if rope:
            cd = _tile_lanes(cd_ref[...], DIFF_W // LANES)
            sd = _tile_lanes(sd_ref[...], DIFF_W // LANES)
            q = _rope(q, permd_ref, cd, sd)
            k = _rope(k, permd_ref, cd, sd)
        qd_ref[...] = (q * (DIFF_HD ** -0.5 * LOG2E)).astype(BF16)
        kd_ref[...] = k.astype(BF16)
        vmt_ref[...] = vm.T.astype(BF16)
        if rope:
            cm = _tile_lanes(cm_ref[...], MLA_HEADS)
            sm = _tile_lanes(sm_ref[...], MLA_HEADS)
            qm = _rope(qm, permm_ref, cm, sm)
            km = _rope(km, permm_ref, cm, sm)
        qm_ref[...] = (qm * (MLA_QK ** -0.5 * LOG2E)).astype(BF16)
        km_ref[...] = km.astype(BF16)

    def cached_context():
        heads = lambda ref: jnp.concatenate([ref[:, hd, :] for hd in range(DIFF_HEADS)], axis=-1)
        kd_ref[...] = heads(ck_ref).astype(BF16)
        vdt_ref[...] = heads(cv_ref).T.astype(BF16)
        km, vm = _mla_keys_values(cckv_ref[...], ckpe_ref[...], wukv_ref)
        km_ref[...] = _group_rms(km, segm_ref, MLA_QK, gmk_ref[...]).astype(BF16)
        vmt_ref[...] = vm.T.astype(BF16)

    if cache:
        j = pl.program_id(1)
        pl.when(j == 0)(cached_context)
        pl.when(j > 0)(tokens)
    else:
        tokens()


def _layer_spec(w, l, grid_rank, **kw):
    index = (lambda i: (l, 0, 0)) if grid_rank == 1 else (lambda b, j: (l, 0, 0))
    return pl.BlockSpec((None,) + w.shape[1:], index, **kw)


def _inproj(x, mod_l, W, l, rope_tabs, cache, *, batch, row0, states, prev_states=None):
    t = x.shape[0]
    tm = TOK_TILE
    n = t // batch // tm
    rope = rope_tabs is not None
    has_cache = cache is not None
    nk = n + 1 if has_cache else n
    if has_cache:
        grid = (batch, nk)
        tok = lambda b, j: (b * n + jnp.maximum(j - 1, 0), 0)
        key = lambda b, j: (b * nk + j, 0)
        keyt = lambda b, j: (b, 0, j)
        const = lambda b, j: (0, 0)
        modrow = lambda b, j: (row0 + b, 0, 0)
        tab = lambda b, j: (jnp.maximum(j - 1, 0), 0)
    else:
        grid = (batch * n,)
        tok = lambda i: (i, 0)
        key = tok
        keyt = lambda i: (i // n, 0, i % n)
        const = lambda i: (0, 0)
        modrow = lambda i: (row0 + (i // n if row0 else 0), 0, 0)
        tab = lambda i: (i % n, 0)
    names = ("norm1_g", "w1", "wkpe", "gq", "gk", "gcq", "gckv", "wuq", "wukv", "gmq", "gmk")
    in_specs = [pl.BlockSpec((tm, D_MODEL), tok), pl.BlockSpec((None, 6, D_MODEL), modrow)]
    in_specs += [_layer_spec(W[k], l, len(grid)) for k in names]
    in_specs += [pl.BlockSpec((MXU_W, MXU_W), const)] * 2
    seg_d, seg_m, perm_d, perm_m = _group_matrices()
    args = [x, mod_l] + [W[k] for k in names] + [seg_d, seg_m]
    if rope:
        in_specs += [pl.BlockSpec((MXU_W, MXU_W), const)] * 2 + [pl.BlockSpec((tm, LANES), tab)] * 4
        args += [perm_d, perm_m] + list(rope_tabs)
    if has_cache:
        in_specs += [pl.BlockSpec((None, None) + a.shape[2:], lambda b, j, r=a.ndim - 2: (b, l) + (0,) * r)
                     for a in cache]
        args += list(cache)
    sk = nk * tm
    out_shape = [
        jax.ShapeDtypeStruct((t, CONV_W), F32),
        jax.ShapeDtypeStruct((t, DIFF_W), BF16),
        jax.ShapeDtypeStruct((t, MLA_PAD), BF16),
        jax.ShapeDtypeStruct((batch * sk, DIFF_W), BF16),
        jax.ShapeDtypeStruct((batch * sk, MLA_PAD), BF16),
        jax.ShapeDtypeStruct((batch, DIFF_W, sk), BF16),
        jax.ShapeDtypeStruct((batch, MLA_W, sk), BF16),
    ]
    out_specs = [
        pl.BlockSpec((tm, CONV_W), tok),
        pl.BlockSpec((tm, DIFF_W), tok),
        pl.BlockSpec((tm, MLA_PAD), tok),
        pl.BlockSpec((tm, DIFF_W), key),
        pl.BlockSpec((tm, MLA_PAD), key),
        pl.BlockSpec((None, DIFF_W, tm), keyt),
        pl.BlockSpec((None, MLA_W, tm), keyt),
    ]
    n_prev = 0
    if states:
        assert n == 1
        n_prev = l
        st_shapes = ((tm * DIFF_HEADS, LANES), (tm * DIFF_HEADS, LANES), (tm, MLA_KV_RANK), (tm, MLA_ROPE))
        seq_block = lambda layers, s: pl.BlockSpec((None, layers) + s, lambda i: (i, 0, 0, 0))
        out_shape += [jax.ShapeDtypeStruct((batch, l + 1) + s, F32) for s in st_shapes]
        out_specs += [seq_block(l + 1, s) for s in st_shapes]
        if n_prev:
            in_specs += [seq_block(l, s) for s in st_shapes]
            args += list(prev_states)
    return pl.pallas_call(
        functools.partial(_inproj_kernel, rope=rope, states=states, cache=has_cache, n_prev=n_prev),
        grid=grid,
        in_specs=in_specs,
        out_specs=out_specs,
        out_shape=out_shape,
        compiler_params=pltpu.CompilerParams(
            dimension_semantics=("arbitrary",) * len(grid), vmem_limit_bytes=VMEM_LIMIT),
        name="inproj_latent" if has_cache else "inproj_context",
    )(*args)


def _fold_rows(x, op):
    n = x.shape[0]
    while n % (2 * SUBLANES) == 0 and n > 4 * SUBLANES:
        n //= 2
        x = op(x[:n], x[n:])
    return x


KEY_CHUNK = 256


def _attn_kernel(qd_ref, qm_ref, kd_ref, km_ref, vdt_ref, vmt_ref, lam_ref, sub_ref, od_ref, om_ref, s_ref, p_ref,
                 *, lam_init):
    lp = lam_ref[...]
    lam = (jnp.exp(jnp.sum(lp[0:1] * lp[1:2], axis=-1, keepdims=True))
           - jnp.exp(jnp.sum(lp[2:3] * lp[3:4], axis=-1, keepdims=True)) + lam_init)
    lo = lax.broadcasted_iota(jnp.int32, (1, LANES), 1) < DIFF_HD
    top = lax.broadcasted_iota(jnp.int32, (LANES, 1), 0) < MLA_V
    subg = sub_ref[...]
    sk = kd_ref.shape[0]
    kc = min(KEY_CHUNK, sk)
    n_chunks = sk // kc

    jobs = []
    for h in range(DIFF_HEADS):
        sl = slice(h * LANES, (h + 1) * LANES)
        jobs += [(kd_ref, sl, qd_ref, 0, vdt_ref, sl), (kd_ref, sl, qd_ref, 1, vdt_ref, sl)]
    for h in range(MLA_HEADS):
        hs = slice(h * LANES, (h + 1) * LANES)
        pair = slice((h // 2) * LANES, (h // 2 + 1) * LANES)
        jobs.append((km_ref, hs, qm_ref, None, vmt_ref, pair))
    n = len(jobs)

    def query(i):
        _, sl, q_ref, half, _, _ = jobs[i]
        if half is None:
            return q_ref[:, sl]
        q = q_ref[:, sl].astype(F32)
        return (jnp.where(lo, q, 0.0) if half == 0 else jnp.where(lo, 0.0, q)).astype(BF16)

    qs, mcol, mfin, lcol, acc = {}, {}, {}, {}, {}
    outs = [None] * n
    for i in range(-2, n + 1):
        a, b, c = i + 2, i, i - 1
        if 0 <= a < n:
            qs[a] = query(a)
        if 0 <= b < n:
            mfin[b] = jnp.max(mcol.pop(b), axis=0, keepdims=True)
        for r in range(n_chunks):
            rows = slice(r * kc, (r + 1) * kc)
            if 0 <= a < n:
                k_ref, ksl = jobs[a][0], jobs[a][1]
                s = lax.dot_general(k_ref[rows, ksl], qs[a], (((1,), (1,)), ((), ())),
                                    preferred_element_type=F32)
                s_ref[a % 3, rows, :] = s
                f = _fold_rows(s, jnp.maximum)
                mcol[a] = f if r == 0 else jnp.maximum(mcol[a], f)
            if 0 <= b < n:
                p = jnp.exp2(s_ref[b % 3, rows, :] - mfin[b])
                f = _fold_rows(p, jnp.add)
                lcol[b] = f if r == 0 else lcol[b] + f
                p_ref[b % 2, rows, :] = p.astype(BF16)
            if 0 <= c < n:
                vt_ref, vrows = jobs[c][4], jobs[c][5]
                d = jnp.dot(vt_ref[vrows, rows], p_ref[c % 2, rows, :], preferred_element_type=F32)
                acc[c] = d if r == 0 else acc[c] + d
        if 0 <= c < n:
            outs[c] = acc.pop(c) / jnp.sum(lcol.pop(c), axis=0, keepdims=True)

    for h in range(DIFF_HEADS):
        sl = slice(h * LANES, (h + 1) * LANES)
        a = outs[2 * h] - lam * outs[2 * h + 1]
        ms = jnp.mean(a * a, axis=0, keepdims=True)
        od = a * lax.rsqrt(ms + EPS) * subg * (1.0 - lam_init)
        od_ref[:, sl] = od.T.astype(BF16)
    base = 2 * DIFF_HEADS
    for hp in range(MLA_HEADS // 2):
        sl = slice(hp * LANES, (hp + 1) * LANES)
        om_ref[:, sl] = jnp.where(top, outs[base + 2 * hp], outs[base + 2 * hp + 1]).T.astype(BF16)


def _attention(qd, qm, kd, km, vdt, vmt, W, l, *, batch, lam_init):
    sq = qd.shape[0] // batch
    sk = kd.shape[0] // batch
    tq = TOK_TILE
    nq = sq // tq
    qrow = lambda b, i: (b * nq + i, 0)
    krow = lambda b, i: (b, 0)
    kcol = lambda b, i: (b, 0, 0)
    const = lambda b, i: (0, 0)
    return pl.pallas_call(
        functools.partial(_attn_kernel, lam_init=lam_init),
        grid=(batch, nq),
        in_specs=[
            pl.BlockSpec((tq, DIFF_W), qrow),
            pl.BlockSpec((tq, MLA_PAD), qrow),
            pl.BlockSpec((sk, DIFF_W), krow),
            pl.BlockSpec((sk, MLA_PAD), krow),
            pl.BlockSpec((None, DIFF_W, sk), kcol),
            pl.BlockSpec((None, MLA_W, sk), kcol),
            _layer_spec(W["lam"], l, 2),
            _layer_spec(W["subg"], l, 2),
        ],
        out_specs=[pl.BlockSpec((tq, DIFF_W), qrow), pl.BlockSpec((tq, MLA_W), qrow)],
        out_shape=[jax.ShapeDtypeStruct((batch * sq, DIFF_W), BF16),
                   jax.ShapeDtypeStruct((batch * sq, MLA_W), BF16)],
        scratch_shapes=[pltpu.VMEM((3, sk, tq), F32), pltpu.VMEM((2, sk, tq), BF16)],
        compiler_params=pltpu.CompilerParams(
            dimension_semantics=("parallel", "arbitrary"), vmem_limit_bytes=VMEM_LIMIT),
        name="attention",
    )(qd, qm, kd, km, vdt, vmt, W["lam"], W["subg"])


CONV_SPAN = (CONV_K + SUBLANES - 1) // SUBLANES * SUBLANES - SUBLANES


def _depthwise_conv(buf_ref, phase_ref, dw_ref, tm):
    base = HALO - CONV_K // 2
    acc = None
    for r in range(SUBLANES):
        offs = [o for o in range(base, base + CONV_K) if o % SUBLANES == r]
        phase_ref[r] = buf_ref[r:r + tm + CONV_SPAN, :]
        for o in offs:
            term = phase_ref[r, o - r:o - r + tm, :] * dw_ref[o - base:o - base + 1, :]
            acc = term if acc is None else acc + term
    return acc


def _merge_mlp_kernel(*refs, tiles_per_seq):
    it = iter(refs)
    x_ref, mod_ref, g1_ref, g2_ref, u_ref = (next(it) for _ in range(5))
    if tiles_per_seq > 1:
        up_ref, un_ref = next(it), next(it)
    dw_ref, cb_ref, cg_ref, od_ref, om_ref = (next(it) for _ in range(5))
    wg_ref, wc_ref, wd_ref, wm_ref, wo_ref, wup_ref, wdn_ref, y_ref, buf_ref, phase_ref = (next(it) for _ in range(10))
    tm = x_ref.shape[0]

    mod = mod_ref[...]
    x = x_ref[...]
    h = _modulated_norm(x, g1_ref[...], mod[0:1], mod[1:2])
    gates = jax.nn.sigmoid(_dot_nt(h.astype(BF16), wg_ref[...]))
    o_diff = jnp.dot(od_ref[...], wd_ref[...], preferred_element_type=F32)
    o_mla = jnp.dot(om_ref[...], wm_ref[...], preferred_element_type=F32)

    zeros = jnp.zeros((HALO, CONV_W), F32)
    if tiles_per_seq > 1:
        j = pl.program_id(0) % tiles_per_seq
        prev = jnp.where(j == 0, zeros, up_ref[...])
        nxt = jnp.where(j == tiles_per_seq - 1, zeros, un_ref[...])
    else:
        prev, nxt = zeros, zeros
    buf_ref[0:HALO, :] = prev
    buf_ref[HALO:HALO + tm, :] = u_ref[...]
    buf_ref[HALO + tm:HALO + tm + HALO, :] = nxt
    conv = _depthwise_conv(buf_ref, phase_ref, dw_ref, tm) + cb_ref[...]
    c = _silu(_rms_full(conv, cg_ref[...])).astype(BF16)
    o_conv = jnp.dot(c, wc_ref[...], preferred_element_type=F32)
    merged = (gates[:, 0:D_MODEL] * o_conv + gates[:, D_MODEL:2 * D_MODEL] * o_diff
              + gates[:, 2 * D_MODEL:3 * D_MODEL] * o_mla)
    x = x + mod[2:3] * jnp.dot(merged.astype(BF16), wo_ref[...], preferred_element_type=F32)
    h2 = _modulated_norm(x, g2_ref[...], mod[3:4], mod[4:5])
    up = jnp.dot(h2.astype(BF16), wup_ref[...], preferred_element_type=F32)
    act = jnp.square(jnp.maximum(up, 0.0)).astype(BF16)
    y_ref[...] = x + mod[5:6] * jnp.dot(act, wdn_ref[...], preferred_element_type=F32)


def _merge_mlp(x, mod_l, u, od, om, W, l, *, tiles_per_seq, row0):
    t = x.shape[0]
    tm = TOK_TILE
    nb = t // HALO
    per = tm // HALO
    row = lambda i: (i, 0)

    in_specs = [
        pl.BlockSpec((tm, D_MODEL), row),
        pl.BlockSpec((None, 6, D_MODEL), lambda i: (row0 + (i // tiles_per_seq if row0 else 0), 0, 0)),
        _layer_spec(W["norm1_g"], l, 1),
        _layer_spec(W["norm2_g"], l, 1),
        pl.BlockSpec((tm, CONV_W), row),
    ]
    args = [x, mod_l, W["norm1_g"], W["norm2_g"], u]
    if tiles_per_seq > 1:
        in_specs += [
            pl.BlockSpec((HALO, CONV_W), lambda i: (jnp.maximum(i * per - 1, 0), 0)),
            pl.BlockSpec((HALO, CONV_W), lambda i: (jnp.minimum((i + 1) * per, nb - 1), 0)),
        ]
        args += [u, u]
    big = ("wg", "w_conv_out", "w_diff_out", "w_mla_out", "w_out", "w_up", "w_down")
    in_specs += [_layer_spec(W[k], l, 1) for k in ("conv_dw", "conv_b", "conv_g")]
    in_specs += [pl.BlockSpec((tm, DIFF_W), row), pl.BlockSpec((tm, MLA_W), row)]
    in_specs += [_layer_spec(W[k], l, 1, pipeline_mode=pl.Buffered(1)) for k in big]
    args += [W["conv_dw"], W["conv_b"], W["conv_g"], od, om] + [W[k] for k in big]
    return pl.pallas_call(
        functools.partial(_merge_mlp_kernel, tiles_per_seq=tiles_per_seq),
        grid=(t // tm,),
        in_specs=in_specs,
        out_specs=pl.BlockSpec((tm, D_MODEL), row),
        out_shape=jax.ShapeDtypeStruct((t, D_MODEL), F32),
        scratch_shapes=[pltpu.VMEM((tm + 2 * HALO, CONV_W), F32),
                        pltpu.VMEM((SUBLANES, tm + CONV_SPAN, CONV_W), F32)],
        compiler_params=pltpu.CompilerParams(
            dimension_semantics=("parallel",), vmem_limit_bytes=VMEM_LIMIT),
        name="merge_mlp",
    )(*args)


def _rope_tables(seq_len):
    rows = seq_len // GRID_W
    row = np.repeat(np.arange(rows, dtype=np.float64), GRID_W)
    col = np.tile(np.arange(GRID_W, dtype=np.float64), rows)

    def tables(rot_dim):
        half = rot_dim // 2
        inv = ROPE_BASE ** (-np.arange(0, half, 2, dtype=np.float64) / half)
        a0 = row[:, None] * inv
        a1 = col[:, None] * inv
        cos = np.concatenate([np.cos(a0), np.cos(a0), np.cos(a1), np.cos(a1)], axis=1)
        sin = np.concatenate([-np.sin(a0), np.sin(a0), -np.sin(a1), np.sin(a1)], axis=1)
        return cos, sin

    cd, sd = tables(DIFF_HD)
    cd, sd = np.tile(cd, (1, LANES // DIFF_HD)), np.tile(sd, (1, LANES // DIFF_HD))
    cm, sm = tables(MLA_ROPE)
    pad = LANES - MLA_QK
    cm = np.concatenate([np.ones((seq_len, MLA_NOPE)), cm, np.ones((seq_len, pad))], axis=1)
    sm = np.concatenate([np.zeros((seq_len, MLA_NOPE)), sm, np.zeros((seq_len, pad))], axis=1)
    return tuple(jnp.asarray(t, dtype=F32) for t in (cd, sd, cm, sm))


def _group_matrices():
    lane = np.arange(MXU_W)
    seg_d = (lane[:, None] // DIFF_HD == lane[None, :] // DIFF_HD)
    seg_m = (lane[:, None] // LANES == lane[None, :] // LANES)

    def swap(rotary, half):
        src = np.where(lane % (2 * half) < half, lane + half, lane - half)
        return (lane[:, None] == src[None, :]) & rotary[None, :]

    perm_d = swap(np.ones(MXU_W, bool), DIFF_HD // 4)
    in_head = lane % LANES
    perm_m = swap((in_head >= MLA_NOPE) & (in_head < MLA_QK), MLA_ROPE // 4)
    return tuple(jnp.asarray(m, dtype=BF16) for m in (seg_d, seg_m, perm_d, perm_m))


def _pad_heads(v):
    return jnp.pad(v, [(0, 0)] * (v.ndim - 1) + [(0, LANES - MLA_QK)])


def _prepare_weights(norm1_g, w_in, conv_dw, conv_b, conv_norm_g, w_conv_out, diff_q_norm, diff_k_norm,
                     diff_lambda, diff_subln, w_diff_out, mla_q_a_norm, mla_kv_a_norm, w_uq, w_ukv,
                     mla_q_norm, mla_k_norm, w_mla_out, w_out, norm2_g, w_up, w_down):
    row = lambda v: v[:, None, :]
    kpe_cols = w_in[:, :, N_MAIN:N_MAIN + MLA_ROPE].astype(BF16)
    zc = jnp.zeros_like(kpe_cols)
    wg_t = jnp.swapaxes(w_in, 1, 2)[:, N_MAIN + MLA_ROPE:, :].astype(BF16)
    wuq = _pad_heads(w_uq.reshape(DEPTH, MLA_Q_RANK, MLA_HEADS, MLA_QK)).reshape(DEPTH, MLA_Q_RANK, MLA_PAD)
    wukv4 = w_ukv.reshape(DEPTH, MLA_KV_RANK, MLA_HEADS, MLA_NOPE + MLA_V)
    wkn = jnp.pad(wukv4[..., :MLA_NOPE], ((0, 0), (0, 0), (0, 0), (0, LANES - MLA_NOPE)))
    wkn = wkn.reshape(DEPTH, MLA_KV_RANK, MLA_PAD)
    wv = wukv4[..., MLA_NOPE:].reshape(DEPTH, MLA_KV_RANK, MLA_W)
    return dict(
        norm1_g=row(norm1_g), norm2_g=row(norm2_g),
        w1=w_in[:, :, :N_MAIN].astype(BF16),
        wkpe=jnp.concatenate([kpe_cols, zc, kpe_cols, zc], axis=-1),
        wg=wg_t,
        gq=row(jnp.tile(diff_q_norm, (1, DIFF_W // DIFF_HD))),
        gk=row(jnp.tile(diff_k_norm, (1, DIFF_W // DIFF_HD))),
        gcq=row(mla_q_a_norm), gckv=row(mla_kv_a_norm),
        wuq=wuq.astype(BF16), wukv=jnp.concatenate([wkn, wv], axis=-1).astype(BF16),
        gmq=row(jnp.tile(_pad_heads(mla_q_norm), (1, MLA_HEADS))),
        gmk=row(jnp.tile(_pad_heads(mla_k_norm), (1, MLA_HEADS))),
        conv_dw=jnp.pad(conv_dw, ((0, 0), (0, 1), (0, 0))), conv_b=row(conv_b), conv_g=row(conv_norm_g),
        lam=diff_lambda, subg=diff_subln[:, :, None],
        w_conv_out=w_conv_out.astype(BF16), w_diff_out=w_diff_out.astype(BF16),
        w_mla_out=w_mla_out.astype(BF16), w_out=w_out.astype(BF16),
        w_up=w_up.astype(BF16), w_down=w_down.astype(BF16),
    )


def kernel(x_prompt, x_sample, cache_diff_k, cache_diff_v, cache_mla_ckv, cache_mla_kpe, c, c_ctx, mod_w, mod_b, norm1_g, w_in, conv_dw, conv_b, conv_norm_g, w_conv_out, diff_q_norm, diff_k_norm, diff_lambda, diff_subln, w_diff_out, mla_q_a_norm, mla_kv_a_norm, w_uq, w_ukv, mla_q_norm, mla_k_norm, w_mla_out, w_out, norm2_g, w_up, w_down):
    batch, seq, _ = x_prompt.shape
    dec_batch, dec_seq, _ = x_sample.shape
    past = cache_diff_k.shape[2]
    assert seq == TOK_TILE and past == TOK_TILE and dec_seq % TOK_TILE == 0 and 1 + dec_batch <= 8

    cond_rows = jnp.concatenate([c_ctx[None], c, jnp.zeros((8 - 1 - dec_batch, D_MODEL), F32)], axis=0)
    mod = _modulation(cond_rows, mod_w, mod_b).reshape(DEPTH, 8, 6, D_MODEL)
    tabs = _rope_tables(dec_seq)

    W = _prepare_weights(norm1_g, w_in, conv_dw, conv_b, conv_norm_g, w_conv_out, diff_q_norm, diff_k_norm,
                         diff_lambda, diff_subln, w_diff_out, mla_q_a_norm, mla_kv_a_norm, w_uq, w_ukv,
                         mla_q_norm, mla_k_norm, w_mla_out, w_out, norm2_g, w_up, w_down)
    cache = (cache_diff_k, cache_diff_v, cache_mla_ckv,
             jnp.pad(cache_mla_kpe, ((0, 0), (0, 0), (0, 0), (MLA_NOPE, LANES - MLA_QK))))

    xp = x_prompt.reshape(batch * seq, D_MODEL)
    xs = x_sample.reshape(dec_batch * dec_seq, D_MODEL)
    new_state = None
    for l in range(DEPTH):
        lam_init = 0.8 - 0.6 * math.exp(-0.3 * l)

        u, qd, qm, kd, km, vdt, vmt, *new_state = _inproj(
            xp, mod[l], W, l, None, None, batch=batch, row0=0, states=True, prev_states=new_state)
        od, om = _attention(qd, qm, kd, km, vdt, vmt, W, l, batch=batch, lam_init=lam_init)
        xp = _merge_mlp(xp, mod[l], u, od, om, W, l, tiles_per_seq=1, row0=0)

        u, qd, qm, kd, km, vdt, vmt = _inproj(xs, mod[l], W, l, tabs, cache, batch=dec_batch, row0=1, states=False)
        od, om = _attention(qd, qm, kd, km, vdt, vmt, W, l, batch=dec_batch, lam_init=lam_init)
        xs = _merge_mlp(xs, mod[l], u, od, om, W, l, tiles_per_seq=dec_seq // TOK_TILE, row0=1)

    sk, sv, sckv, skpe = new_state
    return (xp.reshape(batch, seq, D_MODEL), xs.reshape(dec_batch, dec_seq, D_MODEL),
            sk.reshape(batch, DEPTH, seq, DIFF_HEADS, 2 * DIFF_HD),
            sv.reshape(batch, DEPTH, seq, DIFF_HEADS, 2 * DIFF_HD), sckv, skpe)
```

```python
import functools
import math

import numpy as np
import jax
import jax.numpy as jnp
from jax import lax
from jax.experimental import pallas as pl
from jax.experimental.pallas import tpu as pltpu

D_MODEL = 1024
DEPTH = 2
GRID_W = 64
ROPE_BASE = 10000.0
CONV_W = 512
CONV_K = 31
DIFF_HEADS = 4
DIFF_HD = 64
DIFF_W = DIFF_HEADS * 2 * DIFF_HD
MLA_HEADS = 8
MLA_NOPE = 64
MLA_ROPE = 32
MLA_QK = MLA_NOPE + MLA_ROPE
MLA_V = 64
MLA_Q_RANK = 384
MLA_KV_RANK = 256
MLA_W = MLA_HEADS * MLA_V
D_FF = 4 * D_MODEL
EPS = 1e-6

LANES = 128
SUBLANES = 8
MXU_W = 256
MLA_PAD = MLA_HEADS * LANES
TOK_TILE = 256
HALO = 16
N_MAIN = 2 * CONV_W + 3 * DIFF_W + MLA_Q_RANK + MLA_KV_RANK
VMEM_LIMIT = 56 * 1024 * 1024
LOG2E = math.log2(math.e)
F32 = jnp.float32
BF16 = jnp.bfloat16


def _silu(x):
    return x * jax.nn.sigmoid(x)


def _rms_full(x, g):
    ms = jnp.mean(x * x, axis=-1, keepdims=True)
    return x * lax.rsqrt(ms + EPS) * g


def _lane_tiles(x):
    return jnp.concatenate([x[:, t:t + MXU_W] for t in range(0, x.shape[-1], MXU_W)], axis=0)


def _from_lane_tiles(y, rows):
    return jnp.concatenate([y[r:r + rows] for r in range(0, y.shape[0], rows)], axis=-1)


def _group_rms(x, seg_ref, n, g):
    rows = x.shape[0]
    xt = _lane_tiles(x)
    ss = jnp.dot((xt * xt).astype(BF16), seg_ref[...], preferred_element_type=F32)
    return _from_lane_tiles(xt * lax.rsqrt(ss * (1.0 / n) + EPS), rows) * g


def _rope(x, perm_ref, cos, sin_signed):
    rows = x.shape[0]
    sw = jnp.dot(_lane_tiles(x).astype(BF16), perm_ref[...], preferred_element_type=F32)
    return x * cos + _from_lane_tiles(sw, rows) * sin_signed


def _tile_lanes(t, n):
    return jnp.concatenate([t] * n, axis=-1)


def _modulated_norm(x, g, shift, scale):
    return _rms_full(x, g) * (1.0 + scale) + shift


def _mod_kernel(s_ref, w_ref, b_ref, o_ref):
    s = _silu(s_ref[...]).astype(BF16)
    o_ref[...] = jnp.dot(s, w_ref[...].astype(BF16), preferred_element_type=F32) + b_ref[...]


def _modulation(cond_rows, mod_w, mod_b):
    tn = 1536
    n = 6 * D_MODEL
    return pl.pallas_call(
        _mod_kernel,
        grid=(DEPTH, n // tn),
        in_specs=[
            pl.BlockSpec((8, D_MODEL), lambda l, j: (0, 0)),
            pl.BlockSpec((None, D_MODEL, tn), lambda l, j: (l, 0, j)),
            pl.BlockSpec((None, 1, tn), lambda l, j: (l, 0, j)),
        ],
        out_specs=pl.BlockSpec((None, 8, tn), lambda l, j: (l, 0, j)),
        out_shape=jax.ShapeDtypeStruct((DEPTH, 8, n), F32),
        compiler_params=pltpu.CompilerParams(
            dimension_semantics=("parallel", "parallel"), vmem_limit_bytes=VMEM_LIMIT),
        name="modulation",
    )(cond_rows, mod_w, mod_b.reshape(DEPTH, 1, n))


def _mla_keys_values(ckvn, kpe_hi, wukv_ref):
    kv = jnp.dot(ckvn.astype(BF16), wukv_ref[...], preferred_element_type=F32)
    return kv[:, :MLA_PAD] + _tile_lanes(kpe_hi, MLA_HEADS), kv[:, MLA_PAD:]


def _inproj_kernel(*refs, rope, states, cache, n_prev):
    it = iter(refs)
    x_ref, mod_ref, g1_ref, w_ref, wkpe_ref = next(it), next(it), next(it), next(it), next(it)
    gq_ref, gk_ref, gcq_ref, gckv_ref = next(it), next(it), next(it), next(it)
    wuq_ref, wukv_ref, gmq_ref, gmk_ref = next(it), next(it), next(it), next(it)
    segd_ref, segm_ref = next(it), next(it)
    if rope:
        permd_ref, permm_ref = next(it), next(it)
        cd_ref, sd_ref, cm_ref, sm_ref = next(it), next(it), next(it), next(it)
    if cache:
        ck_ref, cv_ref, cckv_ref, ckpe_ref = next(it), next(it), next(it), next(it)
    prev_refs = [next(it) for _ in range(4)] if n_prev else []
    u_ref, qd_ref, qm_ref, kd_ref, km_ref, vdt_ref, vmt_ref = (next(it) for _ in range(7))
    if states:
        state_refs = sk_ref, sv_ref, sckv_ref, skpe_ref = next(it), next(it), next(it), next(it)
    w16_ref = next(it)
    tm = x_ref.shape[0]

    first = functools.reduce(jnp.logical_and, [pl.program_id(a) == 0 for a in range(2 if cache else 1)])

    @pl.when(first)
    def _():
        w16_ref[...] = w_ref[...].astype(BF16)

    def tokens():
        mod = mod_ref[...]
        h = _modulated_norm(x_ref[...], g1_ref[...], mod[0:1], mod[1:2]).astype(BF16)
        proj = jnp.dot(h, w16_ref[...], preferred_element_type=F32)

        o = 0
        u_a = proj[:, o:o + CONV_W]; o += CONV_W
        u_g = proj[:, o:o + CONV_W]; o += CONV_W
        dq = proj[:, o:o + DIFF_W]; o += DIFF_W
        dk = proj[:, o:o + DIFF_W]; o += DIFF_W
        dv = proj[:, o:o + DIFF_W]; o += DIFF_W
        cq = proj[:, o:o + MLA_Q_RANK]; o += MLA_Q_RANK
        ckv = proj[:, o:o + MLA_KV_RANK]; o += MLA_KV_RANK
        kpe2 = jnp.dot(h, wkpe_ref[...], preferred_element_type=F32)

        cqn = _rms_full(cq, gcq_ref[...])
        ckvn = _rms_full(ckv, gckv_ref[...])
        q = _group_rms(dq, segd_ref, DIFF_HD, gq_ref[...])
        k = _group_rms(dk, segd_ref, DIFF_HD, gk_ref[...])
        qm = jnp.dot(cqn.astype(BF16), wuq_ref[...], preferred_element_type=F32)
        lane = lax.broadcasted_iota(jnp.int32, (1, LANES), 1)
        km, vm = _mla_keys_values(ckvn, jnp.where(lane >= MLA_NOPE, kpe2, 0.0), wukv_ref)

        u_ref[...] = u_a * jax.nn.sigmoid(u_g)
        vdt_ref[...] = dv.T.astype(BF16)
        if states:
            for dst, src in zip(state_refs, prev_refs):
                dst[0:n_prev] = src[...]
            for hd in range(DIFF_HEADS):
                sl = slice(hd * LANES, (hd + 1) * LANES)
                sk_ref[n_prev, pl.ds(hd, tm, stride=DIFF_HEADS), :] = k[:, sl]
                sv_ref[n_prev, pl.ds(hd, tm, stride=DIFF_HEADS), :] = dv[:, sl]
            sckv_ref[n_prev] = ckvn
            skpe_ref[n_prev] = kpe2[:, :MLA_ROPE]

        qm = _group_rms(qm, segm_ref, MLA_QK, gmq_ref[...])
        km = _group_rms(km, segm_ref, MLA_QK, gmk_ref[...])
        if rope:
            cd = _tile_lanes(cd_ref[...], DIFF_W // LANES)
            sd = _tile_lanes(sd_ref[...], DIFF_W // LANES)
            q = _rope(q, permd_ref, cd, sd)
            k = _rope(k, permd_ref, cd, sd)
        qd_ref[...] = (q * (DIFF_HD ** -0.5 * LOG2E)).astype(BF16)
        kd_ref[...] = k.astype(BF16)
        vmt_ref[...] = vm.T.astype(BF16)
        if rope:
            cm = _tile_lanes(cm_ref[...], MLA_HEADS)
            sm = _tile_lanes(sm_ref[...], MLA_HEADS)
            qm = _rope(qm, permm_ref, cm, sm)
            km = _rope(km, permm_ref, cm, sm)
        qm_ref[...] = (qm * (MLA_QK ** -0.5 * LOG2E)).astype(BF16)
        km_ref[...] = km.astype(BF16)

    def cached_context():
        heads = lambda ref: jnp.concatenate([ref[:, hd, :] for hd in range(DIFF_HEADS)], axis=-1)
        kd_ref[...] = heads(ck_ref).astype(BF16)
        vdt_ref[...] = heads(cv_ref).T.astype(BF16)
        km, vm = _mla_keys_values(cckv_ref[...], ckpe_ref[...], wukv_ref)
        km_ref[...] = _group_rms(km, segm_ref, MLA_QK, gmk_ref[...]).astype(BF16)
        vmt_ref[...] = vm.T.astype(BF16)

    if cache:
        j = pl.program_id(1)
        pl.when(j == 0)(cached_context)
        pl.when(j > 0)(tokens)
    else:
        tokens()


def _layer_spec(w, l, grid_rank, **kw):
    index = (lambda i: (l, 0, 0)) if grid_rank == 1 else (lambda b, j: (l, 0, 0))
    return pl.BlockSpec((None,) + w.shape[1:], index, **kw)


def _inproj(x, mod_l, W, l, rope_tabs, cache, *, batch, row0, states, prev_states=None):
    t = x.shape[0]
    tm = TOK_TILE
    n = t // batch // tm
    rope = rope_tabs is not None
    has_cache = cache is not None
    nk = n + 1 if has_cache else n
    if has_cache:
        grid = (batch, nk)
        tok = lambda b, j: (b * n + jnp.maximum(j - 1, 0), 0)
        key = lambda b, j: (b * nk + j, 0)
        keyt = lambda b, j: (b, 0, j)
        const = lambda b, j: (0, 0)
        modrow = lambda b, j: (row0 + b, 0, 0)
        tab = lambda b, j: (jnp.maximum(j - 1, 0), 0)
    else:
        grid = (batch * n,)
        tok = lambda i: (i, 0)
        key = tok
        keyt = lambda i: (i // n, 0, i % n)
        const = lambda i: (0, 0)
        modrow = lambda i: (row0 + (i // n if row0 else 0), 0, 0)
        tab = lambda i: (i % n, 0)
    names = ("norm1_g", "w_in", "wkpe", "gq", "gk", "gcq", "gckv", "wuq", "wukv", "gmq", "gmk")
    in_specs = [pl.BlockSpec((tm, D_MODEL), tok), pl.BlockSpec((None, 6, D_MODEL), modrow)]
    in_specs += [_layer_spec(W[k], l, len(grid)) for k in names]
    in_specs[2 + names.index("w_in")] = pl.BlockSpec((None, D_MODEL, N_MAIN), in_specs[2].index_map,
                                                     pipeline_mode=pl.Buffered(1))
    in_specs += [pl.BlockSpec((MXU_W, MXU_W), const)] * 2
    seg_d, seg_m, perm_d, perm_m = _group_matrices()
    args = [x, mod_l] + [W[k] for k in names] + [seg_d, seg_m]
    if rope:
        in_specs += [pl.BlockSpec((MXU_W, MXU_W), const)] * 2 + [pl.BlockSpec((tm, LANES), tab)] * 4
        args += [perm_d, perm_m] + list(rope_tabs)
    if has_cache:
        in_specs += [pl.BlockSpec((None, None) + a.shape[2:], lambda b, j, r=a.ndim - 2: (b, l) + (0,) * r)
                     for a in cache]
        args += list(cache)
    sk = nk * tm
    out_shape = [
        jax.ShapeDtypeStruct((t, CONV_W), F32),
        jax.ShapeDtypeStruct((t, DIFF_W), BF16),
        jax.ShapeDtypeStruct((t, MLA_PAD), BF16),
        jax.ShapeDtypeStruct((batch * sk, DIFF_W), BF16),
        jax.ShapeDtypeStruct((batch * sk, MLA_PAD), BF16),
        jax.ShapeDtypeStruct((batch, DIFF_W, sk), BF16),
        jax.ShapeDtypeStruct((batch, MLA_W, sk), BF16),
    ]
    out_specs = [
        pl.BlockSpec((tm, CONV_W), tok),
        pl.BlockSpec((tm, DIFF_W), tok),
        pl.BlockSpec((tm, MLA_PAD), tok),
        pl.BlockSpec((tm, DIFF_W), key),
        pl.BlockSpec((tm, MLA_PAD), key),
        pl.BlockSpec((None, DIFF_W, tm), keyt),
        pl.BlockSpec((None, MLA_W, tm), keyt),
    ]
    n_prev = 0
    if states:
        assert n == 1
        n_prev = l
        st_shapes = ((tm * DIFF_HEADS, LANES), (tm * DIFF_HEADS, LANES), (tm, MLA_KV_RANK), (tm, MLA_ROPE))
        seq_block = lambda layers, s: pl.BlockSpec((None, layers) + s, lambda i: (i, 0, 0, 0))
        out_shape += [jax.ShapeDtypeStruct((batch, l + 1) + s, F32) for s in st_shapes]
        out_specs += [seq_block(l + 1, s) for s in st_shapes]
        if n_prev:
            in_specs += [seq_block(l, s) for s in st_shapes]
            args += list(prev_states)
    return pl.pallas_call(
        functools.partial(_inproj_kernel, rope=rope, states=states, cache=has_cache, n_prev=n_prev),
        grid=grid,
        in_specs=in_specs,
        out_specs=out_specs,
        out_shape=out_shape,
        scratch_shapes=[pltpu.VMEM((D_MODEL, N_MAIN), BF16)],
        compiler_params=pltpu.CompilerParams(
            dimension_semantics=("arbitrary",) * len(grid), vmem_limit_bytes=VMEM_LIMIT),
        name="inproj_latent" if has_cache else "inproj_context",
    )(*args)


def _fold_rows(x, op):
    n = x.shape[0]
    while n % (2 * SUBLANES) == 0 and n > 4 * SUBLANES:
        n //= 2
        x = op(x[:n], x[n:])
    return x


KEY_CHUNK = 256


def _attn_kernel(qd_ref, qm_ref, kd_ref, km_ref, vdt_ref, vmt_ref, lam_ref, sub_ref, od_ref, om_ref, s_ref, p_ref,
                 *, lam_init):
    lp = lam_ref[...]
    lam = (jnp.exp(jnp.sum(lp[0:1] * lp[1:2], axis=-1, keepdims=True))
           - jnp.exp(jnp.sum(lp[2:3] * lp[3:4], axis=-1, keepdims=True)) + lam_init)
    lo = lax.broadcasted_iota(jnp.int32, (1, LANES), 1) < DIFF_HD
    top = lax.broadcasted_iota(jnp.int32, (LANES, 1), 0) < MLA_V
    subg = sub_ref[...]
    sk = kd_ref.shape[0]
    kc = min(KEY_CHUNK, sk)
    n_chunks = sk // kc

    jobs = []
    for h in range(DIFF_HEADS):
        sl = slice(h * LANES, (h + 1) * LANES)
        jobs += [(kd_ref, sl, qd_ref, 0, vdt_ref, sl), (kd_ref, sl, qd_ref, 1, vdt_ref, sl)]
    for h in range(MLA_HEADS):
        hs = slice(h * LANES, (h + 1) * LANES)
        pair = slice((h // 2) * LANES, (h // 2 + 1) * LANES)
        jobs.append((km_ref, hs, qm_ref, None, vmt_ref, pair))
    n = len(jobs)

    def query(i):
        _, sl, q_ref, half, _, _ = jobs[i]
        if half is None:
            return q_ref[:, sl]
        q = q_ref[:, sl].astype(F32)
        return (jnp.where(lo, q, 0.0) if half == 0 else jnp.where(lo, 0.0, q)).astype(BF16)

    qs, mcol, mfin, lcol, acc = {}, {}, {}, {}, {}
    outs = [None] * n
    for i in range(-2, n + 1):
        a, b, c = i + 2, i, i - 1
        if 0 <= a < n:
            qs[a] = query(a)
        if 0 <= b < n:
            mfin[b] = jnp.max(mcol.pop(b), axis=0, keepdims=True)
        for r in range(n_chunks):
            rows = slice(r * kc, (r + 1) * kc)
            if 0 <= a < n:
                k_ref, ksl = jobs[a][0], jobs[a][1]
                s = lax.dot_general(k_ref[rows, ksl], qs[a], (((1,), (1,)), ((), ())),
                                    preferred_element_type=F32)
                s_ref[a % 3, rows, :] = s
                f = _fold_rows(s, jnp.maximum)
                mcol[a] = f if r == 0 else jnp.maximum(mcol[a], f)
            if 0 <= b < n:
                p = jnp.exp2(s_ref[b % 3, rows, :] - mfin[b])
                f = _fold_rows(p, jnp.add)
                lcol[b] = f if r == 0 else lcol[b] + f
                p_ref[b % 2, rows, :] = p.astype(BF16)
            if 0 <= c < n:
                vt_ref, vrows = jobs[c][4], jobs[c][5]
                d = jnp.dot(vt_ref[vrows, rows], p_ref[c % 2, rows, :], preferred_element_type=F32)
                acc[c] = d if r == 0 else acc[c] + d
        if 0 <= c < n:
            outs[c] = acc.pop(c) / jnp.sum(lcol.pop(c), axis=0, keepdims=True)

    for h in range(DIFF_HEADS):
        sl = slice(h * LANES, (h + 1) * LANES)
        a = outs[2 * h] - lam * outs[2 * h + 1]
        ms = jnp.mean(a * a, axis=0, keepdims=True)
        od = a * lax.rsqrt(ms + EPS) * subg * (1.0 - lam_init)
        od_ref[:, sl] = od.T.astype(BF16)
    base = 2 * DIFF_HEADS
    for hp in range(MLA_HEADS // 2):
        sl = slice(hp * LANES, (hp + 1) * LANES)
        om_ref[:, sl] = jnp.where(top, outs[base + 2 * hp], outs[base + 2 * hp + 1]).T.astype(BF16)


def _attention(qd, qm, kd, km, vdt, vmt, W, l, *, batch, lam_init):
    sq = qd.shape[0] // batch
    sk = kd.shape[0] // batch
    tq = TOK_TILE
    nq = sq // tq
    qrow = lambda b, i: (b * nq + i, 0)
    krow = lambda b, i: (b, 0)
    kcol = lambda b, i: (b, 0, 0)
    const = lambda b, i: (0, 0)
    return pl.pallas_call(
        functools.partial(_attn_kernel, lam_init=lam_init),
        grid=(batch, nq),
        in_specs=[
            pl.BlockSpec((tq, DIFF_W), qrow),
            pl.BlockSpec((tq, MLA_PAD), qrow),
            pl.BlockSpec((sk, DIFF_W), krow),
            pl.BlockSpec((sk, MLA_PAD), krow),
            pl.BlockSpec((None, DIFF_W, sk), kcol),
            pl.BlockSpec((None, MLA_W, sk), kcol),
            _layer_spec(W["lam"], l, 2),
            _layer_spec(W["subg"], l, 2),
        ],
        out_specs=[pl.BlockSpec((tq, DIFF_W), qrow), pl.BlockSpec((tq, MLA_W), qrow)],
        out_shape=[jax.ShapeDtypeStruct((batch * sq, DIFF_W), BF16),
                   jax.ShapeDtypeStruct((batch * sq, MLA_W), BF16)],
        scratch_shapes=[pltpu.VMEM((3, sk, tq), F32), pltpu.VMEM((2, sk, tq), BF16)],
        compiler_params=pltpu.CompilerParams(
            dimension_semantics=("parallel", "arbitrary"), vmem_limit_bytes=VMEM_LIMIT),
        name="attention",
    )(qd, qm, kd, km, vdt, vmt, W["lam"], W["subg"])


CONV_SPAN = (CONV_K + SUBLANES - 1) // SUBLANES * SUBLANES - SUBLANES


def _depthwise_conv(buf_ref, phase_ref, dw_ref, tm):
    base = HALO - CONV_K // 2
    acc = None
    for r in range(SUBLANES):
        offs = [o for o in range(base, base + CONV_K) if o % SUBLANES == r]
        phase_ref[r] = buf_ref[r:r + tm + CONV_SPAN, :]
        for o in offs:
            term = phase_ref[r, o - r:o - r + tm, :] * dw_ref[o - base:o - base + 1, :]
            acc = term if acc is None else acc + term
    return acc


def _merge_mlp_kernel(*refs, tiles_per_seq):
    it = iter(refs)
    x_ref, mod_ref, g1_ref, g2_ref, u_ref = (next(it) for _ in range(5))
    if tiles_per_seq > 1:
        up_ref, un_ref = next(it), next(it)
    dw_ref, cb_ref, cg_ref, od_ref, om_ref = (next(it) for _ in range(5))
    wg_ref, wc_ref, wd_ref, wm_ref, wo_ref, wup_ref, wdn_ref, y_ref, buf_ref, phase_ref = (next(it) for _ in range(10))
    tm = x_ref.shape[0]

    mod = mod_ref[...]
    x = x_ref[...]
    h = _modulated_norm(x, g1_ref[...], mod[0:1], mod[1:2])
    gates = jax.nn.sigmoid(jnp.dot(h.astype(BF16), wg_ref[...], preferred_element_type=F32))
    o_diff = jnp.dot(od_ref[...], wd_ref[...], preferred_element_type=F32)
    o_mla = jnp.dot(om_ref[...], wm_ref[...], preferred_element_type=F32)

    zeros = jnp.zeros((HALO, CONV_W), F32)
    if tiles_per_seq > 1:
        j = pl.program_id(0) % tiles_per_seq
        prev = jnp.where(j == 0, zeros, up_ref[...])
        nxt = jnp.where(j == tiles_per_seq - 1, zeros, un_ref[...])
    else:
        prev, nxt = zeros, zeros
    buf_ref[0:HALO, :] = prev
    buf_ref[HALO:HALO + tm, :] = u_ref[...]
    buf_ref[HALO + tm:HALO + tm + HALO, :] = nxt
    conv = _depthwise_conv(buf_ref, phase_ref, dw_ref, tm) + cb_ref[...]
    c = _silu(_rms_full(conv, cg_ref[...])).astype(BF16)
    o_conv = jnp.dot(c, wc_ref[...], preferred_element_type=F32)
    merged = (gates[:, 0:D_MODEL] * o_conv + gates[:, D_MODEL:2 * D_MODEL] * o_diff
              + gates[:, 2 * D_MODEL:3 * D_MODEL] * o_mla)
    x = x + mod[2:3] * jnp.dot(merged.astype(BF16), wo_ref[...], preferred_element_type=F32)
    h2 = _modulated_norm(x, g2_ref[...], mod[3:4], mod[4:5])
    up = jnp.dot(h2.astype(BF16), wup_ref[...], preferred_element_type=F32)
    act = jnp.square(jnp.maximum(up, 0.0)).astype(BF16)
    y_ref[...] = x + mod[5:6] * jnp.dot(act, wdn_ref[...], preferred_element_type=F32)


def _merge_mlp(x, mod_l, u, od, om, W, l, *, tiles_per_seq, row0):
    t = x.shape[0]
    tm = TOK_TILE
    nb = t // HALO
    per = tm // HALO
    row = lambda i: (i, 0)

    in_specs = [
        pl.BlockSpec((tm, D_MODEL), row),
        pl.BlockSpec((None, 6, D_MODEL), lambda i: (row0 + (i // tiles_per_seq if row0 else 0), 0, 0)),
        _layer_spec(W["norm1_g"], l, 1),
        _layer_spec(W["norm2_g"], l, 1),
        pl.BlockSpec((tm, CONV_W), row),
    ]
    args = [x, mod_l, W["norm1_g"], W["norm2_g"], u]
    if tiles_per_seq > 1:
        in_specs += [
            pl.BlockSpec((HALO, CONV_W), lambda i: (jnp.maximum(i * per - 1, 0), 0)),
            pl.BlockSpec((HALO, CONV_W), lambda i: (jnp.minimum((i + 1) * per, nb - 1), 0)),
        ]
        args += [u, u]
    big = ("wg", "w_conv_out", "w_diff_out", "w_mla_out", "w_out", "w_up", "w_down")
    in_specs += [_layer_spec(W[k], l, 1) for k in ("conv_dw", "conv_b", "conv_g")]
    in_specs += [pl.BlockSpec((tm, DIFF_W), row), pl.BlockSpec((tm, MLA_W), row)]
    in_specs += [_layer_spec(W[k], l, 1, pipeline_mode=pl.Buffered(1)) for k in big]
    args += [W["conv_dw"], W["conv_b"], W["conv_g"], od, om] + [W[k] for k in big]
    return pl.pallas_call(
        functools.partial(_merge_mlp_kernel, tiles_per_seq=tiles_per_seq),
        grid=(t // tm,),
        in_specs=in_specs,
        out_specs=pl.BlockSpec((tm, D_MODEL), row),
        out_shape=jax.ShapeDtypeStruct((t, D_MODEL), F32),
        scratch_shapes=[pltpu.VMEM((tm + 2 * HALO, CONV_W), F32),
                        pltpu.VMEM((SUBLANES, tm + CONV_SPAN, CONV_W), F32)],
        compiler_params=pltpu.CompilerParams(
            dimension_semantics=("parallel",), vmem_limit_bytes=VMEM_LIMIT),
        name="merge_mlp",
    )(*args)


def _rope_tables(seq_len):
    rows = seq_len // GRID_W
    row = np.repeat(np.arange(rows, dtype=np.float64), GRID_W)
    col = np.tile(np.arange(GRID_W, dtype=np.float64), rows)

    def tables(rot_dim):
        half = rot_dim // 2
        inv = ROPE_BASE ** (-np.arange(0, half, 2, dtype=np.float64) / half)
        a0 = row[:, None] * inv
        a1 = col[:, None] * inv
        cos = np.concatenate([np.cos(a0), np.cos(a0), np.cos(a1), np.cos(a1)], axis=1)
        sin = np.concatenate([-np.sin(a0), np.sin(a0), -np.sin(a1), np.sin(a1)], axis=1)
        return cos, sin

    cd, sd = tables(DIFF_HD)
    cd, sd = np.tile(cd, (1, LANES // DIFF_HD)), np.tile(sd, (1, LANES // DIFF_HD))
    cm, sm = tables(MLA_ROPE)
    pad = LANES - MLA_QK
    cm = np.concatenate([np.ones((seq_len, MLA_NOPE)), cm, np.ones((seq_len, pad))], axis=1)
    sm = np.concatenate([np.zeros((seq_len, MLA_NOPE)), sm, np.zeros((seq_len, pad))], axis=1)
    return tuple(jnp.asarray(t, dtype=F32) for t in (cd, sd, cm, sm))


def _group_matrices():
    lane = np.arange(MXU_W)
    seg_d = (lane[:, None] // DIFF_HD == lane[None, :] // DIFF_HD)
    seg_m = (lane[:, None] // LANES == lane[None, :] // LANES)

    def swap(rotary, half):
        src = np.where(lane % (2 * half) < half, lane + half, lane - half)
        return (lane[:, None] == src[None, :]) & rotary[None, :]

    perm_d = swap(np.ones(MXU_W, bool), DIFF_HD // 4)
    in_head = lane % LANES
    perm_m = swap((in_head >= MLA_NOPE) & (in_head < MLA_QK), MLA_ROPE // 4)
    return tuple(jnp.asarray(m, dtype=BF16) for m in (seg_d, seg_m, perm_d, perm_m))


def _pad_heads(v):
    return jnp.pad(v, [(0, 0)] * (v.ndim - 1) + [(0, LANES - MLA_QK)])


def _prepare_weights(norm1_g, w_in, conv_dw, conv_b, conv_norm_g, w_conv_out, diff_q_norm, diff_k_norm,
                     diff_lambda, diff_subln, w_diff_out, mla_q_a_norm, mla_kv_a_norm, w_uq, w_ukv,
                     mla_q_norm, mla_k_norm, w_mla_out, w_out, norm2_g, w_up, w_down):
    row = lambda v: v[:, None, :]
    kpe_cols = w_in[:, :, N_MAIN:N_MAIN + MLA_ROPE].astype(BF16)
    zc = jnp.zeros_like(kpe_cols)
    wuq = _pad_heads(w_uq.reshape(DEPTH, MLA_Q_RANK, MLA_HEADS, MLA_QK)).reshape(DEPTH, MLA_Q_RANK, MLA_PAD)
    wukv4 = w_ukv.reshape(DEPTH, MLA_KV_RANK, MLA_HEADS, MLA_NOPE + MLA_V)
    wkn = jnp.pad(wukv4[..., :MLA_NOPE], ((0, 0), (0, 0), (0, 0), (0, LANES - MLA_NOPE)))
    wkn = wkn.reshape(DEPTH, MLA_KV_RANK, MLA_PAD)
    wv = wukv4[..., MLA_NOPE:].reshape(DEPTH, MLA_KV_RANK, MLA_W)
    return dict(
        norm1_g=row(norm1_g), norm2_g=row(norm2_g),
        w_in=w_in,
        wkpe=jnp.concatenate([kpe_cols, zc, kpe_cols, zc], axis=-1),
        wg=w_in[:, :, N_MAIN + MLA_ROPE:].astype(BF16),
        gq=row(jnp.tile(diff_q_norm, (1, DIFF_W // DIFF_HD))),
        gk=row(jnp.tile(diff_k_norm, (1, DIFF_W // DIFF_HD))),
        gcq=row(mla_q_a_norm), gckv=row(mla_kv_a_norm),
        wuq=wuq.astype(BF16), wukv=jnp.concatenate([wkn, wv], axis=-1).astype(BF16),
        gmq=row(jnp.tile(_pad_heads(mla_q_norm), (1, MLA_HEADS))),
        gmk=row(jnp.tile(_pad_heads(mla_k_norm), (1, MLA_HEADS))),
        conv_dw=jnp.pad(conv_dw, ((0, 0), (0, 1), (0, 0))), conv_b=row(conv_b), conv_g=row(conv_norm_g),
        lam=diff_lambda, subg=diff_subln[:, :, None],
        w_conv_out=w_conv_out.astype(BF16), w_diff_out=w_diff_out.astype(BF16),
        w_mla_out=w_mla_out.astype(BF16), w_out=w_out.astype(BF16),
        w_up=w_up.astype(BF16), w_down=w_down.astype(BF16),
    )


def kernel(x_prompt, x_sample, cache_diff_k, cache_diff_v, cache_mla_ckv, cache_mla_kpe, c, c_ctx, mod_w, mod_b, norm1_g, w_in, conv_dw, conv_b, conv_norm_g, w_conv_out, diff_q_norm, diff_k_norm, diff_lambda, diff_subln, w_diff_out, mla_q_a_norm, mla_kv_a_norm, w_uq, w_ukv, mla_q_norm, mla_k_norm, w_mla_out, w_out, norm2_g, w_up, w_down):
    batch, seq, _ = x_prompt.shape
    dec_batch, dec_seq, _ = x_sample.shape
    past = cache_diff_k.shape[2]
    assert seq == TOK_TILE and past == TOK_TILE and dec_seq % TOK_TILE == 0 and 1 + dec_batch <= 8

    cond_rows = jnp.concatenate([c_ctx[None], c, jnp.zeros((8 - 1 - dec_batch, D_MODEL), F32)], axis=0)
    mod = _modulation(cond_rows, mod_w, mod_b).reshape(DEPTH, 8, 6, D_MODEL)
    tabs = _rope_tables(dec_seq)

    W = _prepare_weights(norm1_g, w_in, conv_dw, conv_b, conv_norm_g, w_conv_out, diff_q_norm, diff_k_norm,
                         diff_lambda, diff_subln, w_diff_out, mla_q_a_norm, mla_kv_a_norm, w_uq, w_ukv,
                         mla_q_norm, mla_k_norm, w_mla_out, w_out, norm2_g, w_up, w_down)
    cache = (cache_diff_k, cache_diff_v, cache_mla_ckv,
             jnp.pad(cache_mla_kpe, ((0, 0), (0, 0), (0, 0), (MLA_NOPE, LANES - MLA_QK))))

    xp = x_prompt.reshape(batch * seq, D_MODEL)
    xs = x_sample.reshape(dec_batch * dec_seq, D_MODEL)
    new_state = None
    for l in range(DEPTH):
        lam_init = 0.8 - 0.6 * math.exp(-0.3 * l)

        u, qd, qm, kd, km, vdt, vmt, *new_state = _inproj(
            xp, mod[l], W, l, None, None, batch=batch, row0=0, states=True, prev_states=new_state)
        od, om = _attention(qd, qm, kd, km, vdt, vmt, W, l, batch=batch, lam_init=lam_init)
        xp = _merge_mlp(xp, mod[l], u, od, om, W, l, tiles_per_seq=1, row0=0)

        u, qd, qm, kd, km, vdt, vmt = _inproj(xs, mod[l], W, l, tabs, cache, batch=dec_batch, row0=1, states=False)
        od, om = _attention(qd, qm, kd, km, vdt, vmt, W, l, batch=dec_batch, lam_init=lam_init)
        xs = _merge_mlp(xs, mod[l], u, od, om, W, l, tiles_per_seq=dec_seq // TOK_TILE, row0=1)

    sk, sv, sckv, skpe = new_state
    return (xp.reshape(batch, seq, D_MODEL), xs.reshape(dec_batch, dec_seq, D_MODEL),
            sk.reshape(batch, DEPTH, seq, DIFF_HEADS, 2 * DIFF_HD),
            sv.reshape(batch, DEPTH, seq, DIFF_HEADS, 2 * DIFF_HD), sckv, skpe)
```

```python
import functools
import math

import numpy as np
import jax
import jax.numpy as jnp
from jax import lax
from jax.experimental import pallas as pl
from jax.experimental.pallas import tpu as pltpu

D_MODEL = 1024
DEPTH = 2
GRID_W = 64
ROPE_BASE = 10000.0
CONV_W = 512
CONV_K = 31
DIFF_HEADS = 4
DIFF_HD = 64
DIFF_W = DIFF_HEADS * 2 * DIFF_HD
MLA_HEADS = 8
MLA_NOPE = 64
MLA_ROPE = 32
MLA_QK = MLA_NOPE + MLA_ROPE
MLA_V = 64
MLA_Q_RANK = 384
MLA_KV_RANK = 256
MLA_W = MLA_HEADS * MLA_V
D_FF = 4 * D_MODEL
EPS = 1e-6

LANES = 128
SUBLANES = 8
MXU_W = 256
MLA_PAD = MLA_HEADS * LANES
TOK_TILE = 256
HALO = 16
N_MAIN = 2 * CONV_W + 3 * DIFF_W + MLA_Q_RANK + MLA_KV_RANK
VMEM_LIMIT = 56 * 1024 * 1024
LOG2E = math.log2(math.e)
F32 = jnp.float32
BF16 = jnp.bfloat16


def _silu(x):
    return x * jax.nn.sigmoid(x)


def _rms_full(x, g):
    ms = jnp.mean(x * x, axis=-1, keepdims=True)
    return x * lax.rsqrt(ms + EPS) * g


def _lane_tiles(x):
    return jnp.concatenate([x[:, t:t + MXU_W] for t in range(0, x.shape[-1], MXU_W)], axis=0)


def _from_lane_tiles(y, rows):
    return jnp.concatenate([y[r:r + rows] for r in range(0, y.shape[0], rows)], axis=-1)


def _group_rms(x, seg_ref, n, g):
    rows = x.shape[0]
    xt = _lane_tiles(x)
    ss = jnp.dot((xt * xt).astype(BF16), seg_ref[...], preferred_element_type=F32)
    return _from_lane_tiles(xt * lax.rsqrt(ss * (1.0 / n) + EPS), rows) * g


def _rope(x, perm_ref, cos, sin_signed):
    rows = x.shape[0]
    sw = jnp.dot(_lane_tiles(x).astype(BF16), perm_ref[...], preferred_element_type=F32)
    return x * cos + _from_lane_tiles(sw, rows) * sin_signed


def _tile_lanes(t, n):
    return jnp.concatenate([t] * n, axis=-1)


def _modulated_norm(x, g, shift, scale):
    return _rms_full(x, g) * (1.0 + scale) + shift


def _mod_kernel(s_ref, w_ref, b_ref, o_ref):
    s = _silu(s_ref[...]).astype(BF16)
    o_ref[...] = jnp.dot(s, w_ref[...].astype(BF16), preferred_element_type=F32) + b_ref[...]


def _modulation(cond_rows, mod_w, mod_b):
    tn = 1536
    n = 6 * D_MODEL
    return pl.pallas_call(
        _mod_kernel,
        grid=(DEPTH, n // tn),
        in_specs=[
            pl.BlockSpec((8, D_MODEL), lambda l, j: (0, 0)),
            pl.BlockSpec((None, D_MODEL, tn), lambda l, j: (l, 0, j)),
            pl.BlockSpec((None, 1, tn), lambda l, j: (l, 0, j)),
        ],
        out_specs=pl.BlockSpec((None, 8, tn), lambda l, j: (l, 0, j)),
        out_shape=jax.ShapeDtypeStruct((DEPTH, 8, n), F32),
        compiler_params=pltpu.CompilerParams(
            dimension_semantics=("parallel", "parallel"), vmem_limit_bytes=VMEM_LIMIT),
        name="modulation",
    )(cond_rows, mod_w, mod_b.reshape(DEPTH, 1, n))


def _mla_keys_values(ckvn, kpe_hi, wukv_ref):
    kv = jnp.dot(ckvn.astype(BF16), wukv_ref[...], preferred_element_type=F32)
    return kv[:, :MLA_PAD] + _tile_lanes(kpe_hi, MLA_HEADS), kv[:, MLA_PAD:]


def _inproj_kernel(*refs, rope, states, cache, n_prev):
    it = iter(refs)
    x_ref, mod_ref, g1_ref, w_ref, wkpe_ref = next(it), next(it), next(it), next(it), next(it)
    gq_ref, gk_ref, gcq_ref, gckv_ref = next(it), next(it), next(it), next(it)
    wuq_ref, wukv_ref, gmq_ref, gmk_ref = next(it), next(it), next(it), next(it)
    segd_ref, segm_ref = next(it), next(it)
    if rope:
        permd_ref, permm_ref = next(it), next(it)
        cd_ref, sd_ref, cm_ref, sm_ref = next(it), next(it), next(it), next(it)
    if cache:
        ck_ref, cv_ref, cckv_ref, ckpe_ref = next(it), next(it), next(it), next(it)
    prev_refs = [next(it) for _ in range(4)] if n_prev else []
    u_ref, qd_ref, qm_ref, kd_ref, km_ref, vdt_ref, vmt_ref = (next(it) for _ in range(7))
    if states:
        state_refs = sk_ref, sv_ref, sckv_ref, skpe_ref = next(it), next(it), next(it), next(it)
    w16_ref = next(it)
    tm = x_ref.shape[0]

    first = functools.reduce(jnp.logical_and, [pl.program_id(a) == 0 for a in range(2 if cache else 1)])

    @pl.when(first)
    def _():
        for c0 in range(0, N_MAIN, LANES):
            w16_ref[:, c0:c0 + LANES] = w_ref[c0:c0 + LANES, :].T.astype(BF16)

    def tokens():
        mod = mod_ref[...]
        h = _modulated_norm(x_ref[...], g1_ref[...], mod[0:1], mod[1:2]).astype(BF16)
        proj = jnp.dot(h, w16_ref[...], preferred_element_type=F32)

        o = 0
        u_a = proj[:, o:o + CONV_W]; o += CONV_W
        u_g = proj[:, o:o + CONV_W]; o += CONV_W
        dq = proj[:, o:o + DIFF_W]; o += DIFF_W
        dk = proj[:, o:o + DIFF_W]; o += DIFF_W
        dv = proj[:, o:o + DIFF_W]; o += DIFF_W
        cq = proj[:, o:o + MLA_Q_RANK]; o += MLA_Q_RANK
        ckv = proj[:, o:o + MLA_KV_RANK]; o += MLA_KV_RANK
        kpe2 = jnp.dot(h, wkpe_ref[...], preferred_element_type=F32)

        cqn = _rms_full(cq, gcq_ref[...])
        ckvn = _rms_full(ckv, gckv_ref[...])
        q = _group_rms(dq, segd_ref, DIFF_HD, gq_ref[...])
        k = _group_rms(dk, segd_ref, DIFF_HD, gk_ref[...])
        qm = jnp.dot(cqn.astype(BF16), wuq_ref[...], preferred_element_type=F32)
        lane = lax.broadcasted_iota(jnp.int32, (1, LANES), 1)
        km, vm = _mla_keys_values(ckvn, jnp.where(lane >= MLA_NOPE, kpe2, 0.0), wukv_ref)

        u_ref[...] = u_a * jax.nn.sigmoid(u_g)
        vdt_ref[...] = dv.T.astype(BF16)
        if states:
            for dst, src in zip(state_refs, prev_refs):
                dst[0:n_prev] = src[...]
            for hd in range(DIFF_HEADS):
                sl = slice(hd * LANES, (hd + 1) * LANES)
                sk_ref[n_prev, pl.ds(hd, tm, stride=DIFF_HEADS), :] = k[:, sl]
                sv_ref[n_prev, pl.ds(hd, tm, stride=DIFF_HEADS), :] = dv[:, sl]
            sckv_ref[n_prev] = ckvn
            skpe_ref[n_prev] = kpe2[:, :MLA_ROPE]

        qm = _group_rms(qm, segm_ref, MLA_QK, gmq_ref[...])
        km = _group_rms(km, segm_ref, MLA_QK, gmk_ref[...])
        if rope:
            cd = _tile_lanes(cd_ref[...], DIFF_W // LANES)
            sd = _tile_lanes(sd_ref[...], DIFF_W // LANES)
            q = _rope(q, permd_ref, cd, sd)
            k = _rope(k, permd_ref, cd, sd)
        qd_ref[...] = (q * (DIFF_HD ** -0.5 * LOG2E)).astype(BF16)
        kd_ref[...] = k.astype(BF16)
        vmt_ref[...] = vm.T.astype(BF16)
        if rope:
            cm = _tile_lanes(cm_ref[...], MLA_HEADS)
            sm = _tile_lanes(sm_ref[...], MLA_HEADS)
            qm = _rope(qm, permm_ref, cm, sm)
            km = _rope(km, permm_ref, cm, sm)
        qm_ref[...] = (qm * (MLA_QK ** -0.5 * LOG2E)).astype(BF16)
        km_ref[...] = km.astype(BF16)

    def cached_context():
        heads = lambda ref: jnp.concatenate([ref[:, hd, :] for hd in range(DIFF_HEADS)], axis=-1)
        kd_ref[...] = heads(ck_ref).astype(BF16)
        vdt_ref[...] = heads(cv_ref).T.astype(BF16)
        km, vm = _mla_keys_values(cckv_ref[...], ckpe_ref[...], wukv_ref)
        km_ref[...] = _group_rms(km, segm_ref, MLA_QK, gmk_ref[...]).astype(BF16)
        vmt_ref[...] = vm.T.astype(BF16)

    if cache:
        j = pl.program_id(1)
        pl.when(j == 0)(cached_context)
        pl.when(j > 0)(tokens)
    else:
        tokens()


def _layer_spec(w, l, grid_rank, **kw):
    index = (lambda i: (l, 0, 0)) if grid_rank == 1 else (lambda b, j: (l, 0, 0))
    return pl.BlockSpec((None,) + w.shape[1:], index, **kw)


def _inproj(x, mod_l, W, l, rope_tabs, cache, *, batch, row0, states, prev_states=None):
    t = x.shape[0]
    tm = TOK_TILE
    n = t // batch // tm
    rope = rope_tabs is not None
    has_cache = cache is not None
    nk = n + 1 if has_cache else n
    if has_cache:
        grid = (batch, nk)
        tok = lambda b, j: (b * n + jnp.maximum(j - 1, 0), 0)
        key = lambda b, j: (b * nk + j, 0)
        keyt = lambda b, j: (b, 0, j)
        const = lambda b, j: (0, 0)
        modrow = lambda b, j: (row0 + b, 0, 0)
        tab = lambda b, j: (jnp.maximum(j - 1, 0), 0)
    else:
        grid = (batch * n,)
        tok = lambda i: (i, 0)
        key = tok
        keyt = lambda i: (i // n, 0, i % n)
        const = lambda i: (0, 0)
        modrow = lambda i: (row0 + (i // n if row0 else 0), 0, 0)
        tab = lambda i: (i % n, 0)
    names = ("norm1_g", "w_in", "wkpe", "gq", "gk", "gcq", "gckv", "wuq", "wukv", "gmq", "gmk")
    in_specs = [pl.BlockSpec((tm, D_MODEL), tok), pl.BlockSpec((None, 6, D_MODEL), modrow)]
    in_specs += [_layer_spec(W[k], l, len(grid)) for k in names]
    in_specs[2 + names.index("w_in")] = pl.BlockSpec((None, N_MAIN, D_MODEL), in_specs[2].index_map,
                                                     pipeline_mode=pl.Buffered(1))
    in_specs += [pl.BlockSpec((MXU_W, MXU_W), const)] * 2
    seg_d, seg_m, perm_d, perm_m = _group_matrices()
    args = [x, mod_l] + [W[k] for k in names] + [seg_d, seg_m]
    if rope:
        in_specs += [pl.BlockSpec((MXU_W, MXU_W), const)] * 2 + [pl.BlockSpec((tm, LANES), tab)] * 4
        args += [perm_d, perm_m] + list(rope_tabs)
    if has_cache:
        in_specs += [pl.BlockSpec((None, None) + a.shape[2:], lambda b, j, r=a.ndim - 2: (b, l) + (0,) * r)
                     for a in cache]
        args += list(cache)
    sk = nk * tm
    out_shape = [
        jax.ShapeDtypeStruct((t, CONV_W), F32),
        jax.ShapeDtypeStruct((t, DIFF_W), BF16),
        jax.ShapeDtypeStruct((t, MLA_PAD), BF16),
        jax.ShapeDtypeStruct((batch * sk, DIFF_W), BF16),
        jax.ShapeDtypeStruct((batch * sk, MLA_PAD), BF16),
        jax.ShapeDtypeStruct((batch, DIFF_W, sk), BF16),
        jax.ShapeDtypeStruct((batch, MLA_W, sk), BF16),
    ]
    out_specs = [
        pl.BlockSpec((tm, CONV_W), tok),
        pl.BlockSpec((tm, DIFF_W), tok),
        pl.BlockSpec((tm, MLA_PAD), tok),
        pl.BlockSpec((tm, DIFF_W), key),
        pl.BlockSpec((tm, MLA_PAD), key),
        pl.BlockSpec((None, DIFF_W, tm), keyt),
        pl.BlockSpec((None, MLA_W, tm), keyt),
    ]
    n_prev = 0
    if states:
        assert n == 1
        n_prev = l
        st_shapes = ((tm * DIFF_HEADS, LANES), (tm * DIFF_HEADS, LANES), (tm, MLA_KV_RANK), (tm, MLA_ROPE))
        seq_block = lambda layers, s: pl.BlockSpec((None, layers) + s, lambda i: (i, 0, 0, 0))
        out_shape += [jax.ShapeDtypeStruct((batch, l + 1) + s, F32) for s in st_shapes]
        out_specs += [seq_block(l + 1, s) for s in st_shapes]
        if n_prev:
            in_specs += [seq_block(l, s) for s in st_shapes]
            args += list(prev_states)
    return pl.pallas_call(
        functools.partial(_inproj_kernel, rope=rope, states=states, cache=has_cache, n_prev=n_prev),
        grid=grid,
        in_specs=in_specs,
        out_specs=out_specs,
        out_shape=out_shape,
        scratch_shapes=[pltpu.VMEM((D_MODEL, N_MAIN), BF16)],
        compiler_params=pltpu.CompilerParams(
            dimension_semantics=("arbitrary",) * len(grid), vmem_limit_bytes=VMEM_LIMIT),
        name="inproj_latent" if has_cache else "inproj_context",
    )(*args)


def _fold_rows(x, op):
    n = x.shape[0]
    while n % (2 * SUBLANES) == 0 and n > 4 * SUBLANES:
        n //= 2
        x = op(x[:n], x[n:])
    return x


KEY_CHUNK = 256


def _attn_kernel(qd_ref, qm_ref, kd_ref, km_ref, vdt_ref, vmt_ref, lam_ref, sub_ref, od_ref, om_ref, s_ref, p_ref,
                 *, lam_init):
    lp = lam_ref[...]
    lam = (jnp.exp(jnp.sum(lp[0:1] * lp[1:2], axis=-1, keepdims=True))
           - jnp.exp(jnp.sum(lp[2:3] * lp[3:4], axis=-1, keepdims=True)) + lam_init)
    lo = lax.broadcasted_iota(jnp.int32, (1, LANES), 1) < DIFF_HD
    top = lax.broadcasted_iota(jnp.int32, (LANES, 1), 0) < MLA_V
    subg = sub_ref[...]
    sk = kd_ref.shape[0]
    kc = min(KEY_CHUNK, sk)
    n_chunks = sk // kc

    jobs = []
    for h in range(DIFF_HEADS):
        sl = slice(h * LANES, (h + 1) * LANES)
        jobs += [(kd_ref, sl, qd_ref, 0, vdt_ref, sl), (kd_ref, sl, qd_ref, 1, vdt_ref, sl)]
    for h in range(MLA_HEADS):
        hs = slice(h * LANES, (h + 1) * LANES)
        pair = slice((h // 2) * LANES, (h // 2 + 1) * LANES)
        jobs.append((km_ref, hs, qm_ref, None, vmt_ref, pair))
    n = len(jobs)

    def query(i):
        _, sl, q_ref, half, _, _ = jobs[i]
        if half is None:
            return q_ref[:, sl]
        q = q_ref[:, sl].astype(F32)
        return (jnp.where(lo, q, 0.0) if half == 0 else jnp.where(lo, 0.0, q)).astype(BF16)

    qs, mcol, mfin, lcol, acc = {}, {}, {}, {}, {}
    outs = [None] * n
    for i in range(-2, n + 1):
        a, b, c = i + 2, i, i - 1
        if 0 <= a < n:
            qs[a] = query(a)
        if 0 <= b < n:
            mfin[b] = jnp.max(mcol.pop(b), axis=0, keepdims=True)
        for r in range(n_chunks):
            rows = slice(r * kc, (r + 1) * kc)
            if 0 <= a < n:
                k_ref, ksl = jobs[a][0], jobs[a][1]
                s = lax.dot_general(k_ref[rows, ksl], qs[a], (((1,), (1,)), ((), ())),
                                    preferred_element_type=F32)
                s_ref[a % 3, rows, :] = s
                f = _fold_rows(s, jnp.maximum)
                mcol[a] = f if r == 0 else jnp.maximum(mcol[a], f)
            if 0 <= b < n:
                p = jnp.exp2(s_ref[b % 3, rows, :] - mfin[b])
                f = _fold_rows(p, jnp.add)
                lcol[b] = f if r == 0 else lcol[b] + f
                p_ref[b % 2, rows, :] = p.astype(BF16)
            if 0 <= c < n:
                vt_ref, vrows = jobs[c][4], jobs[c][5]
                d = jnp.dot(vt_ref[vrows, rows], p_ref[c % 2, rows, :], preferred_element_type=F32)
                acc[c] = d if r == 0 else acc[c] + d
        if 0 <= c < n:
            outs[c] = acc.pop(c) / jnp.sum(lcol.pop(c), axis=0, keepdims=True)

    for h in range(DIFF_HEADS):
        sl = slice(h * LANES, (h + 1) * LANES)
        a = outs[2 * h] - lam * outs[2 * h + 1]
        ms = jnp.mean(a * a, axis=0, keepdims=True)
        od = a * lax.rsqrt(ms + EPS) * subg * (1.0 - lam_init)
        od_ref[:, sl] = od.T.astype(BF16)
    base = 2 * DIFF_HEADS
    for hp in range(MLA_HEADS // 2):
        sl = slice(hp * LANES, (hp + 1) * LANES)
        om_ref[:, sl] = jnp.where(top, outs[base + 2 * hp], outs[base + 2 * hp + 1]).T.astype(BF16)


def _attention(qd, qm, kd, km, vdt, vmt, W, l, *, batch, lam_init):
    sq = qd.shape[0] // batch
    sk = kd.shape[0] // batch
    tq = TOK_TILE
    nq = sq // tq
    qrow = lambda b, i: (b * nq + i, 0)
    krow = lambda b, i: (b, 0)
    kcol = lambda b, i: (b, 0, 0)
    const = lambda b, i: (0, 0)
    return pl.pallas_call(
        functools.partial(_attn_kernel, lam_init=lam_init),
        grid=(batch, nq),
        in_specs=[
            pl.BlockSpec((tq, DIFF_W), qrow),
            pl.BlockSpec((tq, MLA_PAD), qrow),
            pl.BlockSpec((sk, DIFF_W), krow),
            pl.BlockSpec((sk, MLA_PAD), krow),
            pl.BlockSpec((None, DIFF_W, sk), kcol),
            pl.BlockSpec((None, MLA_W, sk), kcol),
            _layer_spec(W["lam"], l, 2),
            _layer_spec(W["subg"], l, 2),
        ],
        out_specs=[pl.BlockSpec((tq, DIFF_W), qrow), pl.BlockSpec((tq, MLA_W), qrow)],
        out_shape=[jax.ShapeDtypeStruct((batch * sq, DIFF_W), BF16),
                   jax.ShapeDtypeStruct((batch * sq, MLA_W), BF16)],
        scratch_shapes=[pltpu.VMEM((3, sk, tq), F32), pltpu.VMEM((2, sk, tq), BF16)],
        compiler_params=pltpu.CompilerParams(
            dimension_semantics=("parallel", "arbitrary"), vmem_limit_bytes=VMEM_LIMIT),
        name="attention",
    )(qd, qm, kd, km, vdt, vmt, W["lam"], W["subg"])


CONV_SPAN = (CONV_K + SUBLANES - 1) // SUBLANES * SUBLANES - SUBLANES


def _depthwise_conv(buf_ref, phase_ref, dw_ref, tm):
    base = HALO - CONV_K // 2
    acc = None
    for r in range(SUBLANES):
        offs = [o for o in range(base, base + CONV_K) if o % SUBLANES == r]
        phase_ref[r] = buf_ref[r:r + tm + CONV_SPAN, :]
        for o in offs:
            term = phase_ref[r, o - r:o - r + tm, :] * dw_ref[o - base:o - base + 1, :]
            acc = term if acc is None else acc + term
    return acc


def _merge_mlp_kernel(*refs, tiles_per_seq):
    it = iter(refs)
    x_ref, mod_ref, g1_ref, g2_ref, u_ref = (next(it) for _ in range(5))
    if tiles_per_seq > 1:
        up_ref, un_ref = next(it), next(it)
    dw_ref, cb_ref, cg_ref, od_ref, om_ref = (next(it) for _ in range(5))
    wg_ref, wc_ref, wd_ref, wm_ref, wo_ref, wup_ref, wdn_ref, y_ref, buf_ref, phase_ref = (next(it) for _ in range(10))
    tm = x_ref.shape[0]

    mod = mod_ref[...]
    x = x_ref[...]
    h = _modulated_norm(x, g1_ref[...], mod[0:1], mod[1:2])
    gates = jax.nn.sigmoid(lax.dot_general(h.astype(BF16), wg_ref[...], (((1,), (1,)), ((), ())),
                                           preferred_element_type=F32))
    o_diff = jnp.dot(od_ref[...], wd_ref[...], preferred_element_type=F32)
    o_mla = jnp.dot(om_ref[...], wm_ref[...], preferred_element_type=F32)

    zeros = jnp.zeros((HALO, CONV_W), F32)
    if tiles_per_seq > 1:
        j = pl.program_id(0) % tiles_per_seq
        prev = jnp.where(j == 0, zeros, up_ref[...])
        nxt = jnp.where(j == tiles_per_seq - 1, zeros, un_ref[...])
    else:
        prev, nxt = zeros, zeros
    buf_ref[0:HALO, :] = prev
    buf_ref[HALO:HALO + tm, :] = u_ref[...]
    buf_ref[HALO + tm:HALO + tm + HALO, :] = nxt
    conv = _depthwise_conv(buf_ref, phase_ref, dw_ref, tm) + cb_ref[...]
    c = _silu(_rms_full(conv, cg_ref[...])).astype(BF16)
    o_conv = jnp.dot(c, wc_ref[...], preferred_element_type=F32)
    merged = (gates[:, 0:D_MODEL] * o_conv + gates[:, D_MODEL:2 * D_MODEL] * o_diff
              + gates[:, 2 * D_MODEL:3 * D_MODEL] * o_mla)
    x = x + mod[2:3] * jnp.dot(merged.astype(BF16), wo_ref[...], preferred_element_type=F32)
    h2 = _modulated_norm(x, g2_ref[...], mod[3:4], mod[4:5])
    up = jnp.dot(h2.astype(BF16), wup_ref[...], preferred_element_type=F32)
    act = jnp.square(jnp.maximum(up, 0.0)).astype(BF16)
    y_ref[...] = x + mod[5:6] * jnp.dot(act, wdn_ref[...], preferred_element_type=F32)


def _merge_mlp(x, mod_l, u, od, om, W, l, *, tiles_per_seq, row0):
    t = x.shape[0]
    tm = TOK_TILE
    nb = t // HALO
    per = tm // HALO
    row = lambda i: (i, 0)

    in_specs = [
        pl.BlockSpec((tm, D_MODEL), row),
        pl.BlockSpec((None, 6, D_MODEL), lambda i: (row0 + (i // tiles_per_seq if row0 else 0), 0, 0)),
        _layer_spec(W["norm1_g"], l, 1),
        _layer_spec(W["norm2_g"], l, 1),
        pl.BlockSpec((tm, CONV_W), row),
    ]
    args = [x, mod_l, W["norm1_g"], W["norm2_g"], u]
    if tiles_per_seq > 1:
        in_specs += [
            pl.BlockSpec((HALO, CONV_W), lambda i: (jnp.maximum(i * per - 1, 0), 0)),
            pl.BlockSpec((HALO, CONV_W), lambda i: (jnp.minimum((i + 1) * per, nb - 1), 0)),
        ]
        args += [u, u]
    big = ("wg", "w_conv_out", "w_diff_out", "w_mla_out", "w_out", "w_up", "w_down")
    in_specs += [_layer_spec(W[k], l, 1) for k in ("conv_dw", "conv_b", "conv_g")]
    in_specs += [pl.BlockSpec((tm, DIFF_W), row), pl.BlockSpec((tm, MLA_W), row)]
    in_specs += [_layer_spec(W[k], l, 1, pipeline_mode=pl.Buffered(1)) for k in big]
    args += [W["conv_dw"], W["conv_b"], W["conv_g"], od, om] + [W[k] for k in big]
    return pl.pallas_call(
        functools.partial(_merge_mlp_kernel, tiles_per_seq=tiles_per_seq),
        grid=(t // tm,),
        in_specs=in_specs,
        out_specs=pl.BlockSpec((tm, D_MODEL), row),
        out_shape=jax.ShapeDtypeStruct((t, D_MODEL), F32),
        scratch_shapes=[pltpu.VMEM((tm + 2 * HALO, CONV_W), F32),
                        pltpu.VMEM((SUBLANES, tm + CONV_SPAN, CONV_W), F32)],
        compiler_params=pltpu.CompilerParams(
            dimension_semantics=("parallel",), vmem_limit_bytes=VMEM_LIMIT),
        name="merge_mlp",
    )(*args)


def _rope_tables(seq_len):
    rows = seq_len // GRID_W
    row = np.repeat(np.arange(rows, dtype=np.float64), GRID_W)
    col = np.tile(np.arange(GRID_W, dtype=np.float64), rows)

    def tables(rot_dim):
        half = rot_dim // 2
        inv = ROPE_BASE ** (-np.arange(0, half, 2, dtype=np.float64) / half)
        a0 = row[:, None] * inv
        a1 = col[:, None] * inv
        cos = np.concatenate([np.cos(a0), np.cos(a0), np.cos(a1), np.cos(a1)], axis=1)
        sin = np.concatenate([-np.sin(a0), np.sin(a0), -np.sin(a1), np.sin(a1)], axis=1)
        return cos, sin

    cd, sd = tables(DIFF_HD)
    cd, sd = np.tile(cd, (1, LANES // DIFF_HD)), np.tile(sd, (1, LANES // DIFF_HD))
    cm, sm = tables(MLA_ROPE)
    pad = LANES - MLA_QK
    cm = np.concatenate([np.ones((seq_len, MLA_NOPE)), cm, np.ones((seq_len, pad))], axis=1)
    sm = np.concatenate([np.zeros((seq_len, MLA_NOPE)), sm, np.zeros((seq_len, pad))], axis=1)
    return tuple(jnp.asarray(t, dtype=F32) for t in (cd, sd, cm, sm))


def _group_matrices():
    lane = np.arange(MXU_W)
    seg_d = (lane[:, None] // DIFF_HD == lane[None, :] // DIFF_HD)
    seg_m = (lane[:, None] // LANES == lane[None, :] // LANES)

    def swap(rotary, half):
        src = np.where(lane % (2 * half) < half, lane + half, lane - half)
        return (lane[:, None] == src[None, :]) & rotary[None, :]

    perm_d = swap(np.ones(MXU_W, bool), DIFF_HD // 4)
    in_head = lane % LANES
    perm_m = swap((in_head >= MLA_NOPE) & (in_head < MLA_QK), MLA_ROPE // 4)
    return tuple(jnp.asarray(m, dtype=BF16) for m in (seg_d, seg_m, perm_d, perm_m))


def _pad_heads(v):
    return jnp.pad(v, [(0, 0)] * (v.ndim - 1) + [(0, LANES - MLA_QK)])


def _prepare_weights(norm1_g, w_in, conv_dw, conv_b, conv_norm_g, w_conv_out, diff_q_norm, diff_k_norm,
                     diff_lambda, diff_subln, w_diff_out, mla_q_a_norm, mla_kv_a_norm, w_uq, w_ukv,
                     mla_q_norm, mla_k_norm, w_mla_out, w_out, norm2_g, w_up, w_down):
    row = lambda v: v[:, None, :]
    kpe_cols = w_in[:, :, N_MAIN:N_MAIN + MLA_ROPE].astype(BF16)
    zc = jnp.zeros_like(kpe_cols)
    w_in_t = jnp.swapaxes(w_in, 1, 2)
    wuq = _pad_heads(w_uq.reshape(DEPTH, MLA_Q_RANK, MLA_HEADS, MLA_QK)).reshape(DEPTH, MLA_Q_RANK, MLA_PAD)
    wukv4 = w_ukv.reshape(DEPTH, MLA_KV_RANK, MLA_HEADS, MLA_NOPE + MLA_V)
    wkn = jnp.pad(wukv4[..., :MLA_NOPE], ((0, 0), (0, 0), (0, 0), (0, LANES - MLA_NOPE)))
    wkn = wkn.reshape(DEPTH, MLA_KV_RANK, MLA_PAD)
    wv = wukv4[..., MLA_NOPE:].reshape(DEPTH, MLA_KV_RANK, MLA_W)
    return dict(
        norm1_g=row(norm1_g), norm2_g=row(norm2_g),
        w_in=w_in_t,
        wkpe=jnp.concatenate([kpe_cols, zc, kpe_cols, zc], axis=-1),
        wg=w_in_t[:, N_MAIN + MLA_ROPE:, :].astype(BF16),
        gq=row(jnp.tile(diff_q_norm, (1, DIFF_W // DIFF_HD))),
        gk=row(jnp.tile(diff_k_norm, (1, DIFF_W // DIFF_HD))),
        gcq=row(mla_q_a_norm), gckv=row(mla_kv_a_norm),
        wuq=wuq.astype(BF16), wukv=jnp.concatenate([wkn, wv], axis=-1).astype(BF16),
        gmq=row(jnp.tile(_pad_heads(mla_q_norm), (1, MLA_HEADS))),
        gmk=row(jnp.tile(_pad_heads(mla_k_norm), (1, MLA_HEADS))),
        conv_dw=jnp.pad(conv_dw, ((0, 0), (0, 1), (0, 0))), conv_b=row(conv_b), conv_g=row(conv_norm_g),
        lam=diff_lambda, subg=diff_subln[:, :, None],
        w_conv_out=w_conv_out.astype(BF16), w_diff_out=w_diff_out.astype(BF16),
        w_mla_out=w_mla_out.astype(BF16), w_out=w_out.astype(BF16),
        w_up=w_up.astype(BF16), w_down=w_down.astype(BF16),
    )


def kernel(x_prompt, x_sample, cache_diff_k, cache_diff_v, cache_mla_ckv, cache_mla_kpe, c, c_ctx, mod_w, mod_b, norm1_g, w_in, conv_dw, conv_b, conv_norm_g, w_conv_out, diff_q_norm, diff_k_norm, diff_lambda, diff_subln, w_diff_out, mla_q_a_norm, mla_kv_a_norm, w_uq, w_ukv, mla_q_norm, mla_k_norm, w_mla_out, w_out, norm2_g, w_up, w_down):
    batch, seq, _ = x_prompt.shape
    dec_batch, dec_seq, _ = x_sample.shape
    past = cache_diff_k.shape[2]
    assert seq == TOK_TILE and past == TOK_TILE and dec_seq % TOK_TILE == 0 and 1 + dec_batch <= 8

    cond_rows = jnp.concatenate([c_ctx[None], c, jnp.zeros((8 - 1 - dec_batch, D_MODEL), F32)], axis=0)
    mod = _modulation(cond_rows, mod_w, mod_b).reshape(DEPTH, 8, 6, D_MODEL)
    tabs = _rope_tables(dec_seq)

    W = _prepare_weights(norm1_g, w_in, conv_dw, conv_b, conv_norm_g, w_conv_out, diff_q_norm, diff_k_norm,
                         diff_lambda, diff_subln, w_diff_out, mla_q_a_norm, mla_kv_a_norm, w_uq, w_ukv,
                         mla_q_norm, mla_k_norm, w_mla_out, w_out, norm2_g, w_up, w_down)
    cache = (cache_diff_k, cache_diff_v, cache_mla_ckv,
             jnp.pad(cache_mla_kpe, ((0, 0), (0, 0), (0, 0), (MLA_NOPE, LANES - MLA_QK))))

    xp = x_prompt.reshape(batch * seq, D_MODEL)
    xs = x_sample.reshape(dec_batch * dec_seq, D_MODEL)
    new_state = None
    for l in range(DEPTH):
        lam_init = 0.8 - 0.6 * math.exp(-0.3 * l)

        u, qd, qm, kd, km, vdt, vmt, *new_state = _inproj(
            xp, mod[l], W, l, None, None, batch=batch, row0=0, states=True, prev_states=new_state)
        od, om = _attention(qd, qm, kd, km, vdt, vmt, W, l, batch=batch, lam_init=lam_init)
        xp = _merge_mlp(xp, mod[l], u, od, om, W, l, tiles_per_seq=1, row0=0)

        u, qd, qm, kd, km, vdt, vmt = _inproj(xs, mod[l], W, l, tabs, cache, batch=dec_batch, row0=1, states=False)
        od, om = _attention(qd, qm, kd, km, vdt, vmt, W, l, batch=dec_batch, lam_init=lam_init)
        xs = _merge_mlp(xs, mod[l], u, od, om, W, l, tiles_per_seq=dec_seq // TOK_TILE, row0=1)

    sk, sv, sckv, skpe = new_state
    return (xp.reshape(batch, seq, D_MODEL), xs.reshape(dec_batch, dec_seq, D_MODEL),
            sk.reshape(batch, DEPTH, seq, DIFF_HEADS, 2 * DIFF_HD),
            sv.reshape(batch, DEPTH, seq, DIFF_HEADS, 2 * DIFF_HD), sckv, skpe)
```

```python
import functools
import math

import numpy as np
import jax
import jax.numpy as jnp
from jax import lax
from jax.experimental import pallas as pl
from jax.experimental.pallas import tpu as pltpu

D_MODEL = 1024
DEPTH = 2
GRID_W = 64
ROPE_BASE = 10000.0
CONV_W = 512
CONV_K = 31
DIFF_HEADS = 4
DIFF_HD = 64
DIFF_W = DIFF_HEADS * 2 * DIFF_HD
MLA_HEADS = 8
MLA_NOPE = 64
MLA_ROPE = 32
MLA_QK = MLA_NOPE + MLA_ROPE
MLA_V = 64
MLA_Q_RANK = 384
MLA_KV_RANK = 256
MLA_W = MLA_HEADS * MLA_V
D_FF = 4 * D_MODEL
EPS = 1e-6

LANES = 128
SUBLANES = 8
MXU_W = 256
MLA_PAD = MLA_HEADS * LANES
TOK_TILE = 256
HALO = 16
N_MAIN = 2 * CONV_W + 3 * DIFF_W + MLA_Q_RANK + MLA_KV_RANK
VMEM_LIMIT = 56 * 1024 * 1024
LOG2E = math.log2(math.e)
F32 = jnp.float32
BF16 = jnp.bfloat16


def _silu(x):
    return x * jax.nn.sigmoid(x)


def _rms_full(x, g):
    ms = jnp.mean(x * x, axis=-1, keepdims=True)
    return x * lax.rsqrt(ms + EPS) * g


def _lane_tiles(x):
    return jnp.concatenate([x[:, t:t + MXU_W] for t in range(0, x.shape[-1], MXU_W)], axis=0)


def _from_lane_tiles(y, rows):
    return jnp.concatenate([y[r:r + rows] for r in range(0, y.shape[0], rows)], axis=-1)


def _group_rms(x, seg_ref, n, g):
    rows = x.shape[0]
    xt = _lane_tiles(x)
    ss = jnp.dot((xt * xt).astype(BF16), seg_ref[...], preferred_element_type=F32)
    return _from_lane_tiles(xt * lax.rsqrt(ss * (1.0 / n) + EPS), rows) * g


def _rope(x, perm_ref, cos, sin_signed):
    rows = x.shape[0]
    sw = jnp.dot(_lane_tiles(x).astype(BF16), perm_ref[...], preferred_element_type=F32)
    return x * cos + _from_lane_tiles(sw, rows) * sin_signed


def _tile_lanes(t, n):
    return jnp.concatenate([t] * n, axis=-1)


def _modulated_norm(x, g, shift, scale):
    return _rms_full(x, g) * (1.0 + scale) + shift


def _mod_kernel(s_ref, w_ref, b_ref, o_ref):
    s = _silu(s_ref[...]).astype(BF16)
    o_ref[...] = jnp.dot(s, w_ref[...].astype(BF16), preferred_element_type=F32) + b_ref[...]


def _modulation(cond_rows, mod_w, mod_b):
    tn = 1536
    n = 6 * D_MODEL
    return pl.pallas_call(
        _mod_kernel,
        grid=(DEPTH, n // tn),
        in_specs=[
            pl.BlockSpec((8, D_MODEL), lambda l, j: (0, 0)),
            pl.BlockSpec((None, D_MODEL, tn), lambda l, j: (l, 0, j)),
            pl.BlockSpec((None, 1, tn), lambda l, j: (l, 0, j)),
        ],
        out_specs=pl.BlockSpec((None, 8, tn), lambda l, j: (l, 0, j)),
        out_shape=jax.ShapeDtypeStruct((DEPTH, 8, n), F32),
        compiler_params=pltpu.CompilerParams(
            dimension_semantics=("parallel", "parallel"), vmem_limit_bytes=VMEM_LIMIT),
        name="modulation",
    )(cond_rows, mod_w, mod_b.reshape(DEPTH, 1, n))


def _mla_keys_values(ckvn, kpe_hi, wukv_ref):
    kv = jnp.dot(ckvn.astype(BF16), wukv_ref[...], preferred_element_type=F32)
    return kv[:, :MLA_PAD] + _tile_lanes(kpe_hi, MLA_HEADS), kv[:, MLA_PAD:]


def _inproj_kernel(*refs, rope, states, cache, n_prev):
    it = iter(refs)
    x_ref, mod_ref, g1_ref, w_ref, wkpe_ref = next(it), next(it), next(it), next(it), next(it)
    gq_ref, gk_ref, gcq_ref, gckv_ref = next(it), next(it), next(it), next(it)
    wuq_ref, wukv_ref, gmq_ref, gmk_ref = next(it), next(it), next(it), next(it)
    segd_ref, segm_ref = next(it), next(it)
    if rope:
        permd_ref, permm_ref = next(it), next(it)
        cd_ref, sd_ref, cm_ref, sm_ref = next(it), next(it), next(it), next(it)
    if cache:
        ck_ref, cv_ref, cckv_ref, ckpe_ref = next(it), next(it), next(it), next(it)
    prev_refs = [next(it) for _ in range(4)] if n_prev else []
    u_ref, qd_ref, qm_ref, kd_ref, km_ref, vdt_ref, vmt_ref = (next(it) for _ in range(7))
    if states:
        state_refs = sk_ref, sv_ref, sckv_ref, skpe_ref = next(it), next(it), next(it), next(it)
    w16_ref = next(it)
    tm = x_ref.shape[0]

    first = functools.reduce(jnp.logical_and, [pl.program_id(a) == 0 for a in range(2 if cache else 1)])

    @pl.when(first)
    def _():
        for c0 in range(0, N_MAIN, LANES):
            w16_ref[:, c0:c0 + LANES] = w_ref[c0:c0 + LANES, :].T.astype(BF16)

    def tokens():
        mod = mod_ref[...]
        h = _modulated_norm(x_ref[...], g1_ref[...], mod[0:1], mod[1:2]).astype(BF16)
        proj = jnp.dot(h, w16_ref[...], preferred_element_type=F32)

        o = 0
        u_a = proj[:, o:o + CONV_W]; o += CONV_W
        u_g = proj[:, o:o + CONV_W]; o += CONV_W
        dq = proj[:, o:o + DIFF_W]; o += DIFF_W
        dk = proj[:, o:o + DIFF_W]; o += DIFF_W
        dv = proj[:, o:o + DIFF_W]; o += DIFF_W
        cq = proj[:, o:o + MLA_Q_RANK]; o += MLA_Q_RANK
        ckv = proj[:, o:o + MLA_KV_RANK]; o += MLA_KV_RANK
        kpe2 = jnp.dot(h, wkpe_ref[...], preferred_element_type=F32)

        cqn = _rms_full(cq, gcq_ref[...])
        ckvn = _rms_full(ckv, gckv_ref[...])
        q = _group_rms(dq, segd_ref, DIFF_HD, gq_ref[...])
        k = _group_rms(dk, segd_ref, DIFF_HD, gk_ref[...])
        qm = jnp.dot(cqn.astype(BF16), wuq_ref[...], preferred_element_type=F32)
        lane = lax.broadcasted_iota(jnp.int32, (1, LANES), 1)
        km, vm = _mla_keys_values(ckvn, jnp.where(lane >= MLA_NOPE, kpe2, 0.0), wukv_ref)

        u_ref[...] = u_a * jax.nn.sigmoid(u_g)
        vdt_ref[...] = dv.T.astype(BF16)
        if states:
            for dst, src in zip(state_refs, prev_refs):
                dst[0:n_prev] = src[...]
            for hd in range(DIFF_HEADS):
                sl = slice(hd * LANES, (hd + 1) * LANES)
                sk_ref[n_prev, pl.ds(hd, tm, stride=DIFF_HEADS), :] = k[:, sl]
                sv_ref[n_prev, pl.ds(hd, tm, stride=DIFF_HEADS), :] = dv[:, sl]
            sckv_ref[n_prev] = ckvn
            skpe_ref[n_prev] = kpe2[:, :MLA_ROPE]

        qm = _group_rms(qm, segm_ref, MLA_QK, gmq_ref[...])
        km = _group_rms(km, segm_ref, MLA_QK, gmk_ref[...])
        if rope:
            cd = _tile_lanes(cd_ref[...], DIFF_W // LANES)
            sd = _tile_lanes(sd_ref[...], DIFF_W // LANES)
            q = _rope(q, permd_ref, cd, sd)
            k = _rope(k, permd_ref, cd, sd)
        qd_ref[...] = (q * (DIFF_HD ** -0.5 * LOG2E)).astype(BF16)
        kd_ref[...] = k.astype(BF16)
        vmt_ref[...] = vm.T.astype(BF16)
        if rope:
            cm = _tile_lanes(cm_ref[...], MLA_HEADS)
            sm = _tile_lanes(sm_ref[...], MLA_HEADS)
            qm = _rope(qm, permm_ref, cm, sm)
            km = _rope(km, permm_ref, cm, sm)
        qm_ref[...] = (qm * (MLA_QK ** -0.5 * LOG2E)).astype(BF16)
        km_ref[...] = km.astype(BF16)

    def cached_context():
        heads = lambda ref: jnp.concatenate([ref[:, hd, :] for hd in range(DIFF_HEADS)], axis=-1)
        kd_ref[...] = heads(ck_ref).astype(BF16)
        vdt_ref[...] = heads(cv_ref).T.astype(BF16)
        km, vm = _mla_keys_values(cckv_ref[...], ckpe_ref[...], wukv_ref)
        km_ref[...] = _group_rms(km, segm_ref, MLA_QK, gmk_ref[...]).astype(BF16)
        vmt_ref[...] = vm.T.astype(BF16)

    if cache:
        j = pl.program_id(1)
        pl.when(j == 0)(cached_context)
        pl.when(j > 0)(tokens)
    else:
        tokens()


def _layer_spec(w, l, grid_rank, **kw):
    index = (lambda i: (l, 0, 0)) if grid_rank == 1 else (lambda b, j: (l, 0, 0))
    return pl.BlockSpec((None,) + w.shape[1:], index, **kw)


def _inproj(x, mod_l, W, l, rope_tabs, cache, *, batch, row0, states, prev_states=None):
    t = x.shape[0]
    tm = TOK_TILE
    n = t // batch // tm
    rope = rope_tabs is not None
    has_cache = cache is not None
    nk = n + 1 if has_cache else n
    if has_cache:
        grid = (batch, nk)
        tok = lambda b, j: (b * n + jnp.maximum(j - 1, 0), 0)
        key = lambda b, j: (b * nk + j, 0)
        keyt = lambda b, j: (b, 0, j)
        const = lambda b, j: (0, 0)
        modrow = lambda b, j: (row0 + b, 0, 0)
        tab = lambda b, j: (jnp.maximum(j - 1, 0), 0)
    else:
        grid = (batch * n,)
        tok = lambda i: (i, 0)
        key = tok
        keyt = lambda i: (i // n, 0, i % n)
        const = lambda i: (0, 0)
        modrow = lambda i: (row0 + (i // n if row0 else 0), 0, 0)
        tab = lambda i: (i % n, 0)
    names = ("norm1_g", "w_in", "wkpe", "gq", "gk", "gcq", "gckv", "wuq", "wukv", "gmq", "gmk")
    in_specs = [pl.BlockSpec((tm, D_MODEL), tok), pl.BlockSpec((None, 6, D_MODEL), modrow)]
    in_specs += [_layer_spec(W[k], l, len(grid)) for k in names]
    in_specs[2 + names.index("w_in")] = pl.BlockSpec((None, N_MAIN, D_MODEL), in_specs[2].index_map,
                                                     pipeline_mode=pl.Buffered(1))
    in_specs += [pl.BlockSpec((MXU_W, MXU_W), const)] * 2
    seg_d, seg_m, perm_d, perm_m = _group_matrices()
    args = [x, mod_l] + [W[k] for k in names] + [seg_d, seg_m]
    if rope:
        in_specs += [pl.BlockSpec((MXU_W, MXU_W), const)] * 2 + [pl.BlockSpec((tm, LANES), tab)] * 4
        args += [perm_d, perm_m] + list(rope_tabs)
    if has_cache:
        in_specs += [pl.BlockSpec((None, None) + a.shape[2:], lambda b, j, r=a.ndim - 2: (b, l) + (0,) * r)
                     for a in cache]
        args += list(cache)
    sk = nk * tm
    out_shape = [
        jax.ShapeDtypeStruct((t, CONV_W), F32),
        jax.ShapeDtypeStruct((t, DIFF_W), BF16),
        jax.ShapeDtypeStruct((t, MLA_PAD), BF16),
        jax.ShapeDtypeStruct((batch * sk, DIFF_W), BF16),
        jax.ShapeDtypeStruct((batch * sk, MLA_PAD), BF16),
        jax.ShapeDtypeStruct((batch, DIFF_W, sk), BF16),
        jax.ShapeDtypeStruct((batch, MLA_W, sk), BF16),
    ]
    out_specs = [
        pl.BlockSpec((tm, CONV_W), tok),
        pl.BlockSpec((tm, DIFF_W), tok),
        pl.BlockSpec((tm, MLA_PAD), tok),
        pl.BlockSpec((tm, DIFF_W), key),
        pl.BlockSpec((tm, MLA_PAD), key),
        pl.BlockSpec((None, DIFF_W, tm), keyt),
        pl.BlockSpec((None, MLA_W, tm), keyt),
    ]
    n_prev = 0
    if states:
        assert n == 1
        n_prev = l
        st_shapes = ((tm * DIFF_HEADS, LANES), (tm * DIFF_HEADS, LANES), (tm, MLA_KV_RANK), (tm, MLA_ROPE))
        seq_block = lambda layers, s: pl.BlockSpec((None, layers) + s, lambda i: (i, 0, 0, 0))
        out_shape += [jax.ShapeDtypeStruct((batch, l + 1) + s, F32) for s in st_shapes]
        out_specs += [seq_block(l + 1, s) for s in st_shapes]
        if n_prev:
            in_specs += [seq_block(l, s) for s in st_shapes]
            args += list(prev_states)
    return pl.pallas_call(
        functools.partial(_inproj_kernel, rope=rope, states=states, cache=has_cache, n_prev=n_prev),
        grid=grid,
        in_specs=in_specs,
        out_specs=out_specs,
        out_shape=out_shape,
        scratch_shapes=[pltpu.VMEM((D_MODEL, N_MAIN), BF16)],
        compiler_params=pltpu.CompilerParams(
            dimension_semantics=("arbitrary",) * len(grid), vmem_limit_bytes=VMEM_LIMIT),
        name="inproj_latent" if has_cache else "inproj_context",
    )(*args)


def _fold_rows(x, op):
    n = x.shape[0]
    while n % (2 * SUBLANES) == 0 and n > 4 * SUBLANES:
        n //= 2
        x = op(x[:n], x[n:])
    return x


KEY_CHUNK = 256
SCORE_LEAD = 1
VALUE_LAG = 1


def _attn_kernel(qd_ref, qm_ref, kd_ref, km_ref, vdt_ref, vmt_ref, lam_ref, sub_ref, od_ref, om_ref, s_ref, p_ref,
                 *, lam_init):
    lp = lam_ref[...]
    lam = (jnp.exp(jnp.sum(lp[0:1] * lp[1:2], axis=-1, keepdims=True))
           - jnp.exp(jnp.sum(lp[2:3] * lp[3:4], axis=-1, keepdims=True)) + lam_init)
    lo = lax.broadcasted_iota(jnp.int32, (1, LANES), 1) < DIFF_HD
    top = lax.broadcasted_iota(jnp.int32, (LANES, 1), 0) < MLA_V
    subg = sub_ref[...]
    sk = kd_ref.shape[0]
    kc = min(KEY_CHUNK, sk)
    n_chunks = sk // kc

    jobs = []
    for h in range(DIFF_HEADS):
        sl = slice(h * LANES, (h + 1) * LANES)
        jobs += [(kd_ref, sl, qd_ref, 0, vdt_ref, sl), (kd_ref, sl, qd_ref, 1, vdt_ref, sl)]
    for h in range(MLA_HEADS):
        hs = slice(h * LANES, (h + 1) * LANES)
        pair = slice((h // 2) * LANES, (h // 2 + 1) * LANES)
        jobs.append((km_ref, hs, qm_ref, None, vmt_ref, pair))
    n = len(jobs)

    def query(i):
        _, sl, q_ref, half, _, _ = jobs[i]
        if half is None:
            return q_ref[:, sl]
        q = q_ref[:, sl].astype(F32)
        return (jnp.where(lo, q, 0.0) if half == 0 else jnp.where(lo, 0.0, q)).astype(BF16)

    qs, mcol, mfin, lcol, acc = {}, {}, {}, {}, {}
    outs = [None] * n
    slots = SCORE_LEAD + 1
    pslots = VALUE_LAG + 1
    for i in range(-SCORE_LEAD, n + VALUE_LAG):
        a, b, c = i + SCORE_LEAD, i, i - VALUE_LAG
        if 0 <= a < n:
            qs[a] = query(a)
        if 0 <= b < n:
            mfin[b] = jnp.max(mcol.pop(b), axis=0, keepdims=True)
        for r in range(n_chunks):
            rows = slice(r * kc, (r + 1) * kc)
            if 0 <= a < n:
                k_ref, ksl = jobs[a][0], jobs[a][1]
                s = lax.dot_general(k_ref[rows, ksl], qs[a], (((1,), (1,)), ((), ())),
                                    preferred_element_type=F32)
                s_ref[a % slots, rows, :] = s
                f = _fold_rows(s, jnp.maximum)
                mcol[a] = f if r == 0 else jnp.maximum(mcol[a], f)
            if 0 <= b < n:
                p = jnp.exp2(s_ref[b % slots, rows, :] - mfin[b])
                f = _fold_rows(p, jnp.add)
                lcol[b] = f if r == 0 else lcol[b] + f
                p_ref[b % pslots, rows, :] = p.astype(BF16)
            if 0 <= c < n:
                vt_ref, vrows = jobs[c][4], jobs[c][5]
                d = jnp.dot(vt_ref[vrows, rows], p_ref[c % pslots, rows, :], preferred_element_type=F32)
                acc[c] = d if r == 0 else acc[c] + d
        if 0 <= c < n:
            outs[c] = acc.pop(c) / jnp.sum(lcol.pop(c), axis=0, keepdims=True)

    for h in range(DIFF_HEADS):
        sl = slice(h * LANES, (h + 1) * LANES)
        a = outs[2 * h] - lam * outs[2 * h + 1]
        ms = jnp.mean(a * a, axis=0, keepdims=True)
        od = a * lax.rsqrt(ms + EPS) * subg * (1.0 - lam_init)
        od_ref[:, sl] = od.T.astype(BF16)
    base = 2 * DIFF_HEADS
    for hp in range(MLA_HEADS // 2):
        sl = slice(hp * LANES, (hp + 1) * LANES)
        om_ref[:, sl] = jnp.where(top, outs[base + 2 * hp], outs[base + 2 * hp + 1]).T.astype(BF16)


def _attention(qd, qm, kd, km, vdt, vmt, W, l, *, batch, lam_init):
    sq = qd.shape[0] // batch
    sk = kd.shape[0] // batch
    tq = TOK_TILE
    nq = sq // tq
    qrow = lambda b, i: (b * nq + i, 0)
    krow = lambda b, i: (b, 0)
    kcol = lambda b, i: (b, 0, 0)
    const = lambda b, i: (0, 0)
    return pl.pallas_call(
        functools.partial(_attn_kernel, lam_init=lam_init),
        grid=(batch, nq),
        in_specs=[
            pl.BlockSpec((tq, DIFF_W), qrow),
            pl.BlockSpec((tq, MLA_PAD), qrow),
            pl.BlockSpec((sk, DIFF_W), krow),
            pl.BlockSpec((sk, MLA_PAD), krow),
            pl.BlockSpec((None, DIFF_W, sk), kcol),
            pl.BlockSpec((None, MLA_W, sk), kcol),
            _layer_spec(W["lam"], l, 2),
            _layer_spec(W["subg"], l, 2),
        ],
        out_specs=[pl.BlockSpec((tq, DIFF_W), qrow), pl.BlockSpec((tq, MLA_W), qrow)],
        out_shape=[jax.ShapeDtypeStruct((batch * sq, DIFF_W), BF16),
                   jax.ShapeDtypeStruct((batch * sq, MLA_W), BF16)],
        scratch_shapes=[pltpu.VMEM((SCORE_LEAD + 1, sk, tq), F32), pltpu.VMEM((VALUE_LAG + 1, sk, tq), BF16)],
        compiler_params=pltpu.CompilerParams(
            dimension_semantics=("parallel", "arbitrary"), vmem_limit_bytes=VMEM_LIMIT),
        name="attention",
    )(qd, qm, kd, km, vdt, vmt, W["lam"], W["subg"])


CONV_SPAN = (CONV_K + SUBLANES - 1) // SUBLANES * SUBLANES - SUBLANES


def _depthwise_conv(buf_ref, phase_ref, dw_ref, tm):
    base = HALO - CONV_K // 2
    acc = None
    for r in range(SUBLANES):
        offs = [o for o in range(base, base + CONV_K) if o % SUBLANES == r]
        phase_ref[r] = buf_ref[r:r + tm + CONV_SPAN, :]
        for o in offs:
            term = phase_ref[r, o - r:o - r + tm, :] * dw_ref[o - base:o - base + 1, :]
            acc = term if acc is None else acc + term
    return acc


def _merge_mlp_kernel(*refs, tiles_per_seq):
    it = iter(refs)
    x_ref, mod_ref, g1_ref, g2_ref, u_ref = (next(it) for _ in range(5))
    if tiles_per_seq > 1:
        up_ref, un_ref = next(it), next(it)
    dw_ref, cb_ref, cg_ref, od_ref, om_ref = (next(it) for _ in range(5))
    wg_ref, wc_ref, wd_ref, wm_ref, wo_ref, wup_ref, wdn_ref, y_ref, buf_ref, phase_ref = (next(it) for _ in range(10))
    tm = x_ref.shape[0]

    mod = mod_ref[...]
    x = x_ref[...]
    h = _modulated_norm(x, g1_ref[...], mod[0:1], mod[1:2])
    gates = jax.nn.sigmoid(lax.dot_general(h.astype(BF16), wg_ref[...], (((1,), (1,)), ((), ())),
                                           preferred_element_type=F32))
    o_diff = jnp.dot(od_ref[...], wd_ref[...], preferred_element_type=F32)
    o_mla = jnp.dot(om_ref[...], wm_ref[...], preferred_element_type=F32)

    zeros = jnp.zeros((HALO, CONV_W), F32)
    if tiles_per_seq > 1:
        j = pl.program_id(0) % tiles_per_seq
        prev = jnp.where(j == 0, zeros, up_ref[...])
        nxt = jnp.where(j == tiles_per_seq - 1, zeros, un_ref[...])
    else:
        prev, nxt = zeros, zeros
    buf_ref[0:HALO, :] = prev
    buf_ref[HALO:HALO + tm, :] = u_ref[...]
    buf_ref[HALO + tm:HALO + tm + HALO, :] = nxt
    conv = _depthwise_conv(buf_ref, phase_ref, dw_ref, tm) + cb_ref[...]
    c = _silu(_rms_full(conv, cg_ref[...])).astype(BF16)
    o_conv = jnp.dot(c, wc_ref[...], preferred_element_type=F32)
    merged = (gates[:, 0:D_MODEL] * o_conv + gates[:, D_MODEL:2 * D_MODEL] * o_diff
              + gates[:, 2 * D_MODEL:3 * D_MODEL] * o_mla)
    x = x + mod[2:3] * jnp.dot(merged.astype(BF16), wo_ref[...], preferred_element_type=F32)
    h2 = _modulated_norm(x, g2_ref[...], mod[3:4], mod[4:5])
    up = jnp.dot(h2.astype(BF16), wup_ref[...], preferred_element_type=F32)
    act = jnp.square(jnp.maximum(up, 0.0)).astype(BF16)
    y_ref[...] = x + mod[5:6] * jnp.dot(act, wdn_ref[...], preferred_element_type=F32)


def _merge_mlp(x, mod_l, u, od, om, W, l, *, tiles_per_seq, row0):
    t = x.shape[0]
    tm = TOK_TILE
    nb = t // HALO
    per = tm // HALO
    row = lambda i: (i, 0)

    in_specs = [
        pl.BlockSpec((tm, D_MODEL), row),
        pl.BlockSpec((None, 6, D_MODEL), lambda i: (row0 + (i // tiles_per_seq if row0 else 0), 0, 0)),
        _layer_spec(W["norm1_g"], l, 1),
        _layer_spec(W["norm2_g"], l, 1),
        pl.BlockSpec((tm, CONV_W), row),
    ]
    args = [x, mod_l, W["norm1_g"], W["norm2_g"], u]
    if tiles_per_seq > 1:
        in_specs += [
            pl.BlockSpec((HALO, CONV_W), lambda i: (jnp.maximum(i * per - 1, 0), 0)),
            pl.BlockSpec((HALO, CONV_W), lambda i: (jnp.minimum((i + 1) * per, nb - 1), 0)),
        ]
        args += [u, u]
    big = ("wg", "w_conv_out", "w_diff_out", "w_mla_out", "w_out", "w_up", "w_down")
    in_specs += [_layer_spec(W[k], l, 1) for k in ("conv_dw", "conv_b", "conv_g")]
    in_specs += [pl.BlockSpec((tm, DIFF_W), row), pl.BlockSpec((tm, MLA_W), row)]
    in_specs += [_layer_spec(W[k], l, 1, pipeline_mode=pl.Buffered(1)) for k in big]
    args += [W["conv_dw"], W["conv_b"], W["conv_g"], od, om] + [W[k] for k in big]
    return pl.pallas_call(
        functools.partial(_merge_mlp_kernel, tiles_per_seq=tiles_per_seq),
        grid=(t // tm,),
        in_specs=in_specs,
        out_specs=pl.BlockSpec((tm, D_MODEL), row),
        out_shape=jax.ShapeDtypeStruct((t, D_MODEL), F32),
        scratch_shapes=[pltpu.VMEM((tm + 2 * HALO, CONV_W), F32),
                        pltpu.VMEM((SUBLANES, tm + CONV_SPAN, CONV_W), F32)],
        compiler_params=pltpu.CompilerParams(
            dimension_semantics=("parallel",), vmem_limit_bytes=VMEM_LIMIT),
        name="merge_mlp",
    )(*args)


def _rope_tables(seq_len):
    rows = seq_len // GRID_W
    row = np.repeat(np.arange(rows, dtype=np.float64), GRID_W)
    col = np.tile(np.arange(GRID_W, dtype=np.float64), rows)

    def tables(rot_dim):
        half = rot_dim // 2
        inv = ROPE_BASE ** (-np.arange(0, half, 2, dtype=np.float64) / half)
        a0 = row[:, None] * inv
        a1 = col[:, None] * inv
        cos = np.concatenate([np.cos(a0), np.cos(a0), np.cos(a1), np.cos(a1)], axis=1)
        sin = np.concatenate([-np.sin(a0), np.sin(a0), -np.sin(a1), np.sin(a1)], axis=1)
        return cos, sin

    cd, sd = tables(DIFF_HD)
    cd, sd = np.tile(cd, (1, LANES // DIFF_HD)), np.tile(sd, (1, LANES // DIFF_HD))
    cm, sm = tables(MLA_ROPE)
    pad = LANES - MLA_QK
    cm = np.concatenate([np.ones((seq_len, MLA_NOPE)), cm, np.ones((seq_len, pad))], axis=1)
    sm = np.concatenate([np.zeros((seq_len, MLA_NOPE)), sm, np.zeros((seq_len, pad))], axis=1)
    return tuple(jnp.asarray(t, dtype=F32) for t in (cd, sd, cm, sm))


def _group_matrices():
    lane = np.arange(MXU_W)
    seg_d = (lane[:, None] // DIFF_HD == lane[None, :] // DIFF_HD)
    seg_m = (lane[:, None] // LANES == lane[None, :] // LANES)

    def swap(rotary, half):
        src = np.where(lane % (2 * half) < half, lane + half, lane - half)
        return (lane[:, None] == src[None, :]) & rotary[None, :]

    perm_d = swap(np.ones(MXU_W, bool), DIFF_HD // 4)
    in_head = lane % LANES
    perm_m = swap((in_head >= MLA_NOPE) & (in_head < MLA_QK), MLA_ROPE // 4)
    return tuple(jnp.asarray(m, dtype=BF16) for m in (seg_d, seg_m, perm_d, perm_m))


def _pad_heads(v):
    return jnp.pad(v, [(0, 0)] * (v.ndim - 1) + [(0, LANES - MLA_QK)])


def _prepare_weights(norm1_g, w_in, conv_dw, conv_b, conv_norm_g, w_conv_out, diff_q_norm, diff_k_norm,
                     diff_lambda, diff_subln, w_diff_out, mla_q_a_norm, mla_kv_a_norm, w_uq, w_ukv,
                     mla_q_norm, mla_k_norm, w_mla_out, w_out, norm2_g, w_up, w_down):
    row = lambda v: v[:, None, :]
    kpe_cols = w_in[:, :, N_MAIN:N_MAIN + MLA_ROPE].astype(BF16)
    zc = jnp.zeros_like(kpe_cols)
    w_in_t = jnp.swapaxes(w_in, 1, 2)
    wuq = _pad_heads(w_uq.reshape(DEPTH, MLA_Q_RANK, MLA_HEADS, MLA_QK)).reshape(DEPTH, MLA_Q_RANK, MLA_PAD)
    wukv4 = w_ukv.reshape(DEPTH, MLA_KV_RANK, MLA_HEADS, MLA_NOPE + MLA_V)
    wkn = jnp.pad(wukv4[..., :MLA_NOPE], ((0, 0), (0, 0), (0, 0), (0, LANES - MLA_NOPE)))
    wkn = wkn.reshape(DEPTH, MLA_KV_RANK, MLA_PAD)
    wv = wukv4[..., MLA_NOPE:].reshape(DEPTH, MLA_KV_RANK, MLA_W)
    return dict(
        norm1_g=row(norm1_g), norm2_g=row(norm2_g),
        w_in=w_in_t,
        wkpe=jnp.concatenate([kpe_cols, zc, kpe_cols, zc], axis=-1),
        wg=w_in_t[:, N_MAIN + MLA_ROPE:, :].astype(BF16),
        gq=row(jnp.tile(diff_q_norm, (1, DIFF_W // DIFF_HD))),
        gk=row(jnp.tile(diff_k_norm, (1, DIFF_W // DIFF_HD))),
        gcq=row(mla_q_a_norm), gckv=row(mla_kv_a_norm),
        wuq=wuq.astype(BF16), wukv=jnp.concatenate([wkn, wv], axis=-1).astype(BF16),
        gmq=row(jnp.tile(_pad_heads(mla_q_norm), (1, MLA_HEADS))),
        gmk=row(jnp.tile(_pad_heads(mla_k_norm), (1, MLA_HEADS))),
        conv_dw=jnp.pad(conv_dw, ((0, 0), (0, 1), (0, 0))), conv_b=row(conv_b), conv_g=row(conv_norm_g),
        lam=diff_lambda, subg=diff_subln[:, :, None],
        w_conv_out=w_conv_out.astype(BF16), w_diff_out=w_diff_out.astype(BF16),
        w_mla_out=w_mla_out.astype(BF16), w_out=w_out.astype(BF16),
        w_up=w_up.astype(BF16), w_down=w_down.astype(BF16),
    )


def kernel(x_prompt, x_sample, cache_diff_k, cache_diff_v, cache_mla_ckv, cache_mla_kpe, c, c_ctx, mod_w, mod_b, norm1_g, w_in, conv_dw, conv_b, conv_norm_g, w_conv_out, diff_q_norm, diff_k_norm, diff_lambda, diff_subln, w_diff_out, mla_q_a_norm, mla_kv_a_norm, w_uq, w_ukv, mla_q_norm, mla_k_norm, w_mla_out, w_out, norm2_g, w_up, w_down):
    batch, seq, _ = x_prompt.shape
    dec_batch, dec_seq, _ = x_sample.shape
    past = cache_diff_k.shape[2]
    assert seq == TOK_TILE and past == TOK_TILE and dec_seq % TOK_TILE == 0 and 1 + dec_batch <= 8

    cond_rows = jnp.concatenate([c_ctx[None], c, jnp.zeros((8 - 1 - dec_batch, D_MODEL), F32)], axis=0)
    mod = _modulation(cond_rows, mod_w, mod_b).reshape(DEPTH, 8, 6, D_MODEL)
    tabs = _rope_tables(dec_seq)

    W = _prepare_weights(norm1_g, w_in, conv_dw, conv_b, conv_norm_g, w_conv_out, diff_q_norm, diff_k_norm,
                         diff_lambda, diff_subln, w_diff_out, mla_q_a_norm, mla_kv_a_norm, w_uq, w_ukv,
                         mla_q_norm, mla_k_norm, w_mla_out, w_out, norm2_g, w_up, w_down)
    cache = (cache_diff_k, cache_diff_v, cache_mla_ckv,
             jnp.pad(cache_mla_kpe, ((0, 0), (0, 0), (0, 0), (MLA_NOPE, LANES - MLA_QK))))

    xp = x_prompt.reshape(batch * seq, D_MODEL)
    xs = x_sample.reshape(dec_batch * dec_seq, D_MODEL)
    new_state = None
    for l in range(DEPTH):
        lam_init = 0.8 - 0.6 * math.exp(-0.3 * l)

        u, qd, qm, kd, km, vdt, vmt, *new_state = _inproj(
            xp, mod[l], W, l, None, None, batch=batch, row0=0, states=True, prev_states=new_state)
        od, om = _attention(qd, qm, kd, km, vdt, vmt, W, l, batch=batch, lam_init=lam_init)
        xp = _merge_mlp(xp, mod[l], u, od, om, W, l, tiles_per_seq=1, row0=0)

        u, qd, qm, kd, km, vdt, vmt = _inproj(xs, mod[l], W, l, tabs, cache, batch=dec_batch, row0=1, states=False)
        od, om = _attention(qd, qm, kd, km, vdt, vmt, W, l, batch=dec_batch, lam_init=lam_init)
        xs = _merge_mlp(xs, mod[l], u, od, om, W, l, tiles_per_seq=dec_seq // TOK_TILE, row0=1)

    sk, sv, sckv, skpe = new_state
    return (xp.reshape(batch, seq, D_MODEL), xs.reshape(dec_batch, dec_seq, D_MODEL),
            sk.reshape(batch, DEPTH, seq, DIFF_HEADS, 2 * DIFF_HD),
            sv.reshape(batch, DEPTH, seq, DIFF_HEADS, 2 * DIFF_HD), sckv, skpe)
```

```python
import functools
import math

import numpy as np
import jax
import jax.numpy as jnp
from jax import lax
from jax.experimental import pallas as pl
from jax.experimental.pallas import tpu as pltpu

D_MODEL = 1024
DEPTH = 2
GRID_W = 64
ROPE_BASE = 10000.0
CONV_W = 512
CONV_K = 31
DIFF_HEADS = 4
DIFF_HD = 64
DIFF_W = DIFF_HEADS * 2 * DIFF_HD
MLA_HEADS = 8
MLA_NOPE = 64
MLA_ROPE = 32
MLA_QK = MLA_NOPE + MLA_ROPE
MLA_V = 64
MLA_Q_RANK = 384
MLA_KV_RANK = 256
MLA_W = MLA_HEADS * MLA_V
D_FF = 4 * D_MODEL
EPS = 1e-6

LANES = 128
SUBLANES = 8
MXU_W = 256
MLA_PAD = MLA_HEADS * LANES
TOK_TILE = 256
HALO = 16
N_MAIN = 2 * CONV_W + 3 * DIFF_W + MLA_Q_RANK + MLA_KV_RANK
VMEM_LIMIT = 56 * 1024 * 1024
LOG2E = math.log2(math.e)
F32 = jnp.float32
BF16 = jnp.bfloat16


def _silu(x):
    return x * jax.nn.sigmoid(x)


def _rms_full(x, g):
    ms = jnp.mean(x * x, axis=-1, keepdims=True)
    return x * lax.rsqrt(ms + EPS) * g


def _lane_tiles(x):
    return jnp.concatenate([x[:, t:t + MXU_W] for t in range(0, x.shape[-1], MXU_W)], axis=0)


def _from_lane_tiles(y, rows):
    return jnp.concatenate([y[r:r + rows] for r in range(0, y.shape[0], rows)], axis=-1)


def _group_rms(x, seg_ref, n, g):
    rows = x.shape[0]
    xt = _lane_tiles(x)
    ss = jnp.dot((xt * xt).astype(BF16), seg_ref[...], preferred_element_type=F32)
    return _from_lane_tiles(xt * lax.rsqrt(ss * (1.0 / n) + EPS), rows) * g


def _rope(x, perm_ref, cos, sin_signed):
    rows = x.shape[0]
    sw = jnp.dot(_lane_tiles(x).astype(BF16), perm_ref[...], preferred_element_type=F32)
    return x * cos + _from_lane_tiles(sw, rows) * sin_signed


def _tile_lanes(t, n):
    return jnp.concatenate([t] * n, axis=-1)


def _modulated_norm(x, g, shift, scale):
    return _rms_full(x, g) * (1.0 + scale) + shift


def _mod_kernel(s_ref, w_ref, b_ref, o_ref):
    s = _silu(s_ref[...]).astype(BF16)
    o_ref[...] = jnp.dot(s, w_ref[...].astype(BF16), preferred_element_type=F32) + b_ref[...]


def _modulation(cond_rows, mod_w, mod_b):
    tn = 1536
    n = 6 * D_MODEL
    return pl.pallas_call(
        _mod_kernel,
        grid=(DEPTH, n // tn),
        in_specs=[
            pl.BlockSpec((8, D_MODEL), lambda l, j: (0, 0)),
            pl.BlockSpec((None, D_MODEL, tn), lambda l, j: (l, 0, j)),
            pl.BlockSpec((None, 1, tn), lambda l, j: (l, 0, j)),
        ],
        out_specs=pl.BlockSpec((None, 8, tn), lambda l, j: (l, 0, j)),
        out_shape=jax.ShapeDtypeStruct((DEPTH, 8, n), F32),
        compiler_params=pltpu.CompilerParams(
            dimension_semantics=("parallel", "parallel"), vmem_limit_bytes=VMEM_LIMIT),
        name="modulation",
    )(cond_rows, mod_w, mod_b.reshape(DEPTH, 1, n))


def _mla_keys_values(ckvn, kpe_hi, wukv_ref):
    kv = jnp.dot(ckvn.astype(BF16), wukv_ref[...], preferred_element_type=F32)
    return kv[:, :MLA_PAD] + _tile_lanes(kpe_hi, MLA_HEADS), kv[:, MLA_PAD:]


def _inproj_kernel(*refs, rope, states, cache, n_prev):
    it = iter(refs)
    x_ref, mod_ref, g1_ref, w_ref, wkpe_ref = next(it), next(it), next(it), next(it), next(it)
    gq_ref, gk_ref, gcq_ref, gckv_ref = next(it), next(it), next(it), next(it)
    wuq_ref, wukv_ref, gmq_ref, gmk_ref = next(it), next(it), next(it), next(it)
    segd_ref, segm_ref = next(it), next(it)
    if rope:
        permd_ref, permm_ref = next(it), next(it)
        cd_ref, sd_ref, cm_ref, sm_ref = next(it), next(it), next(it), next(it)
    if cache:
        ck_ref, cv_ref, cckv_ref, ckpe_ref = next(it), next(it), next(it), next(it)
    prev_refs = [next(it) for _ in range(4)] if n_prev else []
    u_ref, qd_ref, qm_ref, kd_ref, km_ref, vdt_ref, vmt_ref = (next(it) for _ in range(7))
    if states:
        state_refs = sk_ref, sv_ref, sckv_ref, skpe_ref = next(it), next(it), next(it), next(it)
    w16_ref = next(it)
    tm = x_ref.shape[0]

    first = functools.reduce(jnp.logical_and, [pl.program_id(a) == 0 for a in range(2 if cache else 1)])

    @pl.when(first)
    def _():
        for c0 in range(0, N_MAIN, LANES):
            w16_ref[:, c0:c0 + LANES] = w_ref[c0:c0 + LANES, :].T.astype(BF16)

    def tokens():
        mod = mod_ref[...]
        h = _modulated_norm(x_ref[...], g1_ref[...], mod[0:1], mod[1:2]).astype(BF16)
        proj = jnp.dot(h, w16_ref[...], preferred_element_type=F32)

        o = 0
        u_a = proj[:, o:o + CONV_W]; o += CONV_W
        u_g = proj[:, o:o + CONV_W]; o += CONV_W
        dq = proj[:, o:o + DIFF_W]; o += DIFF_W
        dk = proj[:, o:o + DIFF_W]; o += DIFF_W
        dv = proj[:, o:o + DIFF_W]; o += DIFF_W
        cq = proj[:, o:o + MLA_Q_RANK]; o += MLA_Q_RANK
        ckv = proj[:, o:o + MLA_KV_RANK]; o += MLA_KV_RANK
        kpe2 = jnp.dot(h, wkpe_ref[...], preferred_element_type=F32)

        cqn = _rms_full(cq, gcq_ref[...])
        ckvn = _rms_full(ckv, gckv_ref[...])
        q = _group_rms(dq, segd_ref, DIFF_HD, gq_ref[...])
        k = _group_rms(dk, segd_ref, DIFF_HD, gk_ref[...])
        qm = jnp.dot(cqn.astype(BF16), wuq_ref[...], preferred_element_type=F32)
        lane = lax.broadcasted_iota(jnp.int32, (1, LANES), 1)
        km, vm = _mla_keys_values(ckvn, jnp.where(lane >= MLA_NOPE, kpe2, 0.0), wukv_ref)

        u_ref[...] = u_a * jax.nn.sigmoid(u_g)
        vdt_ref[...] = dv.T.astype(BF16)
        if states:
            for dst, src in zip(state_refs, prev_refs):
                dst[0:n_prev] = src[...]
            for hd in range(DIFF_HEADS):
                sl = slice(hd * LANES, (hd + 1) * LANES)
                sk_ref[n_prev, pl.ds(hd, tm, stride=DIFF_HEADS), :] = k[:, sl]
                sv_ref[n_prev, pl.ds(hd, tm, stride=DIFF_HEADS), :] = dv[:, sl]
            sckv_ref[n_prev] = ckvn
            skpe_ref[n_prev] = kpe2[:, :MLA_ROPE]

        qm = _group_rms(qm, segm_ref, MLA_QK, gmq_ref[...])
        km = _group_rms(km, segm_ref, MLA_QK, gmk_ref[...])
        if rope:
            cd = _tile_lanes(cd_ref[...], DIFF_W // LANES)
            sd = _tile_lanes(sd_ref[...], DIFF_W // LANES)
            q = _rope(q, permd_ref, cd, sd)
            k = _rope(k, permd_ref, cd, sd)
        qd_ref[...] = (q * (DIFF_HD ** -0.5 * LOG2E)).astype(BF16)
        kd_ref[...] = k.astype(BF16)
        vmt_ref[...] = vm.T.astype(BF16)
        if rope:
            cm = _tile_lanes(cm_ref[...], MLA_HEADS)
            sm = _tile_lanes(sm_ref[...], MLA_HEADS)
            qm = _rope(qm, permm_ref, cm, sm)
            km = _rope(km, permm_ref, cm, sm)
        qm_ref[...] = (qm * (MLA_QK ** -0.5 * LOG2E)).astype(BF16)
        km_ref[...] = km.astype(BF16)

    def cached_context():
        heads = lambda ref: jnp.concatenate([ref[:, hd, :] for hd in range(DIFF_HEADS)], axis=-1)
        kd_ref[...] = heads(ck_ref).astype(BF16)
        vdt_ref[...] = heads(cv_ref).T.astype(BF16)
        km, vm = _mla_keys_values(cckv_ref[...], ckpe_ref[...], wukv_ref)
        km_ref[...] = _group_rms(km, segm_ref, MLA_QK, gmk_ref[...]).astype(BF16)
        vmt_ref[...] = vm.T.astype(BF16)

    if cache:
        j = pl.program_id(1)
        pl.when(j == 0)(cached_context)
        pl.when(j > 0)(tokens)
    else:
        tokens()


def _layer_spec(w, l, grid_rank, **kw):
    index = (lambda i: (l, 0, 0)) if grid_rank == 1 else (lambda b, j: (l, 0, 0))
    return pl.BlockSpec((None,) + w.shape[1:], index, **kw)


def _inproj(x, mod_l, W, l, rope_tabs, cache, *, batch, row0, states, prev_states=None):
    t = x.shape[0]
    tm = TOK_TILE
    n = t // batch // tm
    rope = rope_tabs is not None
    has_cache = cache is not None
    nk = n + 1 if has_cache else n
    if has_cache:
        grid = (batch, nk)
        tok = lambda b, j: (b * n + jnp.maximum(j - 1, 0), 0)
        key = lambda b, j: (b * nk + j, 0)
        keyt = lambda b, j: (b, 0, j)
        const = lambda b, j: (0, 0)
        modrow = lambda b, j: (row0 + b, 0, 0)
        tab = lambda b, j: (jnp.maximum(j - 1, 0), 0)
    else:
        grid = (batch * n,)
        tok = lambda i: (i, 0)
        key = tok
        keyt = lambda i: (i // n, 0, i % n)
        const = lambda i: (0, 0)
        modrow = lambda i: (row0 + (i // n if row0 else 0), 0, 0)
        tab = lambda i: (i % n, 0)
    names = ("norm1_g", "w_in", "wkpe", "gq", "gk", "gcq", "gckv", "wuq", "wukv", "gmq", "gmk")
    in_specs = [pl.BlockSpec((tm, D_MODEL), tok), pl.BlockSpec((None, 6, D_MODEL), modrow)]
    in_specs += [_layer_spec(W[k], l, len(grid)) for k in names]
    in_specs[2 + names.index("w_in")] = pl.BlockSpec((None, N_MAIN, D_MODEL), in_specs[2].index_map,
                                                     pipeline_mode=pl.Buffered(1))
    in_specs += [pl.BlockSpec((MXU_W, MXU_W), const)] * 2
    seg_d, seg_m, perm_d, perm_m = _group_matrices()
    args = [x, mod_l] + [W[k] for k in names] + [seg_d, seg_m]
    if rope:
        in_specs += [pl.BlockSpec((MXU_W, MXU_W), const)] * 2 + [pl.BlockSpec((tm, LANES), tab)] * 4
        args += [perm_d, perm_m] + list(rope_tabs)
    if has_cache:
        in_specs += [pl.BlockSpec((None, None) + a.shape[2:], lambda b, j, r=a.ndim - 2: (b, l) + (0,) * r)
                     for a in cache]
        args += list(cache)
    sk = nk * tm
    out_shape = [
        jax.ShapeDtypeStruct((t, CONV_W), F32),
        jax.ShapeDtypeStruct((t, DIFF_W), BF16),
        jax.ShapeDtypeStruct((t, MLA_PAD), BF16),
        jax.ShapeDtypeStruct((batch * sk, DIFF_W), BF16),
        jax.ShapeDtypeStruct((batch * sk, MLA_PAD), BF16),
        jax.ShapeDtypeStruct((batch, DIFF_W, sk), BF16),
        jax.ShapeDtypeStruct((batch, MLA_W, sk), BF16),
    ]
    out_specs = [
        pl.BlockSpec((tm, CONV_W), tok),
        pl.BlockSpec((tm, DIFF_W), tok),
        pl.BlockSpec((tm, MLA_PAD), tok),
        pl.BlockSpec((tm, DIFF_W), key),
        pl.BlockSpec((tm, MLA_PAD), key),
        pl.BlockSpec((None, DIFF_W, tm), keyt),
        pl.BlockSpec((None, MLA_W, tm), keyt),
    ]
    n_prev = 0
    if states:
        assert n == 1
        n_prev = l
        st_shapes = ((tm * DIFF_HEADS, LANES), (tm * DIFF_HEADS, LANES), (tm, MLA_KV_RANK), (tm, MLA_ROPE))
        seq_block = lambda layers, s: pl.BlockSpec((None, layers) + s, lambda i: (i, 0, 0, 0))
        out_shape += [jax.ShapeDtypeStruct((batch, l + 1) + s, F32) for s in st_shapes]
        out_specs += [seq_block(l + 1, s) for s in st_shapes]
        if n_prev:
            in_specs += [seq_block(l, s) for s in st_shapes]
            args += list(prev_states)
    return pl.pallas_call(
        functools.partial(_inproj_kernel, rope=rope, states=states, cache=has_cache, n_prev=n_prev),
        grid=grid,
        in_specs=in_specs,
        out_specs=out_specs,
        out_shape=out_shape,
        scratch_shapes=[pltpu.VMEM((D_MODEL, N_MAIN), BF16)],
        compiler_params=pltpu.CompilerParams(
            dimension_semantics=("arbitrary",) * len(grid), vmem_limit_bytes=VMEM_LIMIT),
        name="inproj_latent" if has_cache else "inproj_context",
    )(*args)


def _fold_rows(x, op):
    n = x.shape[0]
    while n % (2 * SUBLANES) == 0 and n > 4 * SUBLANES:
        n //= 2
        x = op(x[:n], x[n:])
    return x


KEY_CHUNK = 256
SCORE_LEAD_CHUNKED = 1
SCORE_LEAD_SINGLE = 2
VALUE_LAG = 1


def _attn_kernel(qd_ref, qm_ref, kd_ref, km_ref, vdt_ref, vmt_ref, lam_ref, sub_ref, od_ref, om_ref, s_ref, p_ref,
                 *, lam_init):
    lp = lam_ref[...]
    lam = (jnp.exp(jnp.sum(lp[0:1] * lp[1:2], axis=-1, keepdims=True))
           - jnp.exp(jnp.sum(lp[2:3] * lp[3:4], axis=-1, keepdims=True)) + lam_init)
    lo = lax.broadcasted_iota(jnp.int32, (1, LANES), 1) < DIFF_HD
    top = lax.broadcasted_iota(jnp.int32, (LANES, 1), 0) < MLA_V
    subg = sub_ref[...]
    sk = kd_ref.shape[0]
    kc = min(KEY_CHUNK, sk)
    n_chunks = sk // kc

    jobs = []
    for h in range(DIFF_HEADS):
        sl = slice(h * LANES, (h + 1) * LANES)
        jobs += [(kd_ref, sl, qd_ref, 0, vdt_ref, sl), (kd_ref, sl, qd_ref, 1, vdt_ref, sl)]
    for h in range(MLA_HEADS):
        hs = slice(h * LANES, (h + 1) * LANES)
        pair = slice((h // 2) * LANES, (h // 2 + 1) * LANES)
        jobs.append((km_ref, hs, qm_ref, None, vmt_ref, pair))
    n = len(jobs)

    def query(i):
        _, sl, q_ref, half, _, _ = jobs[i]
        if half is None:
            return q_ref[:, sl]
        q = q_ref[:, sl].astype(F32)
        return (jnp.where(lo, q, 0.0) if half == 0 else jnp.where(lo, 0.0, q)).astype(BF16)

    qs, mcol, mfin, lcol, acc = {}, {}, {}, {}, {}
    outs = [None] * n
    slots = s_ref.shape[0]
    lead = slots - 1
    pslots = VALUE_LAG + 1
    for i in range(-lead, n + VALUE_LAG):
        a, b, c = i + lead, i, i - VALUE_LAG
        if 0 <= a < n:
            qs[a] = query(a)
        if 0 <= b < n:
            mfin[b] = jnp.max(mcol.pop(b), axis=0, keepdims=True)
        for r in range(n_chunks):
            rows = slice(r * kc, (r + 1) * kc)
            if 0 <= a < n:
                k_ref, ksl = jobs[a][0], jobs[a][1]
                s = lax.dot_general(k_ref[rows, ksl], qs[a], (((1,), (1,)), ((), ())),
                                    preferred_element_type=F32)
                s_ref[a % slots, rows, :] = s
                f = _fold_rows(s, jnp.maximum)
                mcol[a] = f if r == 0 else jnp.maximum(mcol[a], f)
            if 0 <= b < n:
                p = jnp.exp2(s_ref[b % slots, rows, :] - mfin[b])
                f = _fold_rows(p, jnp.add)
                lcol[b] = f if r == 0 else lcol[b] + f
                p_ref[b % pslots, rows, :] = p.astype(BF16)
            if 0 <= c < n:
                vt_ref, vrows = jobs[c][4], jobs[c][5]
                d = jnp.dot(vt_ref[vrows, rows], p_ref[c % pslots, rows, :], preferred_element_type=F32)
                acc[c] = d if r == 0 else acc[c] + d
        if 0 <= c < n:
            outs[c] = acc.pop(c) / jnp.sum(lcol.pop(c), axis=0, keepdims=True)

    for h in range(DIFF_HEADS):
        sl = slice(h * LANES, (h + 1) * LANES)
        a = outs[2 * h] - lam * outs[2 * h + 1]
        ms = jnp.mean(a * a, axis=0, keepdims=True)
        od = a * lax.rsqrt(ms + EPS) * subg * (1.0 - lam_init)
        od_ref[:, sl] = od.T.astype(BF16)
    base = 2 * DIFF_HEADS
    for hp in range(MLA_HEADS // 2):
        sl = slice(hp * LANES, (hp + 1) * LANES)
        om_ref[:, sl] = jnp.where(top, outs[base + 2 * hp], outs[base + 2 * hp + 1]).T.astype(BF16)


def _attention(qd, qm, kd, km, vdt, vmt, W, l, *, batch, lam_init):
    sq = qd.shape[0] // batch
    sk = kd.shape[0] // batch
    tq = TOK_TILE
    nq = sq // tq
    qrow = lambda b, i: (b * nq + i, 0)
    krow = lambda b, i: (b, 0)
    kcol = lambda b, i: (b, 0, 0)
    lead = SCORE_LEAD_CHUNKED if sk > KEY_CHUNK else SCORE_LEAD_SINGLE
    return pl.pallas_call(
        functools.partial(_attn_kernel, lam_init=lam_init),
        grid=(batch, nq),
        in_specs=[
            pl.BlockSpec((tq, DIFF_W), qrow),
            pl.BlockSpec((tq, MLA_PAD), qrow),
            pl.BlockSpec((sk, DIFF_W), krow),
            pl.BlockSpec((sk, MLA_PAD), krow),
            pl.BlockSpec((None, DIFF_W, sk), kcol),
            pl.BlockSpec((None, MLA_W, sk), kcol),
            _layer_spec(W["lam"], l, 2),
            _layer_spec(W["subg"], l, 2),
        ],
        out_specs=[pl.BlockSpec((tq, DIFF_W), qrow), pl.BlockSpec((tq, MLA_W), qrow)],
        out_shape=[jax.ShapeDtypeStruct((batch * sq, DIFF_W), BF16),
                   jax.ShapeDtypeStruct((batch * sq, MLA_W), BF16)],
        scratch_shapes=[pltpu.VMEM((lead + 1, sk, tq), F32), pltpu.VMEM((VALUE_LAG + 1, sk, tq), BF16)],
        compiler_params=pltpu.CompilerParams(
            dimension_semantics=("parallel", "arbitrary"), vmem_limit_bytes=VMEM_LIMIT),
        name="attention",
    )(qd, qm, kd, km, vdt, vmt, W["lam"], W["subg"])


CONV_SPAN = (CONV_K + SUBLANES - 1) // SUBLANES * SUBLANES - SUBLANES


def _depthwise_conv(buf_ref, phase_ref, dw_ref, tm):
    base = HALO - CONV_K // 2
    acc = None
    for r in range(SUBLANES):
        offs = [o for o in range(base, base + CONV_K) if o % SUBLANES == r]
        phase_ref[r] = buf_ref[r:r + tm + CONV_SPAN, :]
        for o in offs:
            term = phase_ref[r, o - r:o - r + tm, :] * dw_ref[o - base:o - base + 1, :]
            acc = term if acc is None else acc + term
    return acc


def _merge_mlp_kernel(*refs, tiles_per_seq):
    it = iter(refs)
    x_ref, mod_ref, g1_ref, g2_ref, u_ref = (next(it) for _ in range(5))
    if tiles_per_seq > 1:
        up_ref, un_ref = next(it), next(it)
    dw_ref, cb_ref, cg_ref, od_ref, om_ref = (next(it) for _ in range(5))
    wg_ref, wc_ref, wd_ref, wm_ref, wo_ref, wup_ref, wdn_ref, y_ref, buf_ref, phase_ref = (next(it) for _ in range(10))
    tm = x_ref.shape[0]

    mod = mod_ref[...]
    x = x_ref[...]
    h = _modulated_norm(x, g1_ref[...], mod[0:1], mod[1:2])
    gates = jax.nn.sigmoid(lax.dot_general(h.astype(BF16), wg_ref[...], (((1,), (1,)), ((), ())),
                                           preferred_element_type=F32))
    o_diff = jnp.dot(od_ref[...], wd_ref[...], preferred_element_type=F32)
    o_mla = jnp.dot(om_ref[...], wm_ref[...], preferred_element_type=F32)

    zeros = jnp.zeros((HALO, CONV_W), F32)
    if tiles_per_seq > 1:
        j = pl.program_id(0) % tiles_per_seq
        prev = jnp.where(j == 0, zeros, up_ref[...])
        nxt = jnp.where(j == tiles_per_seq - 1, zeros, un_ref[...])
    else:
        prev, nxt = zeros, zeros
    buf_ref[0:HALO, :] = prev
    buf_ref[HALO:HALO + tm, :] = u_ref[...]
    buf_ref[HALO + tm:HALO + tm + HALO, :] = nxt
    conv = _depthwise_conv(buf_ref, phase_ref, dw_ref, tm) + cb_ref[...]
    c = _silu(_rms_full(conv, cg_ref[...])).astype(BF16)
    o_conv = jnp.dot(c, wc_ref[...], preferred_element_type=F32)
    merged = (gates[:, 0:D_MODEL] * o_conv + gates[:, D_MODEL:2 * D_MODEL] * o_diff
              + gates[:, 2 * D_MODEL:3 * D_MODEL] * o_mla)
    x = x + mod[2:3] * jnp.dot(merged.astype(BF16), wo_ref[...], preferred_element_type=F32)
    h2 = _modulated_norm(x, g2_ref[...], mod[3:4], mod[4:5])
    up = jnp.dot(h2.astype(BF16), wup_ref[...], preferred_element_type=F32)
    act = jnp.square(jnp.maximum(up, 0.0)).astype(BF16)
    y_ref[...] = x + mod[5:6] * jnp.dot(act, wdn_ref[...], preferred_element_type=F32)


def _merge_mlp(x, mod_l, u, od, om, W, l, *, tiles_per_seq, row0):
    t = x.shape[0]
    tm = TOK_TILE
    nb = t // HALO
    per = tm // HALO
    row = lambda i: (i, 0)

    in_specs = [
        pl.BlockSpec((tm, D_MODEL), row),
        pl.BlockSpec((None, 6, D_MODEL), lambda i: (row0 + (i // tiles_per_seq if row0 else 0), 0, 0)),
        _layer_spec(W["norm1_g"], l, 1),
        _layer_spec(W["norm2_g"], l, 1),
        pl.BlockSpec((tm, CONV_W), row),
    ]
    args = [x, mod_l, W["norm1_g"], W["norm2_g"], u]
    if tiles_per_seq > 1:
        in_specs += [
            pl.BlockSpec((HALO, CONV_W), lambda i: (jnp.maximum(i * per - 1, 0), 0)),
            pl.BlockSpec((HALO, CONV_W), lambda i: (jnp.minimum((i + 1) * per, nb - 1), 0)),
        ]
        args += [u, u]
    big = ("wg", "w_conv_out", "w_diff_out", "w_mla_out", "w_out", "w_up", "w_down")
    in_specs += [_layer_spec(W[k], l, 1) for k in ("conv_dw", "conv_b", "conv_g")]
    in_specs += [pl.BlockSpec((tm, DIFF_W), row), pl.BlockSpec((tm, MLA_W), row)]
    in_specs += [_layer_spec(W[k], l, 1, pipeline_mode=pl.Buffered(1)) for k in big]
    args += [W["conv_dw"], W["conv_b"], W["conv_g"], od, om] + [W[k] for k in big]
    return pl.pallas_call(
        functools.partial(_merge_mlp_kernel, tiles_per_seq=tiles_per_seq),
        grid=(t // tm,),
        in_specs=in_specs,
        out_specs=pl.BlockSpec((tm, D_MODEL), row),
        out_shape=jax.ShapeDtypeStruct((t, D_MODEL), F32),
        scratch_shapes=[pltpu.VMEM((tm + 2 * HALO, CONV_W), F32),
                        pltpu.VMEM((SUBLANES, tm + CONV_SPAN, CONV_W), F32)],
        compiler_params=pltpu.CompilerParams(
            dimension_semantics=("parallel",), vmem_limit_bytes=VMEM_LIMIT),
        name="merge_mlp",
    )(*args)


def _rope_tables(seq_len):
    rows = seq_len // GRID_W
    row = np.repeat(np.arange(rows, dtype=np.float64), GRID_W)
    col = np.tile(np.arange(GRID_W, dtype=np.float64), rows)

    def tables(rot_dim):
        half = rot_dim // 2
        inv = ROPE_BASE ** (-np.arange(0, half, 2, dtype=np.float64) / half)
        a0 = row[:, None] * inv
        a1 = col[:, None] * inv
        cos = np.concatenate([np.cos(a0), np.cos(a0), np.cos(a1), np.cos(a1)], axis=1)
        sin = np.concatenate([-np.sin(a0), np.sin(a0), -np.sin(a1), np.sin(a1)], axis=1)
        return cos, sin

    cd, sd = tables(DIFF_HD)
    cd, sd = np.tile(cd, (1, LANES // DIFF_HD)), np.tile(sd, (1, LANES // DIFF_HD))
    cm, sm = tables(MLA_ROPE)
    pad = LANES - MLA_QK
    cm = np.concatenate([np.ones((seq_len, MLA_NOPE)), cm, np.ones((seq_len, pad))], axis=1)
    sm = np.concatenate([np.zeros((seq_len, MLA_NOPE)), sm, np.zeros((seq_len, pad))], axis=1)
    return tuple(jnp.asarray(t, dtype=F32) for t in (cd, sd, cm, sm))


def _group_matrices():
    lane = np.arange(MXU_W)
    seg_d = (lane[:, None] // DIFF_HD == lane[None, :] // DIFF_HD)
    seg_m = (lane[:, None] // LANES == lane[None, :] // LANES)

    def swap(rotary, half):
        src = np.where(lane % (2 * half) < half, lane + half, lane - half)
        return (lane[:, None] == src[None, :]) & rotary[None, :]

    perm_d = swap(np.ones(MXU_W, bool), DIFF_HD // 4)
    in_head = lane % LANES
    perm_m = swap((in_head >= MLA_NOPE) & (in_head < MLA_QK), MLA_ROPE // 4)
    return tuple(jnp.asarray(m, dtype=BF16) for m in (seg_d, seg_m, perm_d, perm_m))


def _pad_heads(v):
    return jnp.pad(v, [(0, 0)] * (v.ndim - 1) + [(0, LANES - MLA_QK)])


def _prepare_weights(norm1_g, w_in, conv_dw, conv_b, conv_norm_g, w_conv_out, diff_q_norm, diff_k_norm,
                     diff_lambda, diff_subln, w_diff_out, mla_q_a_norm, mla_kv_a_norm, w_uq, w_ukv,
                     mla_q_norm, mla_k_norm, w_mla_out, w_out, norm2_g, w_up, w_down):
    row = lambda v: v[:, None, :]
    kpe_cols = w_in[:, :, N_MAIN:N_MAIN + MLA_ROPE].astype(BF16)
    zc = jnp.zeros_like(kpe_cols)
    w_in_t = jnp.swapaxes(w_in, 1, 2)
    wuq = _pad_heads(w_uq.reshape(DEPTH, MLA_Q_RANK, MLA_HEADS, MLA_QK)).reshape(DEPTH, MLA_Q_RANK, MLA_PAD)
    wukv4 = w_ukv.reshape(DEPTH, MLA_KV_RANK, MLA_HEADS, MLA_NOPE + MLA_V)
    wkn = jnp.pad(wukv4[..., :MLA_NOPE], ((0, 0), (0, 0), (0, 0), (0, LANES - MLA_NOPE)))
    wkn = wkn.reshape(DEPTH, MLA_KV_RANK, MLA_PAD)
    wv = wukv4[..., MLA_NOPE:].reshape(DEPTH, MLA_KV_RANK, MLA_W)
    return dict(
        norm1_g=row(norm1_g), norm2_g=row(norm2_g),
        w_in=w_in_t,
        wkpe=jnp.concatenate([kpe_cols, zc, kpe_cols, zc], axis=-1),
        wg=w_in_t[:, N_MAIN + MLA_ROPE:, :].astype(BF16),
        gq=row(jnp.tile(diff_q_norm, (1, DIFF_W // DIFF_HD))),
        gk=row(jnp.tile(diff_k_norm, (1, DIFF_W // DIFF_HD))),
        gcq=row(mla_q_a_norm), gckv=row(mla_kv_a_norm),
        wuq=wuq.astype(BF16), wukv=jnp.concatenate([wkn, wv], axis=-1).astype(BF16),
        gmq=row(jnp.tile(_pad_heads(mla_q_norm), (1, MLA_HEADS))),
        gmk=row(jnp.tile(_pad_heads(mla_k_norm), (1, MLA_HEADS))),
        conv_dw=jnp.pad(conv_dw, ((0, 0), (0, 1), (0, 0))), conv_b=row(conv_b), conv_g=row(conv_norm_g),
        lam=diff_lambda, subg=diff_subln[:, :, None],
        w_conv_out=w_conv_out.astype(BF16), w_diff_out=w_diff_out.astype(BF16),
        w_mla_out=w_mla_out.astype(BF16), w_out=w_out.astype(BF16),
        w_up=w_up.astype(BF16), w_down=w_down.astype(BF16),
    )


def kernel(x_prompt, x_sample, cache_diff_k, cache_diff_v, cache_mla_ckv, cache_mla_kpe, c, c_ctx, mod_w, mod_b, norm1_g, w_in, conv_dw, conv_b, conv_norm_g, w_conv_out, diff_q_norm, diff_k_norm, diff_lambda, diff_subln, w_diff_out, mla_q_a_norm, mla_kv_a_norm, w_uq, w_ukv, mla_q_norm, mla_k_norm, w_mla_out, w_out, norm2_g, w_up, w_down):
    batch, seq, _ = x_prompt.shape
    dec_batch, dec_seq, _ = x_sample.shape
    past = cache_diff_k.shape[2]
    assert seq == TOK_TILE and past == TOK_TILE and dec_seq % TOK_TILE == 0 and 1 + dec_batch <= 8

    cond_rows = jnp.concatenate([c_ctx[None], c, jnp.zeros((8 - 1 - dec_batch, D_MODEL), F32)], axis=0)
    mod = _modulation(cond_rows, mod_w, mod_b).reshape(DEPTH, 8, 6, D_MODEL)
    tabs = _rope_tables(dec_seq)

    W = _prepare_weights(norm1_g, w_in, conv_dw, conv_b, conv_norm_g, w_conv_out, diff_q_norm, diff_k_norm,
                         diff_lambda, diff_subln, w_diff_out, mla_q_a_norm, mla_kv_a_norm, w_uq, w_ukv,
                         mla_q_norm, mla_k_norm, w_mla_out, w_out, norm2_g, w_up, w_down)
    cache = (cache_diff_k, cache_diff_v, cache_mla_ckv,
             jnp.pad(cache_mla_kpe, ((0, 0), (0, 0), (0, 0), (MLA_NOPE, LANES - MLA_QK))))

    xp = x_prompt.reshape(batch * seq, D_MODEL)
    xs = x_sample.reshape(dec_batch * dec_seq, D_MODEL)
    new_state = None
    for l in range(DEPTH):
        lam_init = 0.8 - 0.6 * math.exp(-0.3 * l)

        u, qd, qm, kd, km, vdt, vmt, *new_state = _inproj(
            xp, mod[l], W, l, None, None, batch=batch, row0=0, states=True, prev_states=new_state)
        od, om = _attention(qd, qm, kd, km, vdt, vmt, W, l, batch=batch, lam_init=lam_init)
        xp = _merge_mlp(xp, mod[l], u, od, om, W, l, tiles_per_seq=1, row0=0)

        u, qd, qm, kd, km, vdt, vmt = _inproj(xs, mod[l], W, l, tabs, cache, batch=dec_batch, row0=1, states=False)
        od, om = _attention(qd, qm, kd, km, vdt, vmt, W, l, batch=dec_batch, lam_init=lam_init)
        xs = _merge_mlp(xs, mod[l], u, od, om, W, l, tiles_per_seq=dec_seq // TOK_TILE, row0=1)

    sk, sv, sckv, skpe = new_state
    return (xp.reshape(batch, seq, D_MODEL), xs.reshape(dec_batch, dec_seq, D_MODEL),
            sk.reshape(batch, DEPTH, seq, DIFF_HEADS, 2 * DIFF_HD),
            sv.reshape(batch, DEPTH, seq, DIFF_HEADS, 2 * DIFF_HD), sckv, skpe)
```

```python
import functools
import math

import numpy as np
import jax
import jax.numpy as jnp
from jax import lax
from jax.experimental import pallas as pl
from jax.experimental.pallas import tpu as pltpu

D_MODEL = 1024
DEPTH = 2
GRID_W = 64
ROPE_BASE = 10000.0
CONV_W = 512
CONV_K = 31
DIFF_HEADS = 4
DIFF_HD = 64
DIFF_W = DIFF_HEADS * 2 * DIFF_HD
MLA_HEADS = 8
MLA_NOPE = 64
MLA_ROPE = 32
MLA_QK = MLA_NOPE + MLA_ROPE
MLA_V = 64
MLA_Q_RANK = 384
MLA_KV_RANK = 256
MLA_W = MLA_HEADS * MLA_V
D_FF = 4 * D_MODEL
EPS = 1e-6

LANES = 128
SUBLANES = 8
MXU_W = 256
MLA_PAD = MLA_HEADS * LANES
TOK_TILE = 256
HALO = 16
N_MAIN = 2 * CONV_W + 3 * DIFF_W + MLA_Q_RANK + MLA_KV_RANK
VMEM_LIMIT = 56 * 1024 * 1024
LOG2E = math.log2(math.e)
F32 = jnp.float32
BF16 = jnp.bfloat16


def _silu(x):
    return x * jax.nn.sigmoid(x)


def _rms_full(x, g):
    ms = jnp.mean(x * x, axis=-1, keepdims=True)
    return x * lax.rsqrt(ms + EPS) * g


def _lane_tiles(x):
    return jnp.concatenate([x[:, t:t + MXU_W] for t in range(0, x.shape[-1], MXU_W)], axis=0)


def _from_lane_tiles(y, rows):
    return jnp.concatenate([y[r:r + rows] for r in range(0, y.shape[0], rows)], axis=-1)


def _group_rms(x, seg_ref, n, g):
    rows = x.shape[0]
    xt = _lane_tiles(x)
    ss = jnp.dot((xt * xt).astype(BF16), seg_ref[...], preferred_element_type=F32)
    return _from_lane_tiles(xt * lax.rsqrt(ss * (1.0 / n) + EPS), rows) * g


def _rope(x, perm_ref, cos, sin_signed):
    rows = x.shape[0]
    sw = jnp.dot(_lane_tiles(x).astype(BF16), perm_ref[...], preferred_element_type=F32)
    return x * cos + _from_lane_tiles(sw, rows) * sin_signed


def _tile_lanes(t, n):
    return jnp.concatenate([t] * n, axis=-1)


def _modulated_norm(x, g, shift, scale):
    return _rms_full(x, g) * (1.0 + scale) + shift


def _mod_kernel(s_ref, w_ref, b_ref, o_ref):
    s = _silu(s_ref[...]).astype(BF16)
    o_ref[...] = jnp.dot(s, w_ref[...].astype(BF16), preferred_element_type=F32) + b_ref[...]


def _modulation(cond_rows, mod_w, mod_b):
    tn = 1536
    n = 6 * D_MODEL
    return pl.pallas_call(
        _mod_kernel,
        grid=(DEPTH, n // tn),
        in_specs=[
            pl.BlockSpec((8, D_MODEL), lambda l, j: (0, 0)),
            pl.BlockSpec((None, D_MODEL, tn), lambda l, j: (l, 0, j)),
            pl.BlockSpec((None, 1, tn), lambda l, j: (l, 0, j)),
        ],
        out_specs=pl.BlockSpec((None, 8, tn), lambda l, j: (l, 0, j)),
        out_shape=jax.ShapeDtypeStruct((DEPTH, 8, n), F32),
        compiler_params=pltpu.CompilerParams(
            dimension_semantics=("parallel", "parallel"), vmem_limit_bytes=VMEM_LIMIT),
        name="modulation",
    )(cond_rows, mod_w, mod_b.reshape(DEPTH, 1, n))


def _mla_keys_values(ckvn, kpe_hi, wukv_ref):
    kv = jnp.dot(ckvn.astype(BF16), wukv_ref[...], preferred_element_type=F32)
    return kv[:, :MLA_PAD] + _tile_lanes(kpe_hi, MLA_HEADS), kv[:, MLA_PAD:]


def _inproj_kernel(*refs, rope, states, cache, n_prev, make_w16):
    it = iter(refs)
    x_ref, mod_ref, g1_ref, w_ref, wkpe_ref = next(it), next(it), next(it), next(it), next(it)
    gq_ref, gk_ref, gcq_ref, gckv_ref = next(it), next(it), next(it), next(it)
    wuq_ref, wukv_ref, gmq_ref, gmk_ref = next(it), next(it), next(it), next(it)
    segd_ref, segm_ref = next(it), next(it)
    if rope:
        permd_ref, permm_ref = next(it), next(it)
        cd_ref, sd_ref, cm_ref, sm_ref = next(it), next(it), next(it), next(it)
    if cache:
        ck_ref, cv_ref, cckv_ref, ckpe_ref = next(it), next(it), next(it), next(it)
    prev_refs = [next(it) for _ in range(4)] if n_prev else []
    u_ref, qd_ref, qm_ref, kd_ref, km_ref, vdt_ref, vmt_ref = (next(it) for _ in range(7))
    if states:
        state_refs = sk_ref, sv_ref, sckv_ref, skpe_ref = next(it), next(it), next(it), next(it)
    tm = x_ref.shape[0]

    if make_w16:
        w16_ref = next(it)
        first = functools.reduce(jnp.logical_and, [pl.program_id(a) == 0 for a in range(2 if cache else 1)])

        @pl.when(first)
        def _():
            for c0 in range(0, N_MAIN, LANES):
                w16_ref[:, c0:c0 + LANES] = w_ref[c0:c0 + LANES, :].T.astype(BF16)
    else:
        w16_ref = w_ref

    def tokens():
        mod = mod_ref[...]
        h = _modulated_norm(x_ref[...], g1_ref[...], mod[0:1], mod[1:2]).astype(BF16)
        proj = jnp.dot(h, w16_ref[...], preferred_element_type=F32)

        o = 0
        u_a = proj[:, o:o + CONV_W]; o += CONV_W
        u_g = proj[:, o:o + CONV_W]; o += CONV_W
        dq = proj[:, o:o + DIFF_W]; o += DIFF_W
        dk = proj[:, o:o + DIFF_W]; o += DIFF_W
        dv = proj[:, o:o + DIFF_W]; o += DIFF_W
        cq = proj[:, o:o + MLA_Q_RANK]; o += MLA_Q_RANK
        ckv = proj[:, o:o + MLA_KV_RANK]; o += MLA_KV_RANK
        kpe2 = jnp.dot(h, wkpe_ref[...], preferred_element_type=F32)

        cqn = _rms_full(cq, gcq_ref[...])
        ckvn = _rms_full(ckv, gckv_ref[...])
        q = _group_rms(dq, segd_ref, DIFF_HD, gq_ref[...])
        k = _group_rms(dk, segd_ref, DIFF_HD, gk_ref[...])
        qm = jnp.dot(cqn.astype(BF16), wuq_ref[...], preferred_element_type=F32)
        lane = lax.broadcasted_iota(jnp.int32, (1, LANES), 1)
        km, vm = _mla_keys_values(ckvn, jnp.where(lane >= MLA_NOPE, kpe2, 0.0), wukv_ref)

        u_ref[...] = u_a * jax.nn.sigmoid(u_g)
        vdt_ref[...] = dv.T.astype(BF16)
        if states:
            for dst, src in zip(state_refs, prev_refs):
                dst[0:n_prev] = src[...]
            for hd in range(DIFF_HEADS):
                sl = slice(hd * LANES, (hd + 1) * LANES)
                sk_ref[n_prev, pl.ds(hd, tm, stride=DIFF_HEADS), :] = k[:, sl]
                sv_ref[n_prev, pl.ds(hd, tm, stride=DIFF_HEADS), :] = dv[:, sl]
            sckv_ref[n_prev] = ckvn
            skpe_ref[n_prev] = kpe2[:, :MLA_ROPE]

        qm = _group_rms(qm, segm_ref, MLA_QK, gmq_ref[...])
        km = _group_rms(km, segm_ref, MLA_QK, gmk_ref[...])
        if rope:
            cd = _tile_lanes(cd_ref[...], DIFF_W // LANES)
            sd = _tile_lanes(sd_ref[...], DIFF_W // LANES)
            q = _rope(q, permd_ref, cd, sd)
            k = _rope(k, permd_ref, cd, sd)
        qd_ref[...] = (q * (DIFF_HD ** -0.5 * LOG2E)).astype(BF16)
        kd_ref[...] = k.astype(BF16)
        vmt_ref[...] = vm.T.astype(BF16)
        if rope:
            cm = _tile_lanes(cm_ref[...], MLA_HEADS)
            sm = _tile_lanes(sm_ref[...], MLA_HEADS)
            qm = _rope(qm, permm_ref, cm, sm)
            km = _rope(km, permm_ref, cm, sm)
        qm_ref[...] = (qm * (MLA_QK ** -0.5 * LOG2E)).astype(BF16)
        km_ref[...] = km.astype(BF16)

    def cached_context():
        heads = lambda ref: jnp.concatenate([ref[:, hd, :] for hd in range(DIFF_HEADS)], axis=-1)
        kd_ref[...] = heads(ck_ref).astype(BF16)
        vdt_ref[...] = heads(cv_ref).T.astype(BF16)
        km, vm = _mla_keys_values(cckv_ref[...], ckpe_ref[...], wukv_ref)
        km_ref[...] = _group_rms(km, segm_ref, MLA_QK, gmk_ref[...]).astype(BF16)
        vmt_ref[...] = vm.T.astype(BF16)

    if cache:
        j = pl.program_id(1)
        pl.when(j == 0)(cached_context)
        pl.when(j > 0)(tokens)
    else:
        tokens()


def _layer_spec(w, l, grid_rank, **kw):
    index = (lambda i: (l, 0, 0)) if grid_rank == 1 else (lambda b, j: (l, 0, 0))
    return pl.BlockSpec((None,) + w.shape[1:], index, **kw)


def _inproj(x, mod_l, W, l, rope_tabs, cache, *, batch, row0, states, prev_states=None, w16=None):
    t = x.shape[0]
    tm = TOK_TILE
    n = t // batch // tm
    rope = rope_tabs is not None
    has_cache = cache is not None
    nk = n + 1 if has_cache else n
    if has_cache:
        grid = (batch, nk)
        tok = lambda b, j: (b * n + jnp.maximum(j - 1, 0), 0)
        key = lambda b, j: (b * nk + j, 0)
        keyt = lambda b, j: (b, 0, j)
        const = lambda b, j: (0, 0)
        modrow = lambda b, j: (row0 + b, 0, 0)
        tab = lambda b, j: (jnp.maximum(j - 1, 0), 0)
    else:
        grid = (batch * n,)
        tok = lambda i: (i, 0)
        key = tok
        keyt = lambda i: (i // n, 0, i % n)
        const = lambda i: (0, 0)
        modrow = lambda i: (row0 + (i // n if row0 else 0), 0, 0)
        tab = lambda i: (i % n, 0)
    names = ("norm1_g", "w_in", "wkpe", "gq", "gk", "gcq", "gckv", "wuq", "wukv", "gmq", "gmk")
    in_specs = [pl.BlockSpec((tm, D_MODEL), tok), pl.BlockSpec((None, 6, D_MODEL), modrow)]
    in_specs += [_layer_spec(W[k], l, len(grid)) for k in names]
    make_w16 = w16 is None
    if make_w16:
        w_spec = pl.BlockSpec((None, N_MAIN, D_MODEL), in_specs[2].index_map, pipeline_mode=pl.Buffered(1))
    else:
        w_spec = pl.BlockSpec((D_MODEL, N_MAIN), const, pipeline_mode=pl.Buffered(1))
    in_specs[2 + names.index("w_in")] = w_spec
    in_specs += [pl.BlockSpec((MXU_W, MXU_W), const)] * 2
    seg_d, seg_m, perm_d, perm_m = _group_matrices()
    args = [x, mod_l] + [W[k] if (k != "w_in" or make_w16) else w16 for k in names] + [seg_d, seg_m]
    if rope:
        in_specs += [pl.BlockSpec((MXU_W, MXU_W), const)] * 2 + [pl.BlockSpec((tm, LANES), tab)] * 4
        args += [perm_d, perm_m] + list(rope_tabs)
    if has_cache:
        in_specs += [pl.BlockSpec((None, None) + a.shape[2:], lambda b, j, r=a.ndim - 2: (b, l) + (0,) * r)
                     for a in cache]
        args += list(cache)
    sk = nk * tm
    out_shape = [
        jax.ShapeDtypeStruct((t, CONV_W), F32),
        jax.ShapeDtypeStruct((t, DIFF_W), BF16),
        jax.ShapeDtypeStruct((t, MLA_PAD), BF16),
        jax.ShapeDtypeStruct((batch * sk, DIFF_W), BF16),
        jax.ShapeDtypeStruct((batch * sk, MLA_PAD), BF16),
        jax.ShapeDtypeStruct((batch, DIFF_W, sk), BF16),
        jax.ShapeDtypeStruct((batch, MLA_W, sk), BF16),
    ]
    out_specs = [
        pl.BlockSpec((tm, CONV_W), tok),
        pl.BlockSpec((tm, DIFF_W), tok),
        pl.BlockSpec((tm, MLA_PAD), tok),
        pl.BlockSpec((tm, DIFF_W), key),
        pl.BlockSpec((tm, MLA_PAD), key),
        pl.BlockSpec((None, DIFF_W, tm), keyt),
        pl.BlockSpec((None, MLA_W, tm), keyt),
    ]
    n_prev = 0
    if states:
        assert n == 1
        n_prev = l
        st_shapes = ((tm * DIFF_HEADS, LANES), (tm * DIFF_HEADS, LANES), (tm, MLA_KV_RANK), (tm, MLA_ROPE))
        seq_block = lambda layers, s: pl.BlockSpec((None, layers) + s, lambda i: (i, 0, 0, 0))
        out_shape += [jax.ShapeDtypeStruct((batch, l + 1) + s, F32) for s in st_shapes]
        out_specs += [seq_block(l + 1, s) for s in st_shapes]
        if n_prev:
            in_specs += [seq_block(l, s) for s in st_shapes]
            args += list(prev_states)
    if make_w16:
        out_shape.append(jax.ShapeDtypeStruct((D_MODEL, N_MAIN), BF16))
        out_specs.append(pl.BlockSpec((D_MODEL, N_MAIN), const))
    return pl.pallas_call(
        functools.partial(_inproj_kernel, rope=rope, states=states, cache=has_cache, n_prev=n_prev,
                          make_w16=make_w16),
        grid=grid,
        in_specs=in_specs,
        out_specs=out_specs,
        out_shape=out_shape,
        compiler_params=pltpu.CompilerParams(
            dimension_semantics=("arbitrary",) * len(grid), vmem_limit_bytes=VMEM_LIMIT),
        name="inproj_latent" if has_cache else "inproj_context",
    )(*args)


def _fold_rows(x, op):
    n = x.shape[0]
    while n % (2 * SUBLANES) == 0 and n > 4 * SUBLANES:
        n //= 2
        x = op(x[:n], x[n:])
    return x


KEY_CHUNK = 256
SCORE_LEAD_CHUNKED = 1
SCORE_LEAD_SINGLE = 2
VALUE_LAG = 1


def _attn_kernel(qd_ref, qm_ref, kd_ref, km_ref, vdt_ref, vmt_ref, lam_ref, sub_ref, od_ref, om_ref, s_ref, p_ref,
                 *, lam_init):
    lp = lam_ref[...]
    lam = (jnp.exp(jnp.sum(lp[0:1] * lp[1:2], axis=-1, keepdims=True))
           - jnp.exp(jnp.sum(lp[2:3] * lp[3:4], axis=-1, keepdims=True)) + lam_init)
    lo = lax.broadcasted_iota(jnp.int32, (1, LANES), 1) < DIFF_HD
    top = lax.broadcasted_iota(jnp.int32, (LANES, 1), 0) < MLA_V
    subg = sub_ref[...]
    sk = kd_ref.shape[0]
    kc = min(KEY_CHUNK, sk)
    n_chunks = sk // kc

    jobs = []
    for h in range(DIFF_HEADS):
        sl = slice(h * LANES, (h + 1) * LANES)
        jobs += [(kd_ref, sl, qd_ref, 0, vdt_ref, sl), (kd_ref, sl, qd_ref, 1, vdt_ref, sl)]
    for h in range(MLA_HEADS):
        hs = slice(h * LANES, (h + 1) * LANES)
        pair = slice((h // 2) * LANES, (h // 2 + 1) * LANES)
        jobs.append((km_ref, hs, qm_ref, None, vmt_ref, pair))
    n = len(jobs)

    def query(i):
        _, sl, q_ref, half, _, _ = jobs[i]
        if half is None:
            return q_ref[:, sl]
        q = q_ref[:, sl].astype(F32)
        return (jnp.where(lo, q, 0.0) if half == 0 else jnp.where(lo, 0.0, q)).astype(BF16)

    qs, mcol, mfin, lcol, acc = {}, {}, {}, {}, {}
    outs = [None] * n
    slots = s_ref.shape[0]
    lead = slots - 1
    pslots = VALUE_LAG + 1
    for i in range(-lead, n + VALUE_LAG):
        a, b, c = i + lead, i, i - VALUE_LAG
        if 0 <= a < n:
            qs[a] = query(a)
        if 0 <= b < n:
            mfin[b] = jnp.max(mcol.pop(b), axis=0, keepdims=True)
        for r in range(n_chunks):
            rows = slice(r * kc, (r + 1) * kc)
            if 0 <= a < n:
                k_ref, ksl = jobs[a][0], jobs[a][1]
                s = lax.dot_general(k_ref[rows, ksl], qs[a], (((1,), (1,)), ((), ())),
                                    preferred_element_type=F32)
                s_ref[a % slots, rows, :] = s
                f = _fold_rows(s, jnp.maximum)
                mcol[a] = f if r == 0 else jnp.maximum(mcol[a], f)
            if 0 <= b < n:
                p = jnp.exp2(s_ref[b % slots, rows, :] - mfin[b])
                f = _fold_rows(p, jnp.add)
                lcol[b] = f if r == 0 else lcol[b] + f
                p_ref[b % pslots, rows, :] = p.astype(BF16)
            if 0 <= c < n:
                vt_ref, vrows = jobs[c][4], jobs[c][5]
                d = jnp.dot(vt_ref[vrows, rows], p_ref[c % pslots, rows, :], preferred_element_type=F32)
                acc[c] = d if r == 0 else acc[c] + d
        if 0 <= c < n:
            outs[c] = acc.pop(c) / jnp.sum(lcol.pop(c), axis=0, keepdims=True)

    for h in range(DIFF_HEADS):
        sl = slice(h * LANES, (h + 1) * LANES)
        a = outs[2 * h] - lam * outs[2 * h + 1]
        ms = jnp.mean(a * a, axis=0, keepdims=True)
        od = a * lax.rsqrt(ms + EPS) * subg * (1.0 - lam_init)
        od_ref[:, sl] = od.T.astype(BF16)
    base = 2 * DIFF_HEADS
    for hp in range(MLA_HEADS // 2):
        sl = slice(hp * LANES, (hp + 1) * LANES)
        om_ref[:, sl] = jnp.where(top, outs[base + 2 * hp], outs[base + 2 * hp + 1]).T.astype(BF16)


def _attention(qd, qm, kd, km, vdt, vmt, W, l, *, batch, lam_init):
    sq = qd.shape[0] // batch
    sk = kd.shape[0] // batch
    tq = TOK_TILE
    nq = sq // tq
    qrow = lambda b, i: (b * nq + i, 0)
    krow = lambda b, i: (b, 0)
    kcol = lambda b, i: (b, 0, 0)
    lead = SCORE_LEAD_CHUNKED if sk > KEY_CHUNK else SCORE_LEAD_SINGLE
    return pl.pallas_call(
        functools.partial(_attn_kernel, lam_init=lam_init),
        grid=(batch, nq),
        in_specs=[
            pl.BlockSpec((tq, DIFF_W), qrow),
            pl.BlockSpec((tq, MLA_PAD), qrow),
            pl.BlockSpec((sk, DIFF_W), krow),
            pl.BlockSpec((sk, MLA_PAD), krow),
            pl.BlockSpec((None, DIFF_W, sk), kcol),
            pl.BlockSpec((None, MLA_W, sk), kcol),
            _layer_spec(W["lam"], l, 2),
            _layer_spec(W["subg"], l, 2),
        ],
        out_specs=[pl.BlockSpec((tq, DIFF_W), qrow), pl.BlockSpec((tq, MLA_W), qrow)],
        out_shape=[jax.ShapeDtypeStruct((batch * sq, DIFF_W), BF16),
                   jax.ShapeDtypeStruct((batch * sq, MLA_W), BF16)],
        scratch_shapes=[pltpu.VMEM((lead + 1, sk, tq), F32), pltpu.VMEM((VALUE_LAG + 1, sk, tq), BF16)],
        compiler_params=pltpu.CompilerParams(
            dimension_semantics=("parallel", "arbitrary"), vmem_limit_bytes=VMEM_LIMIT),
        name="attention",
    )(qd, qm, kd, km, vdt, vmt, W["lam"], W["subg"])


CONV_SPAN = (CONV_K + SUBLANES - 1) // SUBLANES * SUBLANES - SUBLANES


MERGE_PARTS = 1


def _depthwise_conv(buf_ref, phase_ref, dw_ref, part, rows):
    base = HALO - CONV_K // 2
    row0 = part * rows
    acc = None
    for r in range(SUBLANES):
        offs = [o for o in range(base, base + CONV_K) if o % SUBLANES == r]
        phase_ref[part, r] = buf_ref[row0 + r:row0 + r + rows + CONV_SPAN, :]
        for o in offs:
            term = phase_ref[part, r, o - r:o - r + rows, :] * dw_ref[o - base:o - base + 1, :]
            acc = term if acc is None else acc + term
    return acc


def _merge_mlp_kernel(*refs, tiles_per_seq):
    it = iter(refs)
    x_ref, mod_ref, g1_ref, g2_ref, u_ref = (next(it) for _ in range(5))
    if tiles_per_seq > 1:
        up_ref, un_ref = next(it), next(it)
    dw_ref, cb_ref, cg_ref, od_ref, om_ref = (next(it) for _ in range(5))
    wg_ref, wc_ref, wd_ref, wm_ref, wo_ref, wup_ref, wdn_ref, y_ref, buf_ref, phase_ref = (next(it) for _ in range(10))
    tm = x_ref.shape[0]
    mod = mod_ref[...]

    zeros = jnp.zeros((HALO, CONV_W), F32)
    if tiles_per_seq > 1:
        j = pl.program_id(0) % tiles_per_seq
        prev = jnp.where(j == 0, zeros, up_ref[...])
        nxt = jnp.where(j == tiles_per_seq - 1, zeros, un_ref[...])
    else:
        prev, nxt = zeros, zeros
    buf_ref[0:HALO, :] = prev
    buf_ref[HALO:HALO + tm, :] = u_ref[...]
    buf_ref[HALO + tm:HALO + tm + HALO, :] = nxt

    rows = tm // MERGE_PARTS
    for part in range(MERGE_PARTS):
        rs = slice(part * rows, (part + 1) * rows)
        x = x_ref[rs, :]
        h = _modulated_norm(x, g1_ref[...], mod[0:1], mod[1:2])
        gates = jax.nn.sigmoid(lax.dot_general(h.astype(BF16), wg_ref[...], (((1,), (1,)), ((), ())),
                                               preferred_element_type=F32))
        o_diff = jnp.dot(od_ref[rs, :], wd_ref[...], preferred_element_type=F32)
        o_mla = jnp.dot(om_ref[rs, :], wm_ref[...], preferred_element_type=F32)

        conv = _depthwise_conv(buf_ref, phase_ref, dw_ref, part, rows) + cb_ref[...]
        c = _silu(_rms_full(conv, cg_ref[...])).astype(BF16)
        o_conv = jnp.dot(c, wc_ref[...], preferred_element_type=F32)
        merged = (gates[:, 0:D_MODEL] * o_conv + gates[:, D_MODEL:2 * D_MODEL] * o_diff
                  + gates[:, 2 * D_MODEL:3 * D_MODEL] * o_mla)
        x = x + mod[2:3] * jnp.dot(merged.astype(BF16), wo_ref[...], preferred_element_type=F32)
        h2 = _modulated_norm(x, g2_ref[...], mod[3:4], mod[4:5])
        up = jnp.dot(h2.astype(BF16), wup_ref[...], preferred_element_type=F32)
        act = jnp.square(jnp.maximum(up, 0.0)).astype(BF16)
        y_ref[rs, :] = x + mod[5:6] * jnp.dot(act, wdn_ref[...], preferred_element_type=F32)


def _merge_mlp(x, mod_l, u, od, om, W, l, *, tiles_per_seq, row0):
    t = x.shape[0]
    tm = TOK_TILE
    nb = t // HALO
    per = tm // HALO
    row = lambda i: (i, 0)

    in_specs = [
        pl.BlockSpec((tm, D_MODEL), row),
        pl.BlockSpec((None, 6, D_MODEL), lambda i: (row0 + (i // tiles_per_seq if row0 else 0), 0, 0)),
        _layer_spec(W["norm1_g"], l, 1),
        _layer_spec(W["norm2_g"], l, 1),
        pl.BlockSpec((tm, CONV_W), row),
    ]
    args = [x, mod_l, W["norm1_g"], W["norm2_g"], u]
    if tiles_per_seq > 1:
        in_specs += [
            pl.BlockSpec((HALO, CONV_W), lambda i: (jnp.maximum(i * per - 1, 0), 0)),
            pl.BlockSpec((HALO, CONV_W), lambda i: (jnp.minimum((i + 1) * per, nb - 1), 0)),
        ]
        args += [u, u]
    big = ("wg", "w_conv_out", "w_diff_out", "w_mla_out", "w_out", "w_up", "w_down")
    in_specs += [_layer_spec(W[k], l, 1) for k in ("conv_dw", "conv_b", "conv_g")]
    in_specs += [pl.BlockSpec((tm, DIFF_W), row), pl.BlockSpec((tm, MLA_W), row)]
    in_specs += [_layer_spec(W[k], l, 1, pipeline_mode=pl.Buffered(1)) for k in big]
    args += [W["conv_dw"], W["conv_b"], W["conv_g"], od, om] + [W[k] for k in big]
    return pl.pallas_call(
        functools.partial(_merge_mlp_kernel, tiles_per_seq=tiles_per_seq),
        grid=(t // tm,),
        in_specs=in_specs,
        out_specs=pl.BlockSpec((tm, D_MODEL), row),
        out_shape=jax.ShapeDtypeStruct((t, D_MODEL), F32),
        scratch_shapes=[pltpu.VMEM((tm + 2 * HALO, CONV_W), F32),
                        pltpu.VMEM((MERGE_PARTS, SUBLANES, tm // MERGE_PARTS + CONV_SPAN, CONV_W), F32)],
        compiler_params=pltpu.CompilerParams(
            dimension_semantics=("parallel",), vmem_limit_bytes=VMEM_LIMIT),
        name="merge_mlp",
    )(*args)


def _rope_tables(seq_len):
    rows = seq_len // GRID_W
    row = np.repeat(np.arange(rows, dtype=np.float64), GRID_W)
    col = np.tile(np.arange(GRID_W, dtype=np.float64), rows)

    def tables(rot_dim):
        half = rot_dim // 2
        inv = ROPE_BASE ** (-np.arange(0, half, 2, dtype=np.float64) / half)
        a0 = row[:, None] * inv
        a1 = col[:, None] * inv
        cos = np.concatenate([np.cos(a0), np.cos(a0), np.cos(a1), np.cos(a1)], axis=1)
        sin = np.concatenate([-np.sin(a0), np.sin(a0), -np.sin(a1), np.sin(a1)], axis=1)
        return cos, sin

    cd, sd = tables(DIFF_HD)
    cd, sd = np.tile(cd, (1, LANES // DIFF_HD)), np.tile(sd, (1, LANES // DIFF_HD))
    cm, sm = tables(MLA_ROPE)
    pad = LANES - MLA_QK
    cm = np.concatenate([np.ones((seq_len, MLA_NOPE)), cm, np.ones((seq_len, pad))], axis=1)
    sm = np.concatenate([np.zeros((seq_len, MLA_NOPE)), sm, np.zeros((seq_len, pad))], axis=1)
    return tuple(jnp.asarray(t, dtype=F32) for t in (cd, sd, cm, sm))


def _group_matrices():
    lane = np.arange(MXU_W)
    seg_d = (lane[:, None] // DIFF_HD == lane[None, :] // DIFF_HD)
    seg_m = (lane[:, None] // LANES == lane[None, :] // LANES)

    def swap(rotary, half):
        src = np.where(lane % (2 * half) < half, lane + half, lane - half)
        return (lane[:, None] == src[None, :]) & rotary[None, :]

    perm_d = swap(np.ones(MXU_W, bool), DIFF_HD // 4)
    in_head = lane % LANES
    perm_m = swap((in_head >= MLA_NOPE) & (in_head < MLA_QK), MLA_ROPE // 4)
    return tuple(jnp.asarray(m, dtype=BF16) for m in (seg_d, seg_m, perm_d, perm_m))


def _pad_heads(v):
    return jnp.pad(v, [(0, 0)] * (v.ndim - 1) + [(0, LANES - MLA_QK)])


def _prepare_weights(norm1_g, w_in, conv_dw, conv_b, conv_norm_g, w_conv_out, diff_q_norm, diff_k_norm,
                     diff_lambda, diff_subln, w_diff_out, mla_q_a_norm, mla_kv_a_norm, w_uq, w_ukv,
                     mla_q_norm, mla_k_norm, w_mla_out, w_out, norm2_g, w_up, w_down):
    row = lambda v: v[:, None, :]
    kpe_cols = w_in[:, :, N_MAIN:N_MAIN + MLA_ROPE].astype(BF16)
    zc = jnp.zeros_like(kpe_cols)
    w_in_t = jnp.swapaxes(w_in, 1, 2)
    wuq = _pad_heads(w_uq.reshape(DEPTH, MLA_Q_RANK, MLA_HEADS, MLA_QK)).reshape(DEPTH, MLA_Q_RANK, MLA_PAD)
    wukv4 = w_ukv.reshape(DEPTH, MLA_KV_RANK, MLA_HEADS, MLA_NOPE + MLA_V)
    wkn = jnp.pad(wukv4[..., :MLA_NOPE], ((0, 0), (0, 0), (0, 0), (0, LANES - MLA_NOPE)))
    wkn = wkn.reshape(DEPTH, MLA_KV_RANK, MLA_PAD)
    wv = wukv4[..., MLA_NOPE:].reshape(DEPTH, MLA_KV_RANK, MLA_W)
    return dict(
        norm1_g=row(norm1_g), norm2_g=row(norm2_g),
        w_in=w_in_t,
        wkpe=jnp.concatenate([kpe_cols, zc, kpe_cols, zc], axis=-1),
        wg=w_in_t[:, N_MAIN + MLA_ROPE:, :].astype(BF16),
        gq=row(jnp.tile(diff_q_norm, (1, DIFF_W // DIFF_HD))),
        gk=row(jnp.tile(diff_k_norm, (1, DIFF_W // DIFF_HD))),
        gcq=row(mla_q_a_norm), gckv=row(mla_kv_a_norm),
        wuq=wuq.astype(BF16), wukv=jnp.concatenate([wkn, wv], axis=-1).astype(BF16),
        gmq=row(jnp.tile(_pad_heads(mla_q_norm), (1, MLA_HEADS))),
        gmk=row(jnp.tile(_pad_heads(mla_k_norm), (1, MLA_HEADS))),
        conv_dw=jnp.pad(conv_dw, ((0, 0), (0, 1), (0, 0))), conv_b=row(conv_b), conv_g=row(conv_norm_g),
        lam=diff_lambda, subg=diff_subln[:, :, None],
        w_conv_out=w_conv_out.astype(BF16), w_diff_out=w_diff_out.astype(BF16),
        w_mla_out=w_mla_out.astype(BF16), w_out=w_out.astype(BF16),
        w_up=w_up.astype(BF16), w_down=w_down.astype(BF16),
    )


def kernel(x_prompt, x_sample, cache_diff_k, cache_diff_v, cache_mla_ckv, cache_mla_kpe, c, c_ctx, mod_w, mod_b, norm1_g, w_in, conv_dw, conv_b, conv_norm_g, w_conv_out, diff_q_norm, diff_k_norm, diff_lambda, diff_subln, w_diff_out, mla_q_a_norm, mla_kv_a_norm, w_uq, w_ukv, mla_q_norm, mla_k_norm, w_mla_out, w_out, norm2_g, w_up, w_down):
    batch, seq, _ = x_prompt.shape
    dec_batch, dec_seq, _ = x_sample.shape
    past = cache_diff_k.shape[2]
    assert seq == TOK_TILE and past == TOK_TILE and dec_seq % TOK_TILE == 0 and 1 + dec_batch <= 8

    cond_rows = jnp.concatenate([c_ctx[None], c, jnp.zeros((8 - 1 - dec_batch, D_MODEL), F32)], axis=0)
    mod = _modulation(cond_rows, mod_w, mod_b).reshape(DEPTH, 8, 6, D_MODEL)
    tabs = _rope_tables(dec_seq)

    W = _prepare_weights(norm1_g, w_in, conv_dw, conv_b, conv_norm_g, w_conv_out, diff_q_norm, diff_k_norm,
                         diff_lambda, diff_subln, w_diff_out, mla_q_a_norm, mla_kv_a_norm, w_uq, w_ukv,
                         mla_q_norm, mla_k_norm, w_mla_out, w_out, norm2_g, w_up, w_down)
    cache = (cache_diff_k, cache_diff_v, cache_mla_ckv,
             jnp.pad(cache_mla_kpe, ((0, 0), (0, 0), (0, 0), (MLA_NOPE, LANES - MLA_QK))))

    xp = x_prompt.reshape(batch * seq, D_MODEL)
    xs = x_sample.reshape(dec_batch * dec_seq, D_MODEL)
    new_state = None
    for l in range(DEPTH):
        lam_init = 0.8 - 0.6 * math.exp(-0.3 * l)

        u, qd, qm, kd, km, vdt, vmt, *new_state, w16 = _inproj(
            xp, mod[l], W, l, None, None, batch=batch, row0=0, states=True, prev_states=new_state)
        od, om = _attention(qd, qm, kd, km, vdt, vmt, W, l, batch=batch, lam_init=lam_init)
        xp = _merge_mlp(xp, mod[l], u, od, om, W, l, tiles_per_seq=1, row0=0)

        u, qd, qm, kd, km, vdt, vmt = _inproj(xs, mod[l], W, l, tabs, cache, batch=dec_batch, row0=1, states=False,
                                              w16=w16)
        od, om = _attention(qd, qm, kd, km, vdt, vmt, W, l, batch=dec_batch, lam_init=lam_init)
        xs = _merge_mlp(xs, mod[l], u, od, om, W, l, tiles_per_seq=dec_seq // TOK_TILE, row0=1)

    sk, sv, sckv, skpe = new_state
    return (xp.reshape(batch, seq, D_MODEL), xs.reshape(dec_batch, dec_seq, D_MODEL),
            sk.reshape(batch, DEPTH, seq, DIFF_HEADS, 2 * DIFF_HD),
            sv.reshape(batch, DEPTH, seq, DIFF_HEADS, 2 * DIFF_HD), sckv, skpe)
```

```python
import functools
import math

import numpy as np
import jax
import jax.numpy as jnp
from jax import lax
from jax.experimental import pallas as pl
from jax.experimental.pallas import tpu as pltpu

D_MODEL = 1024
DEPTH = 2
GRID_W = 64
ROPE_BASE = 10000.0
CONV_W = 512
CONV_K = 31
DIFF_HEADS = 4
DIFF_HD = 64
DIFF_W = DIFF_HEADS * 2 * DIFF_HD
MLA_HEADS = 8
MLA_NOPE = 64
MLA_ROPE = 32
MLA_QK = MLA_NOPE + MLA_ROPE
MLA_V = 64
MLA_Q_RANK = 384
MLA_KV_RANK = 256
MLA_W = MLA_HEADS * MLA_V
D_FF = 4 * D_MODEL
EPS = 1e-6

LANES = 128
SUBLANES = 8
MXU_W = 256
MLA_PAD = MLA_HEADS * LANES
TOK_TILE = 256
HALO = 16
N_MAIN = 2 * CONV_W + 3 * DIFF_W + MLA_Q_RANK + MLA_KV_RANK
VMEM_LIMIT = 56 * 1024 * 1024
LOG2E = math.log2(math.e)
F32 = jnp.float32
BF16 = jnp.bfloat16


def _silu(x):
    return x * jax.nn.sigmoid(x)


def _rms_full(x, g):
    ms = jnp.mean(x * x, axis=-1, keepdims=True)
    return x * lax.rsqrt(ms + EPS) * g


def _lane_tiles(x):
    return jnp.concatenate([x[:, t:t + MXU_W] for t in range(0, x.shape[-1], MXU_W)], axis=0)


def _from_lane_tiles(y, rows):
    return jnp.concatenate([y[r:r + rows] for r in range(0, y.shape[0], rows)], axis=-1)


def _group_rms(x, seg_ref, n, g):
    rows = x.shape[0]
    xt = _lane_tiles(x)
    ss = jnp.dot((xt * xt).astype(BF16), seg_ref[...], preferred_element_type=F32)
    return _from_lane_tiles(xt * lax.rsqrt(ss * (1.0 / n) + EPS), rows) * g


def _rope(x, perm_ref, cos, sin_signed):
    rows = x.shape[0]
    sw = jnp.dot(_lane_tiles(x).astype(BF16), perm_ref[...], preferred_element_type=F32)
    return x * cos + _from_lane_tiles(sw, rows) * sin_signed


def _tile_lanes(t, n):
    return jnp.concatenate([t] * n, axis=-1)


def _modulated_norm(x, g, shift, scale):
    return _rms_full(x, g) * (1.0 + scale) + shift


def _mod_kernel(s_ref, w_ref, b_ref, o_ref):
    s = _silu(s_ref[...]).astype(BF16)
    o_ref[...] = jnp.dot(s, w_ref[...].astype(BF16), preferred_element_type=F32) + b_ref[...]


def _modulation(cond_rows, mod_w, mod_b):
    tn = 1536
    n = 6 * D_MODEL
    return pl.pallas_call(
        _mod_kernel,
        grid=(DEPTH, n // tn),
        in_specs=[
            pl.BlockSpec((8, D_MODEL), lambda l, j: (0, 0)),
            pl.BlockSpec((None, D_MODEL, tn), lambda l, j: (l, 0, j)),
            pl.BlockSpec((None, 1, tn), lambda l, j: (l, 0, j)),
        ],
        out_specs=pl.BlockSpec((None, 8, tn), lambda l, j: (l, 0, j)),
        out_shape=jax.ShapeDtypeStruct((DEPTH, 8, n), F32),
        compiler_params=pltpu.CompilerParams(
            dimension_semantics=("parallel", "parallel"), vmem_limit_bytes=VMEM_LIMIT),
        name="modulation",
    )(cond_rows, mod_w, mod_b.reshape(DEPTH, 1, n))


def _mla_keys_values(ckvn, kpe_hi, wukv_ref):
    kv = jnp.dot(ckvn.astype(BF16), wukv_ref[...], preferred_element_type=F32)
    return kv[:, :MLA_PAD] + _tile_lanes(kpe_hi, MLA_HEADS), kv[:, MLA_PAD:]


def _inproj_kernel(*refs, rope, states, cache, n_prev):
    it = iter(refs)
    x_ref, mod_ref, g1_ref, w_ref, wkpe_ref = next(it), next(it), next(it), next(it), next(it)
    gq_ref, gk_ref, gcq_ref, gckv_ref = next(it), next(it), next(it), next(it)
    wuq_ref, wukv_ref, gmq_ref, gmk_ref = next(it), next(it), next(it), next(it)
    segd_ref, segm_ref = next(it), next(it)
    if rope:
        permd_ref, permm_ref = next(it), next(it)
        cd_ref, sd_ref, cm_ref, sm_ref = next(it), next(it), next(it), next(it)
    if cache:
        ck_ref, cv_ref, cckv_ref, ckpe_ref = next(it), next(it), next(it), next(it)
    prev_refs = [next(it) for _ in range(4)] if n_prev else []
    u_ref, qd_ref, qm_ref, kd_ref, km_ref, vdt_ref, vmt_ref = (next(it) for _ in range(7))
    if states:
        state_refs = sk_ref, sv_ref, sckv_ref, skpe_ref = next(it), next(it), next(it), next(it)
    w16_ref = next(it)
    tm = x_ref.shape[0]

    first = functools.reduce(jnp.logical_and, [pl.program_id(a) == 0 for a in range(2 if cache else 1)])

    @pl.when(first)
    def _():
        for c0 in range(0, N_MAIN, LANES):
            w16_ref[:, c0:c0 + LANES] = w_ref[c0:c0 + LANES, :].T.astype(BF16)

    def tokens():
        mod = mod_ref[...]
        h = _modulated_norm(x_ref[...], g1_ref[...], mod[0:1], mod[1:2]).astype(BF16)
        proj = jnp.dot(h, w16_ref[...], preferred_element_type=F32)

        o = 0
        u_a = proj[:, o:o + CONV_W]; o += CONV_W
        u_g = proj[:, o:o + CONV_W]; o += CONV_W
        dq = proj[:, o:o + DIFF_W]; o += DIFF_W
        dk = proj[:, o:o + DIFF_W]; o += DIFF_W
        dv = proj[:, o:o + DIFF_W]; o += DIFF_W
        cq = proj[:, o:o + MLA_Q_RANK]; o += MLA_Q_RANK
        ckv = proj[:, o:o + MLA_KV_RANK]; o += MLA_KV_RANK
        kpe2 = jnp.dot(h, wkpe_ref[...], preferred_element_type=F32)

        cqn = _rms_full(cq, gcq_ref[...])
        ckvn = _rms_full(ckv, gckv_ref[...])
        q = _group_rms(dq, segd_ref, DIFF_HD, gq_ref[...])
        k = _group_rms(dk, segd_ref, DIFF_HD, gk_ref[...])
        qm = jnp.dot(cqn.astype(BF16), wuq_ref[...], preferred_element_type=F32)
        lane = lax.broadcasted_iota(jnp.int32, (1, LANES), 1)
        km, vm = _mla_keys_values(ckvn, jnp.where(lane >= MLA_NOPE, kpe2, 0.0), wukv_ref)

        u_ref[...] = u_a * jax.nn.sigmoid(u_g)
        vdt_ref[...] = dv.T.astype(BF16)
        if states:
            for dst, src in zip(state_refs, prev_refs):
                dst[0:n_prev] = src[...]
            for hd in range(DIFF_HEADS):
                sl = slice(hd * LANES, (hd + 1) * LANES)
                sk_ref[n_prev, pl.ds(hd, tm, stride=DIFF_HEADS), :] = k[:, sl]
                sv_ref[n_prev, pl.ds(hd, tm, stride=DIFF_HEADS), :] = dv[:, sl]
            sckv_ref[n_prev] = ckvn
            skpe_ref[n_prev] = kpe2[:, :MLA_ROPE]

        qm = _group_rms(qm, segm_ref, MLA_QK, gmq_ref[...])
        km = _group_rms(km, segm_ref, MLA_QK, gmk_ref[...])
        if rope:
            cd = _tile_lanes(cd_ref[...], DIFF_W // LANES)
            sd = _tile_lanes(sd_ref[...], DIFF_W // LANES)
            q = _rope(q, permd_ref, cd, sd)
            k = _rope(k, permd_ref, cd, sd)
        qd_ref[...] = (q * (DIFF_HD ** -0.5 * LOG2E)).astype(BF16)
        kd_ref[...] = k.astype(BF16)
        vmt_ref[...] = vm.T.astype(BF16)
        if rope:
            cm = _tile_lanes(cm_ref[...], MLA_HEADS)
            sm = _tile_lanes(sm_ref[...], MLA_HEADS)
            qm = _rope(qm, permm_ref, cm, sm)
            km = _rope(km, permm_ref, cm, sm)
        qm_ref[...] = (qm * (MLA_QK ** -0.5 * LOG2E)).astype(BF16)
        km_ref[...] = km.astype(BF16)

    def cached_context():
        heads = lambda ref: jnp.concatenate([ref[:, hd, :] for hd in range(DIFF_HEADS)], axis=-1)
        kd_ref[...] = heads(ck_ref).astype(BF16)
        vdt_ref[...] = heads(cv_ref).T.astype(BF16)
        km, vm = _mla_keys_values(cckv_ref[...], ckpe_ref[...], wukv_ref)
        km_ref[...] = _group_rms(km, segm_ref, MLA_QK, gmk_ref[...]).astype(BF16)
        vmt_ref[...] = vm.T.astype(BF16)

    if cache:
        j = pl.program_id(1)
        pl.when(j == 0)(cached_context)
        pl.when(j > 0)(tokens)
    else:
        tokens()


def _layer_spec(w, l, grid_rank, **kw):
    index = (lambda i: (l, 0, 0)) if grid_rank == 1 else (lambda b, j: (l, 0, 0))
    return pl.BlockSpec((None,) + w.shape[1:], index, **kw)


def _inproj(x, mod_l, W, l, rope_tabs, cache, *, batch, row0, states, prev_states=None):
    t = x.shape[0]
    tm = TOK_TILE
    n = t // batch // tm
    rope = rope_tabs is not None
    has_cache = cache is not None
    nk = n + 1 if has_cache else n
    if has_cache:
        grid = (batch, nk)
        tok = lambda b, j: (b * n + jnp.maximum(j - 1, 0), 0)
        key = lambda b, j: (b * nk + j, 0)
        keyt = lambda b, j: (b, 0, j)
        const = lambda b, j: (0, 0)
        modrow = lambda b, j: (row0 + b, 0, 0)
        tab = lambda b, j: (jnp.maximum(j - 1, 0), 0)
    else:
        grid = (batch * n,)
        tok = lambda i: (i, 0)
        key = tok
        keyt = lambda i: (i // n, 0, i % n)
        const = lambda i: (0, 0)
        modrow = lambda i: (row0 + (i // n if row0 else 0), 0, 0)
        tab = lambda i: (i % n, 0)
    names = ("norm1_g", "w_in", "wkpe", "gq", "gk", "gcq", "gckv", "wuq", "wukv", "gmq", "gmk")
    in_specs = [pl.BlockSpec((tm, D_MODEL), tok), pl.BlockSpec((None, 6, D_MODEL), modrow)]
    in_specs += [_layer_spec(W[k], l, len(grid)) for k in names]
    in_specs[2 + names.index("w_in")] = pl.BlockSpec((None, N_MAIN, D_MODEL), in_specs[2].index_map,
                                                     pipeline_mode=pl.Buffered(1))
    in_specs += [pl.BlockSpec((MXU_W, MXU_W), const)] * 2
    seg_d, seg_m, perm_d, perm_m = _group_matrices()
    args = [x, mod_l] + [W[k] for k in names] + [seg_d, seg_m]
    if rope:
        in_specs += [pl.BlockSpec((MXU_W, MXU_W), const)] * 2 + [pl.BlockSpec((tm, LANES), tab)] * 4
        args += [perm_d, perm_m] + list(rope_tabs)
    if has_cache:
        in_specs += [pl.BlockSpec((None, None) + a.shape[2:], lambda b, j, r=a.ndim - 2: (b, l) + (0,) * r)
                     for a in cache]
        args += list(cache)
    sk = nk * tm
    out_shape = [
        jax.ShapeDtypeStruct((t, CONV_W), F32),
        jax.ShapeDtypeStruct((t, DIFF_W), BF16),
        jax.ShapeDtypeStruct((t, MLA_PAD), BF16),
        jax.ShapeDtypeStruct((batch * sk, DIFF_W), BF16),
        jax.ShapeDtypeStruct((batch * sk, MLA_PAD), BF16),
        jax.ShapeDtypeStruct((batch, DIFF_W, sk), BF16),
        jax.ShapeDtypeStruct((batch, MLA_W, sk), BF16),
    ]
    out_specs = [
        pl.BlockSpec((tm, CONV_W), tok),
        pl.BlockSpec((tm, DIFF_W), tok),
        pl.BlockSpec((tm, MLA_PAD), tok),
        pl.BlockSpec((tm, DIFF_W), key),
        pl.BlockSpec((tm, MLA_PAD), key),
        pl.BlockSpec((None, DIFF_W, tm), keyt),
        pl.BlockSpec((None, MLA_W, tm), keyt),
    ]
    n_prev = 0
    if states:
        assert n == 1
        n_prev = l
        st_shapes = ((tm * DIFF_HEADS, LANES), (tm * DIFF_HEADS, LANES), (tm, MLA_KV_RANK), (tm, MLA_ROPE))
        seq_block = lambda layers, s: pl.BlockSpec((None, layers) + s, lambda i: (i, 0, 0, 0))
        out_shape += [jax.ShapeDtypeStruct((batch, l + 1) + s, F32) for s in st_shapes]
        out_specs += [seq_block(l + 1, s) for s in st_shapes]
        if n_prev:
            in_specs += [seq_block(l, s) for s in st_shapes]
            args += list(prev_states)
    return pl.pallas_call(
        functools.partial(_inproj_kernel, rope=rope, states=states, cache=has_cache, n_prev=n_prev),
        grid=grid,
        in_specs=in_specs,
        out_specs=out_specs,
        out_shape=out_shape,
        scratch_shapes=[pltpu.VMEM((D_MODEL, N_MAIN), BF16)],
        compiler_params=pltpu.CompilerParams(
            dimension_semantics=("arbitrary",) * len(grid), vmem_limit_bytes=VMEM_LIMIT),
        name="inproj_latent" if has_cache else "inproj_context",
    )(*args)


def _fold_rows(x, op):
    n = x.shape[0]
    while n % (2 * SUBLANES) == 0 and n > 4 * SUBLANES:
        n //= 2
        x = op(x[:n], x[n:])
    return x


KEY_CHUNK = 256
SCORE_LEAD_CHUNKED = 1
SCORE_LEAD_SINGLE = 2
VALUE_LAG = 1


def _attn_kernel(qd_ref, qm_ref, kd_ref, km_ref, vdt_ref, vmt_ref, lam_ref, sub_ref, od_ref, om_ref, s_ref, p_ref,
                 *, lam_init):
    lp = lam_ref[...]
    lam = (jnp.exp(jnp.sum(lp[0:1] * lp[1:2], axis=-1, keepdims=True))
           - jnp.exp(jnp.sum(lp[2:3] * lp[3:4], axis=-1, keepdims=True)) + lam_init)
    lo = lax.broadcasted_iota(jnp.int32, (1, LANES), 1) < DIFF_HD
    top = lax.broadcasted_iota(jnp.int32, (LANES, 1), 0) < MLA_V
    subg = sub_ref[...]
    sk = kd_ref.shape[0]
    kc = min(KEY_CHUNK, sk)
    n_chunks = sk // kc

    jobs = []
    for h in range(DIFF_HEADS):
        sl = slice(h * LANES, (h + 1) * LANES)
        jobs += [(kd_ref, sl, qd_ref, 0, vdt_ref, sl), (kd_ref, sl, qd_ref, 1, vdt_ref, sl)]
    for h in range(MLA_HEADS):
        hs = slice(h * LANES, (h + 1) * LANES)
        pair = slice((h // 2) * LANES, (h // 2 + 1) * LANES)
        jobs.append((km_ref, hs, qm_ref, None, vmt_ref, pair))
    n = len(jobs)

    def query(i):
        _, sl, q_ref, half, _, _ = jobs[i]
        if half is None:
            return q_ref[:, sl]
        q = q_ref[:, sl].astype(F32)
        return (jnp.where(lo, q, 0.0) if half == 0 else jnp.where(lo, 0.0, q)).astype(BF16)

    qs, mcol, mfin, lcol, acc = {}, {}, {}, {}, {}
    outs = [None] * n
    slots = s_ref.shape[0]
    lead = slots - 1
    pslots = VALUE_LAG + 1
    for i in range(-lead, n + VALUE_LAG):
        a, b, c = i + lead, i, i - VALUE_LAG
        if 0 <= a < n:
            qs[a] = query(a)
        if 0 <= b < n:
            mfin[b] = jnp.max(mcol.pop(b), axis=0, keepdims=True)
        for r in range(n_chunks):
            rows = slice(r * kc, (r + 1) * kc)
            if 0 <= a < n:
                k_ref, ksl = jobs[a][0], jobs[a][1]
                s = lax.dot_general(k_ref[rows, ksl], qs[a], (((1,), (1,)), ((), ())),
                                    preferred_element_type=F32)
                s_ref[a % slots, rows, :] = s
                f = _fold_rows(s, jnp.maximum)
                mcol[a] = f if r == 0 else jnp.maximum(mcol[a], f)
            if 0 <= b < n:
                p = jnp.exp2(s_ref[b % slots, rows, :] - mfin[b])
                f = _fold_rows(p, jnp.add)
                lcol[b] = f if r == 0 else lcol[b] + f
                p_ref[b % pslots, rows, :] = p.astype(BF16)
            if 0 <= c < n:
                vt_ref, vrows = jobs[c][4], jobs[c][5]
                d = jnp.dot(vt_ref[vrows, rows], p_ref[c % pslots, rows, :], preferred_element_type=F32)
                acc[c] = d if r == 0 else acc[c] + d
        if 0 <= c < n:
            outs[c] = acc.pop(c) / jnp.sum(lcol.pop(c), axis=0, keepdims=True)

    for h in range(DIFF_HEADS):
        sl = slice(h * LANES, (h + 1) * LANES)
        a = outs[2 * h] - lam * outs[2 * h + 1]
        ms = jnp.mean(a * a, axis=0, keepdims=True)
        od = a * lax.rsqrt(ms + EPS) * subg * (1.0 - lam_init)
        od_ref[:, sl] = od.T.astype(BF16)
    base = 2 * DIFF_HEADS
    for hp in range(MLA_HEADS // 2):
        sl = slice(hp * LANES, (hp + 1) * LANES)
        om_ref[:, sl] = jnp.where(top, outs[base + 2 * hp], outs[base + 2 * hp + 1]).T.astype(BF16)


def _attention(qd, qm, kd, km, vdt, vmt, W, l, *, batch, lam_init):
    sq = qd.shape[0] // batch
    sk = kd.shape[0] // batch
    tq = TOK_TILE
    nq = sq // tq
    qrow = lambda b, i: (b * nq + i, 0)
    krow = lambda b, i: (b, 0)
    kcol = lambda b, i: (b, 0, 0)
    lead = SCORE_LEAD_CHUNKED if sk > KEY_CHUNK else SCORE_LEAD_SINGLE
    return pl.pallas_call(
        functools.partial(_attn_kernel, lam_init=lam_init),
        grid=(batch, nq),
        in_specs=[
            pl.BlockSpec((tq, DIFF_W), qrow),
            pl.BlockSpec((tq, MLA_PAD), qrow),
            pl.BlockSpec((sk, DIFF_W), krow),
            pl.BlockSpec((sk, MLA_PAD), krow),
            pl.BlockSpec((None, DIFF_W, sk), kcol),
            pl.BlockSpec((None, MLA_W, sk), kcol),
            _layer_spec(W["lam"], l, 2),
            _layer_spec(W["subg"], l, 2),
        ],
        out_specs=[pl.BlockSpec((tq, DIFF_W), qrow), pl.BlockSpec((tq, MLA_W), qrow)],
        out_shape=[jax.ShapeDtypeStruct((batch * sq, DIFF_W), BF16),
                   jax.ShapeDtypeStruct((batch * sq, MLA_W), BF16)],
        scratch_shapes=[pltpu.VMEM((lead + 1, sk, tq), F32), pltpu.VMEM((VALUE_LAG + 1, sk, tq), BF16)],
        compiler_params=pltpu.CompilerParams(
            dimension_semantics=("parallel", "arbitrary"), vmem_limit_bytes=VMEM_LIMIT),
        name="attention",
    )(qd, qm, kd, km, vdt, vmt, W["lam"], W["subg"])


CONV_SPAN = (CONV_K + SUBLANES - 1) // SUBLANES * SUBLANES - SUBLANES


def _depthwise_conv(buf_ref, phase_ref, dw_ref, tm):
    base = HALO - CONV_K // 2
    acc = None
    for r in range(SUBLANES):
        offs = [o for o in range(base, base + CONV_K) if o % SUBLANES == r]
        phase_ref[r] = buf_ref[r:r + tm + CONV_SPAN, :]
        for o in offs:
            term = phase_ref[r, o - r:o - r + tm, :] * dw_ref[o - base:o - base + 1, :]
            acc = term if acc is None else acc + term
    return acc


def _merge_mlp_kernel(*refs, tiles_per_seq):
    it = iter(refs)
    x_ref, mod_ref, g1_ref, g2_ref, u_ref = (next(it) for _ in range(5))
    if tiles_per_seq > 1:
        up_ref, un_ref = next(it), next(it)
    dw_ref, cb_ref, cg_ref, od_ref, om_ref = (next(it) for _ in range(5))
    wg_ref, wc_ref, wd_ref, wm_ref, wo_ref, wup_ref, wdn_ref, y_ref, buf_ref, phase_ref = (next(it) for _ in range(10))
    tm = x_ref.shape[0]

    mod = mod_ref[...]
    x = x_ref[...]
    h = _modulated_norm(x, g1_ref[...], mod[0:1], mod[1:2])
    gates = jax.nn.sigmoid(lax.dot_general(h.astype(BF16), wg_ref[...], (((1,), (1,)), ((), ())),
                                           preferred_element_type=F32))
    o_diff = jnp.dot(od_ref[...], wd_ref[...], preferred_element_type=F32)
    o_mla = jnp.dot(om_ref[...], wm_ref[...], preferred_element_type=F32)

    zeros = jnp.zeros((HALO, CONV_W), F32)
    if tiles_per_seq > 1:
        j = pl.program_id(0) % tiles_per_seq
        prev = jnp.where(j == 0, zeros, up_ref[...])
        nxt = jnp.where(j == tiles_per_seq - 1, zeros, un_ref[...])
    else:
        prev, nxt = zeros, zeros
    buf_ref[0:HALO, :] = prev
    buf_ref[HALO:HALO + tm, :] = u_ref[...]
    buf_ref[HALO + tm:HALO + tm + HALO, :] = nxt
    conv = _depthwise_conv(buf_ref, phase_ref, dw_ref, tm) + cb_ref[...]
    c = _silu(_rms_full(conv, cg_ref[...])).astype(BF16)
    o_conv = jnp.dot(c, wc_ref[...], preferred_element_type=F32)
    merged = (gates[:, 0:D_MODEL] * o_conv + gates[:, D_MODEL:2 * D_MODEL] * o_diff
              + gates[:, 2 * D_MODEL:3 * D_MODEL] * o_mla)
    x = x + mod[2:3] * jnp.dot(merged.astype(BF16), wo_ref[...], preferred_element_type=F32)
    h2 = _modulated_norm(x, g2_ref[...], mod[3:4], mod[4:5])
    up = jnp.dot(h2.astype(BF16), wup_ref[...], preferred_element_type=F32)
    act = jnp.square(jnp.maximum(up, 0.0)).astype(BF16)
    y_ref[...] = x + mod[5:6] * jnp.dot(act, wdn_ref[...], preferred_element_type=F32)


def _merge_mlp(x, mod_l, u, od, om, W, l, *, tiles_per_seq, row0):
    t = x.shape[0]
    tm = TOK_TILE
    nb = t // HALO
    per = tm // HALO
    row = lambda i: (i, 0)

    in_specs = [
        pl.BlockSpec((tm, D_MODEL), row),
        pl.BlockSpec((None, 6, D_MODEL), lambda i: (row0 + (i // tiles_per_seq if row0 else 0), 0, 0)),
        _layer_spec(W["norm1_g"], l, 1),
        _layer_spec(W["norm2_g"], l, 1),
        pl.BlockSpec((tm, CONV_W), row),
    ]
    args = [x, mod_l, W["norm1_g"], W["norm2_g"], u]
    if tiles_per_seq > 1:
        in_specs += [
            pl.BlockSpec((HALO, CONV_W), lambda i: (jnp.maximum(i * per - 1, 0), 0)),
            pl.BlockSpec((HALO, CONV_W), lambda i: (jnp.minimum((i + 1) * per, nb - 1), 0)),
        ]
        args += [u, u]
    big = ("wg", "w_conv_out", "w_diff_out", "w_mla_out", "w_out", "w_up", "w_down")
    in_specs += [_layer_spec(W[k], l, 1) for k in ("conv_dw", "conv_b", "conv_g")]
    in_specs += [pl.BlockSpec((tm, DIFF_W), row), pl.BlockSpec((tm, MLA_W), row)]
    in_specs += [_layer_spec(W[k], l, 1, pipeline_mode=pl.Buffered(1)) for k in big]
    args += [W["conv_dw"], W["conv_b"], W["conv_g"], od, om] + [W[k] for k in big]
    return pl.pallas_call(
        functools.partial(_merge_mlp_kernel, tiles_per_seq=tiles_per_seq),
        grid=(t // tm,),
        in_specs=in_specs,
        out_specs=pl.BlockSpec((tm, D_MODEL), row),
        out_shape=jax.ShapeDtypeStruct((t, D_MODEL), F32),
        scratch_shapes=[pltpu.VMEM((tm + 2 * HALO, CONV_W), F32),
                        pltpu.VMEM((SUBLANES, tm + CONV_SPAN, CONV_W), F32)],
        compiler_params=pltpu.CompilerParams(
            dimension_semantics=("parallel",), vmem_limit_bytes=VMEM_LIMIT),
        name="merge_mlp",
    )(*args)


def _rope_tables(seq_len):
    rows = seq_len // GRID_W
    row = np.repeat(np.arange(rows, dtype=np.float64), GRID_W)
    col = np.tile(np.arange(GRID_W, dtype=np.float64), rows)

    def tables(rot_dim):
        half = rot_dim // 2
        inv = ROPE_BASE ** (-np.arange(0, half, 2, dtype=np.float64) / half)
        a0 = row[:, None] * inv
        a1 = col[:, None] * inv
        cos = np.concatenate([np.cos(a0), np.cos(a0), np.cos(a1), np.cos(a1)], axis=1)
        sin = np.concatenate([-np.sin(a0), np.sin(a0), -np.sin(a1), np.sin(a1)], axis=1)
        return cos, sin

    cd, sd = tables(DIFF_HD)
    cd, sd = np.tile(cd, (1, LANES // DIFF_HD)), np.tile(sd, (1, LANES // DIFF_HD))
    cm, sm = tables(MLA_ROPE)
    pad = LANES - MLA_QK
    cm = np.concatenate([np.ones((seq_len, MLA_NOPE)), cm, np.ones((seq_len, pad))], axis=1)
    sm = np.concatenate([np.zeros((seq_len, MLA_NOPE)), sm, np.zeros((seq_len, pad))], axis=1)
    return tuple(jnp.asarray(t, dtype=F32) for t in (cd, sd, cm, sm))


def _group_matrices():
    lane = np.arange(MXU_W)
    seg_d = (lane[:, None] // DIFF_HD == lane[None, :] // DIFF_HD)
    seg_m = (lane[:, None] // LANES == lane[None, :] // LANES)

    def swap(rotary, half):
        src = np.where(lane % (2 * half) < half, lane + half, lane - half)
        return (lane[:, None] == src[None, :]) & rotary[None, :]

    perm_d = swap(np.ones(MXU_W, bool), DIFF_HD // 4)
    in_head = lane % LANES
    perm_m = swap((in_head >= MLA_NOPE) & (in_head < MLA_QK), MLA_ROPE // 4)
    return tuple(jnp.asarray(m, dtype=BF16) for m in (seg_d, seg_m, perm_d, perm_m))


def _pad_heads(v):
    return jnp.pad(v, [(0, 0)] * (v.ndim - 1) + [(0, LANES - MLA_QK)])


def _prepare_weights(norm1_g, w_in, conv_dw, conv_b, conv_norm_g, w_conv_out, diff_q_norm, diff_k_norm,
                     diff_lambda, diff_subln, w_diff_out, mla_q_a_norm, mla_kv_a_norm, w_uq, w_ukv,
                     mla_q_norm, mla_k_norm, w_mla_out, w_out, norm2_g, w_up, w_down):
    row = lambda v: v[:, None, :]
    kpe_cols = w_in[:, :, N_MAIN:N_MAIN + MLA_ROPE].astype(BF16)
    zc = jnp.zeros_like(kpe_cols)
    w_in_t = jnp.swapaxes(w_in, 1, 2)
    wuq = _pad_heads(w_uq.reshape(DEPTH, MLA_Q_RANK, MLA_HEADS, MLA_QK)).reshape(DEPTH, MLA_Q_RANK, MLA_PAD)
    wukv4 = w_ukv.reshape(DEPTH, MLA_KV_RANK, MLA_HEADS, MLA_NOPE + MLA_V)
    wkn = jnp.pad(wukv4[..., :MLA_NOPE], ((0, 0), (0, 0), (0, 0), (0, LANES - MLA_NOPE)))
    wkn = wkn.reshape(DEPTH, MLA_KV_RANK, MLA_PAD)
    wv = wukv4[..., MLA_NOPE:].reshape(DEPTH, MLA_KV_RANK, MLA_W)
    return dict(
        norm1_g=row(norm1_g), norm2_g=row(norm2_g),
        w_in=w_in_t,
        wkpe=jnp.concatenate([kpe_cols, zc, kpe_cols, zc], axis=-1),
        wg=w_in_t[:, N_MAIN + MLA_ROPE:, :].astype(BF16),
        gq=row(jnp.tile(diff_q_norm, (1, DIFF_W // DIFF_HD))),
        gk=row(jnp.tile(diff_k_norm, (1, DIFF_W // DIFF_HD))),
        gcq=row(mla_q_a_norm), gckv=row(mla_kv_a_norm),
        wuq=wuq.astype(BF16), wukv=jnp.concatenate([wkn, wv], axis=-1).astype(BF16),
        gmq=row(jnp.tile(_pad_heads(mla_q_norm), (1, MLA_HEADS))),
        gmk=row(jnp.tile(_pad_heads(mla_k_norm), (1, MLA_HEADS))),
        conv_dw=jnp.pad(conv_dw, ((0, 0), (0, 1), (0, 0))), conv_b=row(conv_b), conv_g=row(conv_norm_g),
        lam=diff_lambda, subg=diff_subln[:, :, None],
        w_conv_out=w_conv_out.astype(BF16), w_diff_out=w_diff_out.astype(BF16),
        w_mla_out=w_mla_out.astype(BF16), w_out=w_out.astype(BF16),
        w_up=w_up.astype(BF16), w_down=w_down.astype(BF16),
    )


def kernel(x_prompt, x_sample, cache_diff_k, cache_diff_v, cache_mla_ckv, cache_mla_kpe, c, c_ctx, mod_w, mod_b, norm1_g, w_in, conv_dw, conv_b, conv_norm_g, w_conv_out, diff_q_norm, diff_k_norm, diff_lambda, diff_subln, w_diff_out, mla_q_a_norm, mla_kv_a_norm, w_uq, w_ukv, mla_q_norm, mla_k_norm, w_mla_out, w_out, norm2_g, w_up, w_down):
    batch, seq, _ = x_prompt.shape
    dec_batch, dec_seq, _ = x_sample.shape
    past = cache_diff_k.shape[2]
    assert seq == TOK_TILE and past == TOK_TILE and dec_seq % TOK_TILE == 0 and 1 + dec_batch <= 8

    cond_rows = jnp.concatenate([c_ctx[None], c, jnp.zeros((8 - 1 - dec_batch, D_MODEL), F32)], axis=0)
    mod = _modulation(cond_rows, mod_w, mod_b).reshape(DEPTH, 8, 6, D_MODEL)
    tabs = _rope_tables(dec_seq)

    W = _prepare_weights(norm1_g, w_in, conv_dw, conv_b, conv_norm_g, w_conv_out, diff_q_norm, diff_k_norm,
                         diff_lambda, diff_subln, w_diff_out, mla_q_a_norm, mla_kv_a_norm, w_uq, w_ukv,
                         mla_q_norm, mla_k_norm, w_mla_out, w_out, norm2_g, w_up, w_down)
    cache = (cache_diff_k, cache_diff_v, cache_mla_ckv,
             jnp.pad(cache_mla_kpe, ((0, 0), (0, 0), (0, 0), (MLA_NOPE, LANES - MLA_QK))))

    xp = x_prompt.reshape(batch * seq, D_MODEL)
    xs = x_sample.reshape(dec_batch * dec_seq, D_MODEL)
    new_state = None
    for l in range(DEPTH):
        lam_init = 0.8 - 0.6 * math.exp(-0.3 * l)

        u, qd, qm, kd, km, vdt, vmt, *new_state = _inproj(
            xp, mod[l], W, l, None, None, batch=batch, row0=0, states=True, prev_states=new_state)
        od, om = _attention(qd, qm, kd, km, vdt, vmt, W, l, batch=batch, lam_init=lam_init)
        xp = _merge_mlp(xp, mod[l], u, od, om, W, l, tiles_per_seq=1, row0=0)

        u, qd, qm, kd, km, vdt, vmt = _inproj(xs, mod[l], W, l, tabs, cache, batch=dec_batch, row0=1, states=False)
        od, om = _attention(qd, qm, kd, km, vdt, vmt, W, l, batch=dec_batch, lam_init=lam_init)
        xs = _merge_mlp(xs, mod[l], u, od, om, W, l, tiles_per_seq=dec_seq // TOK_TILE, row0=1)

    sk, sv, sckv, skpe = new_state
    return (xp.reshape(batch, seq, D_MODEL), xs.reshape(dec_batch, dec_seq, D_MODEL),
            sk.reshape(batch, DEPTH, seq, DIFF_HEADS, 2 * DIFF_HD),
            sv.reshape(batch, DEPTH, seq, DIFF_HEADS, 2 * DIFF_HD), sckv, skpe)
```

```python
import functools
import math

import numpy as np
import jax
import jax.numpy as jnp
from jax import lax
from jax.experimental import pallas as pl
from jax.experimental.pallas import tpu as pltpu

D_MODEL = 1024
DEPTH = 2
GRID_W = 64
ROPE_BASE = 10000.0
CONV_W = 512
CONV_K = 31
DIFF_HEADS = 4
DIFF_HD = 64
DIFF_W = DIFF_HEADS * 2 * DIFF_HD
MLA_HEADS = 8
MLA_NOPE = 64
MLA_ROPE = 32
MLA_QK = MLA_NOPE + MLA_ROPE
MLA_V = 64
MLA_Q_RANK = 384
MLA_KV_RANK = 256
MLA_W = MLA_HEADS * MLA_V
D_FF = 4 * D_MODEL
EPS = 1e-6

LANES = 128
SUBLANES = 8
MXU_W = 256
MLA_PAD = MLA_HEADS * LANES
TOK_TILE = 256
HALO = 16
N_MAIN = 2 * CONV_W + 3 * DIFF_W + MLA_Q_RANK + MLA_KV_RANK
VMEM_LIMIT = 56 * 1024 * 1024
LOG2E = math.log2(math.e)
F32 = jnp.float32
BF16 = jnp.bfloat16


def _silu(x):
    return x * jax.nn.sigmoid(x)


def _rms_full(x, g):
    ms = jnp.mean(x * x, axis=-1, keepdims=True)
    return x * lax.rsqrt(ms + EPS) * g


def _lane_tiles(x):
    return jnp.concatenate([x[:, t:t + MXU_W] for t in range(0, x.shape[-1], MXU_W)], axis=0)


def _from_lane_tiles(y, rows):
    return jnp.concatenate([y[r:r + rows] for r in range(0, y.shape[0], rows)], axis=-1)


def _group_rms(x, seg_ref, n, g):
    rows = x.shape[0]
    xt = _lane_tiles(x)
    ss = jnp.dot((xt * xt).astype(BF16), seg_ref[...], preferred_element_type=F32)
    return _from_lane_tiles(xt * lax.rsqrt(ss * (1.0 / n) + EPS), rows) * g


def _rope(x, perm_ref, cos, sin_signed):
    rows = x.shape[0]
    sw = jnp.dot(_lane_tiles(x).astype(BF16), perm_ref[...], preferred_element_type=F32)
    return x * cos + _from_lane_tiles(sw, rows) * sin_signed


def _tile_lanes(t, n):
    return jnp.concatenate([t] * n, axis=-1)


def _modulated_norm(x, g, shift, scale):
    return _rms_full(x, g) * (1.0 + scale) + shift


def _mod_kernel(s_ref, w_ref, b_ref, o_ref):
    s = _silu(s_ref[...]).astype(BF16)
    o_ref[...] = jnp.dot(s, w_ref[...].astype(BF16), preferred_element_type=F32) + b_ref[...]


def _modulation(cond_rows, mod_w, mod_b):
    tn = 1536
    n = 6 * D_MODEL
    return pl.pallas_call(
        _mod_kernel,
        grid=(DEPTH, n // tn),
        in_specs=[
            pl.BlockSpec((8, D_MODEL), lambda l, j: (0, 0)),
            pl.BlockSpec((None, D_MODEL, tn), lambda l, j: (l, 0, j)),
            pl.BlockSpec((None, 1, tn), lambda l, j: (l, 0, j)),
        ],
        out_specs=pl.BlockSpec((None, 8, tn), lambda l, j: (l, 0, j)),
        out_shape=jax.ShapeDtypeStruct((DEPTH, 8, n), F32),
        compiler_params=pltpu.CompilerParams(
            dimension_semantics=("parallel", "parallel"), vmem_limit_bytes=VMEM_LIMIT),
        name="modulation",
    )(cond_rows, mod_w, mod_b.reshape(DEPTH, 1, n))


def _mla_keys_values(ckvn, kpe_hi, wukv_ref):
    kv = jnp.dot(ckvn.astype(BF16), wukv_ref[...], preferred_element_type=F32)
    return kv[:, :MLA_PAD] + _tile_lanes(kpe_hi, MLA_HEADS), kv[:, MLA_PAD:]


def _inproj_kernel(*refs, rope, states, cache, n_prev):
    it = iter(refs)
    x_ref, mod_ref, g1_ref, w_ref, wkpe_ref = next(it), next(it), next(it), next(it), next(it)
    gq_ref, gk_ref, gcq_ref, gckv_ref = next(it), next(it), next(it), next(it)
    wuq_ref, wukv_ref, gmq_ref, gmk_ref = next(it), next(it), next(it), next(it)
    segd_ref, segm_ref = next(it), next(it)
    if rope:
        permd_ref, permm_ref = next(it), next(it)
        cd_ref, sd_ref, cm_ref, sm_ref = next(it), next(it), next(it), next(it)
    if cache:
        ck_ref, cv_ref, cckv_ref, ckpe_ref = next(it), next(it), next(it), next(it)
    prev_refs = [next(it) for _ in range(4)] if n_prev else []
    u_ref, qd_ref, qm_ref, kd_ref, km_ref, vdt_ref, vmt_ref = (next(it) for _ in range(7))
    if states:
        state_refs = sk_ref, sv_ref, sckv_ref, skpe_ref = next(it), next(it), next(it), next(it)
    w16_ref = next(it)
    tm = x_ref.shape[0]

    first = functools.reduce(jnp.logical_and, [pl.program_id(a) == 0 for a in range(2 if cache else 1)])

    @pl.when(first)
    def _():
        for c0 in range(0, N_MAIN, LANES):
            w16_ref[:, c0:c0 + LANES] = w_ref[c0:c0 + LANES, :].T.astype(BF16)

    def tokens():
        mod = mod_ref[...]
        h = _modulated_norm(x_ref[...], g1_ref[...], mod[0:1], mod[1:2]).astype(BF16)
        proj = jnp.dot(h, w16_ref[...], preferred_element_type=F32)

        o = 0
        u_a = proj[:, o:o + CONV_W]; o += CONV_W
        u_g = proj[:, o:o + CONV_W]; o += CONV_W
        dq = proj[:, o:o + DIFF_W]; o += DIFF_W
        dk = proj[:, o:o + DIFF_W]; o += DIFF_W
        dv = proj[:, o:o + DIFF_W]; o += DIFF_W
        cq = proj[:, o:o + MLA_Q_RANK]; o += MLA_Q_RANK
        ckv = proj[:, o:o + MLA_KV_RANK]; o += MLA_KV_RANK
        kpe2 = jnp.dot(h, wkpe_ref[...], preferred_element_type=F32)

        cqn = _rms_full(cq, gcq_ref[...])
        ckvn = _rms_full(ckv, gckv_ref[...])
        q = _group_rms(dq, segd_ref, DIFF_HD, gq_ref[...])
        k = _group_rms(dk, segd_ref, DIFF_HD, gk_ref[...])
        qm = jnp.dot(cqn.astype(BF16), wuq_ref[...], preferred_element_type=F32)
        lane = lax.broadcasted_iota(jnp.int32, (1, LANES), 1)
        km, vm = _mla_keys_values(ckvn, jnp.where(lane >= MLA_NOPE, kpe2, 0.0), wukv_ref)

        u_ref[...] = u_a * jax.nn.sigmoid(u_g)
        vdt_ref[...] = dv.T.astype(BF16)
        if states:
            for dst, src in zip(state_refs, prev_refs):
                dst[0:n_prev] = src[...]
            for hd in range(DIFF_HEADS):
                sl = slice(hd * LANES, (hd + 1) * LANES)
                sk_ref[n_prev, pl.ds(hd, tm, stride=DIFF_HEADS), :] = k[:, sl]
                sv_ref[n_prev, pl.ds(hd, tm, stride=DIFF_HEADS), :] = dv[:, sl]
            sckv_ref[n_prev] = ckvn
            skpe_ref[n_prev] = kpe2[:, :MLA_ROPE]

        qm = _group_rms(qm, segm_ref, MLA_QK, gmq_ref[...])
        km = _group_rms(km, segm_ref, MLA_QK, gmk_ref[...])
        if rope:
            cd = _tile_lanes(cd_ref[...], DIFF_W // LANES)
            sd = _tile_lanes(sd_ref[...], DIFF_W // LANES)
            q = _rope(q, permd_ref, cd, sd)
            k = _rope(k, permd_ref, cd, sd)
        qd_ref[...] = (q * (DIFF_HD ** -0.5 * LOG2E)).astype(BF16)
        kd_ref[...] = k.astype(BF16)
        vmt_ref[...] = vm.T.astype(BF16)
        if rope:
            cm = _tile_lanes(cm_ref[...], MLA_HEADS)
            sm = _tile_lanes(sm_ref[...], MLA_HEADS)
            qm = _rope(qm, permm_ref, cm, sm)
            km = _rope(km, permm_ref, cm, sm)
        qm_ref[...] = (qm * (MLA_QK ** -0.5 * LOG2E)).astype(BF16)
        km_ref[...] = km.astype(BF16)

    def cached_context():
        heads = lambda ref: jnp.concatenate([ref[:, hd, :] for hd in range(DIFF_HEADS)], axis=-1)
        kd_ref[...] = heads(ck_ref).astype(BF16)
        vdt_ref[...] = heads(cv_ref).T.astype(BF16)
        km, vm = _mla_keys_values(cckv_ref[...], ckpe_ref[...], wukv_ref)
        km_ref[...] = _group_rms(km, segm_ref, MLA_QK, gmk_ref[...]).astype(BF16)
        vmt_ref[...] = vm.T.astype(BF16)

    if cache:
        j = pl.program_id(1)
        pl.when(j == 0)(cached_context)
        pl.when(j > 0)(tokens)
    else:
        tokens()


def _layer_spec(w, l, grid_rank, **kw):
    index = (lambda i: (l, 0, 0)) if grid_rank == 1 else (lambda b, j: (l, 0, 0))
    return pl.BlockSpec((None,) + w.shape[1:], index, **kw)


def _inproj(x, mod_l, W, l, rope_tabs, cache, *, batch, row0, states, prev_states=None):
    t = x.shape[0]
    tm = TOK_TILE
    n = t // batch // tm
    rope = rope_tabs is not None
    has_cache = cache is not None
    nk = n + 1 if has_cache else n
    if has_cache:
        grid = (batch, nk)
        tok = lambda b, j: (b * n + jnp.maximum(j - 1, 0), 0)
        key = lambda b, j: (b * nk + j, 0)
        keyt = lambda b, j: (b, 0, j)
        const = lambda b, j: (0, 0)
        modrow = lambda b, j: (row0 + b, 0, 0)
        tab = lambda b, j: (jnp.maximum(j - 1, 0), 0)
    else:
        grid = (batch * n,)
        tok = lambda i: (i, 0)
        key = tok
        keyt = lambda i: (i // n, 0, i % n)
        const = lambda i: (0, 0)
        modrow = lambda i: (row0 + (i // n if row0 else 0), 0, 0)
        tab = lambda i: (i % n, 0)
    names = ("norm1_g", "w_in", "wkpe", "gq", "gk", "gcq", "gckv", "wuq", "wukv", "gmq", "gmk")
    in_specs = [pl.BlockSpec((tm, D_MODEL), tok), pl.BlockSpec((None, 6, D_MODEL), modrow)]
    in_specs += [_layer_spec(W[k], l, len(grid)) for k in names]
    in_specs[2 + names.index("w_in")] = pl.BlockSpec((None, N_MAIN, D_MODEL), in_specs[2].index_map,
                                                     pipeline_mode=pl.Buffered(1))
    in_specs += [pl.BlockSpec((MXU_W, MXU_W), const)] * 2
    seg_d, seg_m, perm_d, perm_m = _group_matrices()
    args = [x, mod_l] + [W[k] for k in names] + [seg_d, seg_m]
    if rope:
        in_specs += [pl.BlockSpec((MXU_W, MXU_W), const)] * 2 + [pl.BlockSpec((tm, LANES), tab)] * 4
        args += [perm_d, perm_m] + list(rope_tabs)
    if has_cache:
        in_specs += [pl.BlockSpec((None, None) + a.shape[2:], lambda b, j, r=a.ndim - 2: (b, l) + (0,) * r)
                     for a in cache]
        args += list(cache)
    sk = nk * tm
    out_shape = [
        jax.ShapeDtypeStruct((t, CONV_W), F32),
        jax.ShapeDtypeStruct((t, DIFF_W), BF16),
        jax.ShapeDtypeStruct((t, MLA_PAD), BF16),
        jax.ShapeDtypeStruct((batch * sk, DIFF_W), BF16),
        jax.ShapeDtypeStruct((batch * sk, MLA_PAD), BF16),
        jax.ShapeDtypeStruct((batch, DIFF_W, sk), BF16),
        jax.ShapeDtypeStruct((batch, MLA_W, sk), BF16),
    ]
    out_specs = [
        pl.BlockSpec((tm, CONV_W), tok),
        pl.BlockSpec((tm, DIFF_W), tok),
        pl.BlockSpec((tm, MLA_PAD), tok),
        pl.BlockSpec((tm, DIFF_W), key),
        pl.BlockSpec((tm, MLA_PAD), key),
        pl.BlockSpec((None, DIFF_W, tm), keyt),
        pl.BlockSpec((None, MLA_W, tm), keyt),
    ]
    n_prev = 0
    if states:
        assert n == 1
        n_prev = l
        st_shapes = ((tm * DIFF_HEADS, LANES), (tm * DIFF_HEADS, LANES), (tm, MLA_KV_RANK), (tm, MLA_ROPE))
        seq_block = lambda layers, s: pl.BlockSpec((None, layers) + s, lambda i: (i, 0, 0, 0))
        out_shape += [jax.ShapeDtypeStruct((batch, l + 1) + s, F32) for s in st_shapes]
        out_specs += [seq_block(l + 1, s) for s in st_shapes]
        if n_prev:
            in_specs += [seq_block(l, s) for s in st_shapes]
            args += list(prev_states)
    return pl.pallas_call(
        functools.partial(_inproj_kernel, rope=rope, states=states, cache=has_cache, n_prev=n_prev),
        grid=grid,
        in_specs=in_specs,
        out_specs=out_specs,
        out_shape=out_shape,
        scratch_shapes=[pltpu.VMEM((D_MODEL, N_MAIN), BF16)],
        compiler_params=pltpu.CompilerParams(
            dimension_semantics=("arbitrary",) * len(grid), vmem_limit_bytes=VMEM_LIMIT),
        name="inproj_latent" if has_cache else "inproj_context",
    )(*args)


def _fold_rows(x, op):
    n = x.shape[0]
    while n % (2 * SUBLANES) == 0 and n > 4 * SUBLANES:
        n //= 2
        x = op(x[:n], x[n:])
    return x


KEY_CHUNK = 256
SCORE_LEAD_CHUNKED = 1
SCORE_LEAD_SINGLE = 2
VALUE_LAG = 1


def _attn_kernel(qd_ref, qm_ref, kd_ref, km_ref, vdt_ref, vmt_ref, lam_ref, sub_ref, od_ref, om_ref, s_ref, p_ref,
                 *, lam_init):
    lp = lam_ref[...]
    lam = (jnp.exp(jnp.sum(lp[0:1] * lp[1:2], axis=-1, keepdims=True))
           - jnp.exp(jnp.sum(lp[2:3] * lp[3:4], axis=-1, keepdims=True)) + lam_init)
    lo = lax.broadcasted_iota(jnp.int32, (1, LANES), 1) < DIFF_HD
    top = lax.broadcasted_iota(jnp.int32, (LANES, 1), 0) < MLA_V
    subg = sub_ref[...]
    sk = kd_ref.shape[0]
    kc = min(KEY_CHUNK, sk)
    n_chunks = sk // kc

    jobs = []
    for h in range(DIFF_HEADS):
        sl = slice(h * LANES, (h + 1) * LANES)
        jobs += [(kd_ref, sl, qd_ref, 0, vdt_ref, sl), (kd_ref, sl, qd_ref, 1, vdt_ref, sl)]
    for h in range(MLA_HEADS):
        hs = slice(h * LANES, (h + 1) * LANES)
        pair = slice((h // 2) * LANES, (h // 2 + 1) * LANES)
        jobs.append((km_ref, hs, qm_ref, None, vmt_ref, pair))
    n = len(jobs)

    def query(i):
        _, sl, q_ref, half, _, _ = jobs[i]
        if half is None:
            return q_ref[:, sl]
        q = q_ref[:, sl].astype(F32)
        return (jnp.where(lo, q, 0.0) if half == 0 else jnp.where(lo, 0.0, q)).astype(BF16)

    qs, mcol, mfin, lcol, acc = {}, {}, {}, {}, {}
    outs = [None] * n
    slots = s_ref.shape[0]
    lead = slots - 1
    pslots = VALUE_LAG + 1
    for i in range(-lead, n + VALUE_LAG):
        a, b, c = i + lead, i, i - VALUE_LAG
        if 0 <= a < n:
            qs[a] = query(a)
        if 0 <= b < n:
            mfin[b] = jnp.max(mcol.pop(b), axis=0, keepdims=True)
        for r in range(n_chunks):
            rows = slice(r * kc, (r + 1) * kc)
            if 0 <= a < n:
                k_ref, ksl = jobs[a][0], jobs[a][1]
                s = lax.dot_general(k_ref[rows, ksl], qs[a], (((1,), (1,)), ((), ())),
                                    preferred_element_type=F32)
                s_ref[a % slots, rows, :] = s
                f = _fold_rows(s, jnp.maximum)
                mcol[a] = f if r == 0 else jnp.maximum(mcol[a], f)
            if 0 <= b < n:
                p = jnp.exp2(s_ref[b % slots, rows, :] - mfin[b])
                f = _fold_rows(p, jnp.add)
                lcol[b] = f if r == 0 else lcol[b] + f
                p_ref[b % pslots, rows, :] = p.astype(BF16)
            if 0 <= c < n:
                vt_ref, vrows = jobs[c][4], jobs[c][5]
                d = jnp.dot(vt_ref[vrows, rows], p_ref[c % pslots, rows, :], preferred_element_type=F32)
                acc[c] = d if r == 0 else acc[c] + d
        if 0 <= c < n:
            outs[c] = acc.pop(c) / jnp.sum(lcol.pop(c), axis=0, keepdims=True)

    for h in range(DIFF_HEADS):
        sl = slice(h * LANES, (h + 1) * LANES)
        a = outs[2 * h] - lam * outs[2 * h + 1]
        ms = jnp.mean(a * a, axis=0, keepdims=True)
        od = a * lax.rsqrt(ms + EPS) * subg * (1.0 - lam_init)
        od_ref[:, sl] = od.T.astype(BF16)
    base = 2 * DIFF_HEADS
    for hp in range(MLA_HEADS // 2):
        sl = slice(hp * LANES, (hp + 1) * LANES)
        om_ref[:, sl] = jnp.where(top, outs[base + 2 * hp], outs[base + 2 * hp + 1]).T.astype(BF16)


def _attention(qd, qm, kd, km, vdt, vmt, W, l, *, batch, lam_init):
    sq = qd.shape[0] // batch
    sk = kd.shape[0] // batch
    tq = TOK_TILE
    nq = sq // tq
    qrow = lambda b, i: (b * nq + i, 0)
    krow = lambda b, i: (b, 0)
    kcol = lambda b, i: (b, 0, 0)
    lead = SCORE_LEAD_CHUNKED if sk > KEY_CHUNK else SCORE_LEAD_SINGLE
    return pl.pallas_call(
        functools.partial(_attn_kernel, lam_init=lam_init),
        grid=(batch, nq),
        in_specs=[
            pl.BlockSpec((tq, DIFF_W), qrow),
            pl.BlockSpec((tq, MLA_PAD), qrow),
            pl.BlockSpec((sk, DIFF_W), krow),
            pl.BlockSpec((sk, MLA_PAD), krow),
            pl.BlockSpec((None, DIFF_W, sk), kcol),
            pl.BlockSpec((None, MLA_W, sk), kcol),
            _layer_spec(W["lam"], l, 2),
            _layer_spec(W["subg"], l, 2),
        ],
        out_specs=[pl.BlockSpec((tq, DIFF_W), qrow), pl.BlockSpec((tq, MLA_W), qrow)],
        out_shape=[jax.ShapeDtypeStruct((batch * sq, DIFF_W), BF16),
                   jax.ShapeDtypeStruct((batch * sq, MLA_W), BF16)],
        scratch_shapes=[pltpu.VMEM((lead + 1, sk, tq), F32), pltpu.VMEM((VALUE_LAG + 1, sk, tq), BF16)],
        compiler_params=pltpu.CompilerParams(
            dimension_semantics=("parallel", "arbitrary"), vmem_limit_bytes=VMEM_LIMIT),
        name="attention",
    )(qd, qm, kd, km, vdt, vmt, W["lam"], W["subg"])


CONV_SPAN = (CONV_K + SUBLANES - 1) // SUBLANES * SUBLANES - SUBLANES


def _depthwise_conv(buf_ref, phase_ref, dw_ref, tm):
    base = HALO - CONV_K // 2
    acc = None
    for r in range(SUBLANES):
        offs = [o for o in range(base, base + CONV_K) if o % SUBLANES == r]
        phase_ref[r] = buf_ref[r:r + tm + CONV_SPAN, :]
        for o in offs:
            term = phase_ref[r, o - r:o - r + tm, :] * dw_ref[o - base:o - base + 1, :]
            acc = term if acc is None else acc + term
    return acc


def _merge_mlp_kernel(*refs, tiles_per_seq):
    it = iter(refs)
    x_ref, mod_ref, g1_ref, g2_ref, u_ref = (next(it) for _ in range(5))
    if tiles_per_seq > 1:
        up_ref, un_ref = next(it), next(it)
    dw_ref, cb_ref, cg_ref, od_ref, om_ref = (next(it) for _ in range(5))
    wg_ref, wc_ref, wd_ref, wm_ref, wo_ref, wup_ref, wdn_ref, y_ref, buf_ref, phase_ref = (next(it) for _ in range(10))
    tm = x_ref.shape[0]

    o_diff = jnp.dot(od_ref[...], wd_ref[...], preferred_element_type=F32)
    o_mla = jnp.dot(om_ref[...], wm_ref[...], preferred_element_type=F32)
    mod = mod_ref[...]
    x = x_ref[...]
    h = _modulated_norm(x, g1_ref[...], mod[0:1], mod[1:2])
    gates = jax.nn.sigmoid(lax.dot_general(h.astype(BF16), wg_ref[...], (((1,), (1,)), ((), ())),
                                           preferred_element_type=F32))

    zeros = jnp.zeros((HALO, CONV_W), F32)
    if tiles_per_seq > 1:
        j = pl.program_id(0) % tiles_per_seq
        prev = jnp.where(j == 0, zeros, up_ref[...])
        nxt = jnp.where(j == tiles_per_seq - 1, zeros, un_ref[...])
    else:
        prev, nxt = zeros, zeros
    buf_ref[0:HALO, :] = prev
    buf_ref[HALO:HALO + tm, :] = u_ref[...]
    buf_ref[HALO + tm:HALO + tm + HALO, :] = nxt
    conv = _depthwise_conv(buf_ref, phase_ref, dw_ref, tm) + cb_ref[...]
    c = _silu(_rms_full(conv, cg_ref[...])).astype(BF16)
    o_conv = jnp.dot(c, wc_ref[...], preferred_element_type=F32)
    merged = (gates[:, 0:D_MODEL] * o_conv + gates[:, D_MODEL:2 * D_MODEL] * o_diff
              + gates[:, 2 * D_MODEL:3 * D_MODEL] * o_mla)
    x = x + mod[2:3] * jnp.dot(merged.astype(BF16), wo_ref[...], preferred_element_type=F32)
    h2 = _modulated_norm(x, g2_ref[...], mod[3:4], mod[4:5])
    up = jnp.dot(h2.astype(BF16), wup_ref[...], preferred_element_type=F32)
    act = jnp.square(jnp.maximum(up, 0.0)).astype(BF16)
    y_ref[...] = x + mod[5:6] * jnp.dot(act, wdn_ref[...], preferred_element_type=F32)


def _merge_mlp(x, mod_l, u, od, om, W, l, *, tiles_per_seq, row0):
    t = x.shape[0]
    tm = TOK_TILE
    nb = t // HALO
    per = tm // HALO
    row = lambda i: (i, 0)

    in_specs = [
        pl.BlockSpec((tm, D_MODEL), row),
        pl.BlockSpec((None, 6, D_MODEL), lambda i: (row0 + (i // tiles_per_seq if row0 else 0), 0, 0)),
        _layer_spec(W["norm1_g"], l, 1),
        _layer_spec(W["norm2_g"], l, 1),
        pl.BlockSpec((tm, CONV_W), row),
    ]
    args = [x, mod_l, W["norm1_g"], W["norm2_g"], u]
    if tiles_per_seq > 1:
        in_specs += [
            pl.BlockSpec((HALO, CONV_W), lambda i: (jnp.maximum(i * per - 1, 0), 0)),
            pl.BlockSpec((HALO, CONV_W), lambda i: (jnp.minimum((i + 1) * per, nb - 1), 0)),
        ]
        args += [u, u]
    big = ("wg", "w_conv_out", "w_diff_out", "w_mla_out", "w_out", "w_up", "w_down")
    in_specs += [_layer_spec(W[k], l, 1) for k in ("conv_dw", "conv_b", "conv_g")]
    in_specs += [pl.BlockSpec((tm, DIFF_W), row), pl.BlockSpec((tm, MLA_W), row)]
    in_specs += [_layer_spec(W[k], l, 1, pipeline_mode=pl.Buffered(1)) for k in big]
    args += [W["conv_dw"], W["conv_b"], W["conv_g"], od, om] + [W[k] for k in big]
    return pl.pallas_call(
        functools.partial(_merge_mlp_kernel, tiles_per_seq=tiles_per_seq),
        grid=(t // tm,),
        in_specs=in_specs,
        out_specs=pl.BlockSpec((tm, D_MODEL), row),
        out_shape=jax.ShapeDtypeStruct((t, D_MODEL), F32),
        scratch_shapes=[pltpu.VMEM((tm + 2 * HALO, CONV_W), F32),
                        pltpu.VMEM((SUBLANES, tm + CONV_SPAN, CONV_W), F32)],
        compiler_params=pltpu.CompilerParams(
            dimension_semantics=("parallel",), vmem_limit_bytes=VMEM_LIMIT),
        name="merge_mlp",
    )(*args)


def _rope_tables(seq_len):
    rows = seq_len // GRID_W
    row = np.repeat(np.arange(rows, dtype=np.float64), GRID_W)
    col = np.tile(np.arange(GRID_W, dtype=np.float64), rows)

    def tables(rot_dim):
        half = rot_dim // 2
        inv = ROPE_BASE ** (-np.arange(0, half, 2, dtype=np.float64) / half)
        a0 = row[:, None] * inv
        a1 = col[:, None] * inv
        cos = np.concatenate([np.cos(a0), np.cos(a0), np.cos(a1), np.cos(a1)], axis=1)
        sin = np.concatenate([-np.sin(a0), np.sin(a0), -np.sin(a1), np.sin(a1)], axis=1)
        return cos, sin

    cd, sd = tables(DIFF_HD)
    cd, sd = np.tile(cd, (1, LANES // DIFF_HD)), np.tile(sd, (1, LANES // DIFF_HD))
    cm, sm = tables(MLA_ROPE)
    pad = LANES - MLA_QK
    cm = np.concatenate([np.ones((seq_len, MLA_NOPE)), cm, np.ones((seq_len, pad))], axis=1)
    sm = np.concatenate([np.zeros((seq_len, MLA_NOPE)), sm, np.zeros((seq_len, pad))], axis=1)
    return tuple(jnp.asarray(t, dtype=F32) for t in (cd, sd, cm, sm))


def _group_matrices():
    lane = np.arange(MXU_W)
    seg_d = (lane[:, None] // DIFF_HD == lane[None, :] // DIFF_HD)
    seg_m = (lane[:, None] // LANES == lane[None, :] // LANES)

    def swap(rotary, half):
        src = np.where(lane % (2 * half) < half, lane + half, lane - half)
        return (lane[:, None] == src[None, :]) & rotary[None, :]

    perm_d = swap(np.ones(MXU_W, bool), DIFF_HD // 4)
    in_head = lane % LANES
    perm_m = swap((in_head >= MLA_NOPE) & (in_head < MLA_QK), MLA_ROPE // 4)
    return tuple(jnp.asarray(m, dtype=BF16) for m in (seg_d, seg_m, perm_d, perm_m))


def _pad_heads(v):
    return jnp.pad(v, [(0, 0)] * (v.ndim - 1) + [(0, LANES - MLA_QK)])


def _prepare_weights(norm1_g, w_in, conv_dw, conv_b, conv_norm_g, w_conv_out, diff_q_norm, diff_k_norm,
                     diff_lambda, diff_subln, w_diff_out, mla_q_a_norm, mla_kv_a_norm, w_uq, w_ukv,
                     mla_q_norm, mla_k_norm, w_mla_out, w_out, norm2_g, w_up, w_down):
    row = lambda v: v[:, None, :]
    kpe_cols = w_in[:, :, N_MAIN:N_MAIN + MLA_ROPE].astype(BF16)
    zc = jnp.zeros_like(kpe_cols)
    w_in_t = jnp.swapaxes(w_in, 1, 2)
    wuq = _pad_heads(w_uq.reshape(DEPTH, MLA_Q_RANK, MLA_HEADS, MLA_QK)).reshape(DEPTH, MLA_Q_RANK, MLA_PAD)
    wukv4 = w_ukv.reshape(DEPTH, MLA_KV_RANK, MLA_HEADS, MLA_NOPE + MLA_V)
    wkn = jnp.pad(wukv4[..., :MLA_NOPE], ((0, 0), (0, 0), (0, 0), (0, LANES - MLA_NOPE)))
    wkn = wkn.reshape(DEPTH, MLA_KV_RANK, MLA_PAD)
    wv = wukv4[..., MLA_NOPE:].reshape(DEPTH, MLA_KV_RANK, MLA_W)
    return dict(
        norm1_g=row(norm1_g), norm2_g=row(norm2_g),
        w_in=w_in_t,
        wkpe=jnp.concatenate([kpe_cols, zc, kpe_cols, zc], axis=-1),
        wg=w_in_t[:, N_MAIN + MLA_ROPE:, :].astype(BF16),
        gq=row(jnp.tile(diff_q_norm, (1, DIFF_W // DIFF_HD))),
        gk=row(jnp.tile(diff_k_norm, (1, DIFF_W // DIFF_HD))),
        gcq=row(mla_q_a_norm), gckv=row(mla_kv_a_norm),
        wuq=wuq.astype(BF16), wukv=jnp.concatenate([wkn, wv], axis=-1).astype(BF16),
        gmq=row(jnp.tile(_pad_heads(mla_q_norm), (1, MLA_HEADS))),
        gmk=row(jnp.tile(_pad_heads(mla_k_norm), (1, MLA_HEADS))),
        conv_dw=jnp.pad(conv_dw, ((0, 0), (0, 1), (0, 0))), conv_b=row(conv_b), conv_g=row(conv_norm_g),
        lam=diff_lambda, subg=diff_subln[:, :, None],
        w_conv_out=w_conv_out.astype(BF16), w_diff_out=w_diff_out.astype(BF16),
        w_mla_out=w_mla_out.astype(BF16), w_out=w_out.astype(BF16),
        w_up=w_up.astype(BF16), w_down=w_down.astype(BF16),
    )


def kernel(x_prompt, x_sample, cache_diff_k, cache_diff_v, cache_mla_ckv, cache_mla_kpe, c, c_ctx, mod_w, mod_b, norm1_g, w_in, conv_dw, conv_b, conv_norm_g, w_conv_out, diff_q_norm, diff_k_norm, diff_lambda, diff_subln, w_diff_out, mla_q_a_norm, mla_kv_a_norm, w_uq, w_ukv, mla_q_norm, mla_k_norm, w_mla_out, w_out, norm2_g, w_up, w_down):
    batch, seq, _ = x_prompt.shape
    dec_batch, dec_seq, _ = x_sample.shape
    past = cache_diff_k.shape[2]
    assert seq == TOK_TILE and past == TOK_TILE and dec_seq % TOK_TILE == 0 and 1 + dec_batch <= 8

    cond_rows = jnp.concatenate([c_ctx[None], c, jnp.zeros((8 - 1 - dec_batch, D_MODEL), F32)], axis=0)
    mod = _modulation(cond_rows, mod_w, mod_b).reshape(DEPTH, 8, 6, D_MODEL)
    tabs = _rope_tables(dec_seq)

    W = _prepare_weights(norm1_g, w_in, conv_dw, conv_b, conv_norm_g, w_conv_out, diff_q_norm, diff_k_norm,
                         diff_lambda, diff_subln, w_diff_out, mla_q_a_norm, mla_kv_a_norm, w_uq, w_ukv,
                         mla_q_norm, mla_k_norm, w_mla_out, w_out, norm2_g, w_up, w_down)
    cache = (cache_diff_k, cache_diff_v, cache_mla_ckv,
             jnp.pad(cache_mla_kpe, ((0, 0), (0, 0), (0, 0), (MLA_NOPE, LANES - MLA_QK))))

    xp = x_prompt.reshape(batch * seq, D_MODEL)
    xs = x_sample.reshape(dec_batch * dec_seq, D_MODEL)
    new_state = None
    for l in range(DEPTH):
        lam_init = 0.8 - 0.6 * math.exp(-0.3 * l)

        u, qd, qm, kd, km, vdt, vmt, *new_state = _inproj(
            xp, mod[l], W, l, None, None, batch=batch, row0=0, states=True, prev_states=new_state)
        od, om = _attention(qd, qm, kd, km, vdt, vmt, W, l, batch=batch, lam_init=lam_init)
        xp = _merge_mlp(xp, mod[l], u, od, om, W, l, tiles_per_seq=1, row0=0)

        u, qd, qm, kd, km, vdt, vmt = _inproj(xs, mod[l], W, l, tabs, cache, batch=dec_batch, row0=1, states=False)
        od, om = _attention(qd, qm, kd, km, vdt, vmt, W, l, batch=dec_batch, lam_init=lam_init)
        xs = _merge_mlp(xs, mod[l], u, od, om, W, l, tiles_per_seq=dec_seq // TOK_TILE, row0=1)

    sk, sv, sckv, skpe = new_state
    return (xp.reshape(batch, seq, D_MODEL), xs.reshape(dec_batch, dec_seq, D_MODEL),
            sk.reshape(batch, DEPTH, seq, DIFF_HEADS, 2 * DIFF_HD),
            sv.reshape(batch, DEPTH, seq, DIFF_HEADS, 2 * DIFF_HD), sckv, skpe)
```

```python
import functools
import math

import numpy as np
import jax
import jax.numpy as jnp
from jax import lax
from jax.experimental import pallas as pl
from jax.experimental.pallas import tpu as pltpu

D_MODEL = 1024
DEPTH = 2
GRID_W = 64
ROPE_BASE = 10000.0
CONV_W = 512
CONV_K = 31
DIFF_HEADS = 4
DIFF_HD = 64
DIFF_W = DIFF_HEADS * 2 * DIFF_HD
MLA_HEADS = 8
MLA_NOPE = 64
MLA_ROPE = 32
MLA_QK = MLA_NOPE + MLA_ROPE
MLA_V = 64
MLA_Q_RANK = 384
MLA_KV_RANK = 256
MLA_W = MLA_HEADS * MLA_V
D_FF = 4 * D_MODEL
EPS = 1e-6

LANES = 128
SUBLANES = 8
MXU_W = 256
MLA_PAD = MLA_HEADS * LANES
TOK_TILE = 256
HALO = 16
N_MAIN = 2 * CONV_W + 3 * DIFF_W + MLA_Q_RANK + MLA_KV_RANK
VMEM_LIMIT = 56 * 1024 * 1024
LOG2E = math.log2(math.e)
F32 = jnp.float32
BF16 = jnp.bfloat16


def _silu(x):
    return x * jax.nn.sigmoid(x)


def _rms_full(x, g):
    ms = jnp.mean(x * x, axis=-1, keepdims=True)
    return x * lax.rsqrt(ms + EPS) * g


def _lane_tiles(x):
    return jnp.concatenate([x[:, t:t + MXU_W] for t in range(0, x.shape[-1], MXU_W)], axis=0)


def _from_lane_tiles(y, rows):
    return jnp.concatenate([y[r:r + rows] for r in range(0, y.shape[0], rows)], axis=-1)


def _group_rms(x, seg_ref, n, g):
    rows = x.shape[0]
    xt = _lane_tiles(x)
    ss = jnp.dot((xt * xt).astype(BF16), seg_ref[...], preferred_element_type=F32)
    return _from_lane_tiles(xt * lax.rsqrt(ss * (1.0 / n) + EPS), rows) * g


def _rope(x, perm_ref, cos, sin_signed):
    rows = x.shape[0]
    sw = jnp.dot(_lane_tiles(x).astype(BF16), perm_ref[...], preferred_element_type=F32)
    return x * cos + _from_lane_tiles(sw, rows) * sin_signed


def _tile_lanes(t, n):
    return jnp.concatenate([t] * n, axis=-1)


def _modulated_norm(x, g, shift, scale):
    return _rms_full(x, g) * (1.0 + scale) + shift


def _mod_kernel(s_ref, w_ref, b_ref, o_ref):
    s = _silu(s_ref[...]).astype(BF16)
    o_ref[...] = jnp.dot(s, w_ref[...].astype(BF16), preferred_element_type=F32) + b_ref[...]


def _modulation(cond_rows, mod_w, mod_b):
    tn = 1536
    n = 6 * D_MODEL
    return pl.pallas_call(
        _mod_kernel,
        grid=(DEPTH, n // tn),
        in_specs=[
            pl.BlockSpec((8, D_MODEL), lambda l, j: (0, 0)),
            pl.BlockSpec((None, D_MODEL, tn), lambda l, j: (l, 0, j)),
            pl.BlockSpec((None, 1, tn), lambda l, j: (l, 0, j)),
        ],
        out_specs=pl.BlockSpec((None, 8, tn), lambda l, j: (l, 0, j)),
        out_shape=jax.ShapeDtypeStruct((DEPTH, 8, n), F32),
        compiler_params=pltpu.CompilerParams(
            dimension_semantics=("parallel", "parallel"), vmem_limit_bytes=VMEM_LIMIT),
        name="modulation",
    )(cond_rows, mod_w, mod_b.reshape(DEPTH, 1, n))


def _mla_keys_values(ckvn, kpe_hi, wukv_ref):
    kv = jnp.dot(ckvn.astype(BF16), wukv_ref[...], preferred_element_type=F32)
    return kv[:, :MLA_PAD] + _tile_lanes(kpe_hi, MLA_HEADS), kv[:, MLA_PAD:]


def _inproj_kernel(*refs, rope, states, cache, n_prev):
    it = iter(refs)
    x_ref, mod_ref, g1_ref, w_ref, wkpe_ref = next(it), next(it), next(it), next(it), next(it)
    gq_ref, gk_ref, gcq_ref, gckv_ref = next(it), next(it), next(it), next(it)
    wuq_ref, wukv_ref, gmq_ref, gmk_ref = next(it), next(it), next(it), next(it)
    segd_ref, segm_ref = next(it), next(it)
    if rope:
        permd_ref, permm_ref = next(it), next(it)
        cd_ref, sd_ref, cm_ref, sm_ref = next(it), next(it), next(it), next(it)
    if cache:
        ck_ref, cv_ref, cckv_ref, ckpe_ref = next(it), next(it), next(it), next(it)
    prev_refs = [next(it) for _ in range(4)] if n_prev else []
    u_ref, qd_ref, qm_ref, kd_ref, km_ref, vdt_ref, vmt_ref = (next(it) for _ in range(7))
    if states:
        state_refs = sk_ref, sv_ref, sckv_ref, skpe_ref = next(it), next(it), next(it), next(it)
    w16_ref = next(it)
    tm = x_ref.shape[0]

    first = functools.reduce(jnp.logical_and, [pl.program_id(a) == 0 for a in range(2 if cache else 1)])

    @pl.when(first)
    def _():
        for c0 in range(0, N_MAIN, LANES):
            w16_ref[:, c0:c0 + LANES] = w_ref[c0:c0 + LANES, :].T.astype(BF16)

    def tokens():
        mod = mod_ref[...]
        h = _modulated_norm(x_ref[...], g1_ref[...], mod[0:1], mod[1:2]).astype(BF16)
        proj = jnp.dot(h, w16_ref[...], preferred_element_type=F32)

        o = 0
        u_a = proj[:, o:o + CONV_W]; o += CONV_W
        u_g = proj[:, o:o + CONV_W]; o += CONV_W
        dq = proj[:, o:o + DIFF_W]; o += DIFF_W
        dk = proj[:, o:o + DIFF_W]; o += DIFF_W
        dv = proj[:, o:o + DIFF_W]; o += DIFF_W
        cq = proj[:, o:o + MLA_Q_RANK]; o += MLA_Q_RANK
        ckv = proj[:, o:o + MLA_KV_RANK]; o += MLA_KV_RANK
        kpe2 = jnp.dot(h, wkpe_ref[...], preferred_element_type=F32)

        cqn = _rms_full(cq, gcq_ref[...])
        ckvn = _rms_full(ckv, gckv_ref[...])
        q = _group_rms(dq, segd_ref, DIFF_HD, gq_ref[...])
        k = _group_rms(dk, segd_ref, DIFF_HD, gk_ref[...])
        qm = jnp.dot(cqn.astype(BF16), wuq_ref[...], preferred_element_type=F32)
        lane = lax.broadcasted_iota(jnp.int32, (1, LANES), 1)
        km, vm = _mla_keys_values(ckvn, jnp.where(lane >= MLA_NOPE, kpe2, 0.0), wukv_ref)

        u_ref[...] = u_a * jax.nn.sigmoid(u_g)
        vdt_ref[...] = dv.T.astype(BF16)
        if states:
            for dst, src in zip(state_refs, prev_refs):
                dst[0:n_prev] = src[...]
            for hd in range(DIFF_HEADS):
                sl = slice(hd * LANES, (hd + 1) * LANES)
                sk_ref[n_prev, pl.ds(hd, tm, stride=DIFF_HEADS), :] = k[:, sl]
                sv_ref[n_prev, pl.ds(hd, tm, stride=DIFF_HEADS), :] = dv[:, sl]
            sckv_ref[n_prev] = ckvn
            skpe_ref[n_prev] = kpe2[:, :MLA_ROPE]

        qm = _group_rms(qm, segm_ref, MLA_QK, gmq_ref[...])
        km = _group_rms(km, segm_ref, MLA_QK, gmk_ref[...])
        if rope:
            cd = _tile_lanes(cd_ref[...], DIFF_W // LANES)
            sd = _tile_lanes(sd_ref[...], DIFF_W // LANES)
            q = _rope(q, permd_ref, cd, sd)
            k = _rope(k, permd_ref, cd, sd)
        qd_ref[...] = (q * (DIFF_HD ** -0.5 * LOG2E)).astype(BF16)
        kd_ref[...] = k.astype(BF16)
        vmt_ref[...] = vm.T.astype(BF16)
        if rope:
            cm = _tile_lanes(cm_ref[...], MLA_HEADS)
            sm = _tile_lanes(sm_ref[...], MLA_HEADS)
            qm = _rope(qm, permm_ref, cm, sm)
            km = _rope(km, permm_ref, cm, sm)
        qm_ref[...] = (qm * (MLA_QK ** -0.5 * LOG2E)).astype(BF16)
        km_ref[...] = km.astype(BF16)

    def cached_context():
        heads = lambda ref: jnp.concatenate([ref[:, hd, :] for hd in range(DIFF_HEADS)], axis=-1)
        kd_ref[...] = heads(ck_ref).astype(BF16)
        vdt_ref[...] = heads(cv_ref).T.astype(BF16)
        km, vm = _mla_keys_values(cckv_ref[...], ckpe_ref[...], wukv_ref)
        km_ref[...] = _group_rms(km, segm_ref, MLA_QK, gmk_ref[...]).astype(BF16)
        vmt_ref[...] = vm.T.astype(BF16)

    if cache:
        j = pl.program_id(1)
        pl.when(j == 0)(cached_context)
        pl.when(j > 0)(tokens)
    else:
        tokens()


def _layer_spec(w, l, grid_rank, **kw):
    index = (lambda i: (l, 0, 0)) if grid_rank == 1 else (lambda b, j: (l, 0, 0))
    return pl.BlockSpec((None,) + w.shape[1:], index, **kw)


def _inproj(x, mod_l, W, l, rope_tabs, cache, *, batch, row0, states, prev_states=None):
    t = x.shape[0]
    tm = TOK_TILE
    n = t // batch // tm
    rope = rope_tabs is not None
    has_cache = cache is not None
    nk = n + 1 if has_cache else n
    if has_cache:
        grid = (batch, nk)
        tok = lambda b, j: (b * n + jnp.maximum(j - 1, 0), 0)
        key = lambda b, j: (b * nk + j, 0)
        keyt = lambda b, j: (b, 0, j)
        const = lambda b, j: (0, 0)
        modrow = lambda b, j: (row0 + b, 0, 0)
        tab = lambda b, j: (jnp.maximum(j - 1, 0), 0)
    else:
        grid = (batch * n,)
        tok = lambda i: (i, 0)
        key = tok
        keyt = lambda i: (i // n, 0, i % n)
        const = lambda i: (0, 0)
        modrow = lambda i: (row0 + (i // n if row0 else 0), 0, 0)
        tab = lambda i: (i % n, 0)
    names = ("norm1_g", "w_in", "wkpe", "gq", "gk", "gcq", "gckv", "wuq", "wukv", "gmq", "gmk")
    in_specs = [pl.BlockSpec((tm, D_MODEL), tok), pl.BlockSpec((None, 6, D_MODEL), modrow)]
    in_specs += [_layer_spec(W[k], l, len(grid)) for k in names]
    in_specs[2 + names.index("w_in")] = pl.BlockSpec((None, N_MAIN, D_MODEL), in_specs[2].index_map,
                                                     pipeline_mode=pl.Buffered(1))
    in_specs += [pl.BlockSpec((MXU_W, MXU_W), const)] * 2
    seg_d, seg_m, perm_d, perm_m = _group_matrices()
    args = [x, mod_l] + [W[k] for k in names] + [seg_d, seg_m]
    if rope:
        in_specs += [pl.BlockSpec((MXU_W, MXU_W), const)] * 2 + [pl.BlockSpec((tm, LANES), tab)] * 4
        args += [perm_d, perm_m] + list(rope_tabs)
    if has_cache:
        in_specs += [pl.BlockSpec((None, None) + a.shape[2:], lambda b, j, r=a.ndim - 2: (b, l) + (0,) * r)
                     for a in cache]
        args += list(cache)
    sk = nk * tm
    out_shape = [
        jax.ShapeDtypeStruct((t, CONV_W), F32),
        jax.ShapeDtypeStruct((t, DIFF_W), BF16),
        jax.ShapeDtypeStruct((t, MLA_PAD), BF16),
        jax.ShapeDtypeStruct((batch * sk, DIFF_W), BF16),
        jax.ShapeDtypeStruct((batch * sk, MLA_PAD), BF16),
        jax.ShapeDtypeStruct((batch, DIFF_W, sk), BF16),
        jax.ShapeDtypeStruct((batch, MLA_W, sk), BF16),
    ]
    out_specs = [
        pl.BlockSpec((tm, CONV_W), tok),
        pl.BlockSpec((tm, DIFF_W), tok),
        pl.BlockSpec((tm, MLA_PAD), tok),
        pl.BlockSpec((tm, DIFF_W), key),
        pl.BlockSpec((tm, MLA_PAD), key),
        pl.BlockSpec((None, DIFF_W, tm), keyt),
        pl.BlockSpec((None, MLA_W, tm), keyt),
    ]
    n_prev = 0
    if states:
        assert n == 1
        n_prev = l
        st_shapes = ((tm * DIFF_HEADS, LANES), (tm * DIFF_HEADS, LANES), (tm, MLA_KV_RANK), (tm, MLA_ROPE))
        seq_block = lambda layers, s: pl.BlockSpec((None, layers) + s, lambda i: (i, 0, 0, 0))
        out_shape += [jax.ShapeDtypeStruct((batch, l + 1) + s, F32) for s in st_shapes]
        out_specs += [seq_block(l + 1, s) for s in st_shapes]
        if n_prev:
            in_specs += [seq_block(l, s) for s in st_shapes]
            args += list(prev_states)
    return pl.pallas_call(
        functools.partial(_inproj_kernel, rope=rope, states=states, cache=has_cache, n_prev=n_prev),
        grid=grid,
        in_specs=in_specs,
        out_specs=out_specs,
        out_shape=out_shape,
        scratch_shapes=[pltpu.VMEM((D_MODEL, N_MAIN), BF16)],
        compiler_params=pltpu.CompilerParams(
            dimension_semantics=("arbitrary",) * len(grid), vmem_limit_bytes=VMEM_LIMIT),
        name="inproj_latent" if has_cache else "inproj_context",
    )(*args)


def _fold_rows(x, op):
    n = x.shape[0]
    while n % (2 * SUBLANES) == 0 and n > 4 * SUBLANES:
        n //= 2
        x = op(x[:n], x[n:])
    return x


KEY_CHUNK = 256
SCORE_LEAD_CHUNKED = 1
SCORE_LEAD_SINGLE = 2
VALUE_LAG = 1


def _attn_kernel(qd_ref, qm_ref, kd_ref, km_ref, vdt_ref, vmt_ref, lam_ref, sub_ref, od_ref, om_ref, s_ref, p_ref,
                 *, lam_init):
    lp = lam_ref[...]
    lam = (jnp.exp(jnp.sum(lp[0:1] * lp[1:2], axis=-1, keepdims=True))
           - jnp.exp(jnp.sum(lp[2:3] * lp[3:4], axis=-1, keepdims=True)) + lam_init)
    lo = lax.broadcasted_iota(jnp.int32, (1, LANES), 1) < DIFF_HD
    top = lax.broadcasted_iota(jnp.int32, (LANES, 1), 0) < MLA_V
    subg = sub_ref[...]
    sk = kd_ref.shape[0]
    kc = min(KEY_CHUNK, sk)
    n_chunks = sk // kc

    jobs = []
    for h in range(DIFF_HEADS):
        sl = slice(h * LANES, (h + 1) * LANES)
        jobs += [(kd_ref, sl, qd_ref, 0, vdt_ref, sl), (kd_ref, sl, qd_ref, 1, vdt_ref, sl)]
    for h in range(MLA_HEADS):
        hs = slice(h * LANES, (h + 1) * LANES)
        pair = slice((h // 2) * LANES, (h // 2 + 1) * LANES)
        jobs.append((km_ref, hs, qm_ref, None, vmt_ref, pair))
    n = len(jobs)

    def query(i):
        _, sl, q_ref, half, _, _ = jobs[i]
        if half is None:
            return q_ref[:, sl]
        q = q_ref[:, sl].astype(F32)
        return (jnp.where(lo, q, 0.0) if half == 0 else jnp.where(lo, 0.0, q)).astype(BF16)

    qs, mcol, mfin, lcol, acc = {}, {}, {}, {}, {}
    outs = [None] * n
    slots = s_ref.shape[0]
    lead = slots - 1
    pslots = VALUE_LAG + 1
    for i in range(-lead, n + VALUE_LAG):
        a, b, c = i + lead, i, i - VALUE_LAG
        if 0 <= a < n:
            qs[a] = query(a)
        if 0 <= b < n:
            mfin[b] = jnp.max(mcol.pop(b), axis=0, keepdims=True)
        for r in range(n_chunks):
            rows = slice(r * kc, (r + 1) * kc)
            if 0 <= a < n:
                k_ref, ksl = jobs[a][0], jobs[a][1]
                s = lax.dot_general(k_ref[rows, ksl], qs[a], (((1,), (1,)), ((), ())),
                                    preferred_element_type=F32)
                s_ref[a % slots, rows, :] = s
                f = _fold_rows(s, jnp.maximum)
                mcol[a] = f if r == 0 else jnp.maximum(mcol[a], f)
            if 0 <= b < n:
                p = jnp.exp2(s_ref[b % slots, rows, :] - mfin[b])
                f = _fold_rows(p, jnp.add)
                lcol[b] = f if r == 0 else lcol[b] + f
                p_ref[b % pslots, rows, :] = p.astype(BF16)
            if 0 <= c < n:
                vt_ref, vrows = jobs[c][4], jobs[c][5]
                d = jnp.dot(vt_ref[vrows, rows], p_ref[c % pslots, rows, :], preferred_element_type=F32)
                acc[c] = d if r == 0 else acc[c] + d
        if 0 <= c < n:
            outs[c] = acc.pop(c) / jnp.sum(lcol.pop(c), axis=0, keepdims=True)

    for h in range(DIFF_HEADS):
        sl = slice(h * LANES, (h + 1) * LANES)
        a = outs[2 * h] - lam * outs[2 * h + 1]
        ms = jnp.mean(a * a, axis=0, keepdims=True)
        od = a * lax.rsqrt(ms + EPS) * subg * (1.0 - lam_init)
        od_ref[:, sl] = od.T.astype(BF16)
    base = 2 * DIFF_HEADS
    for hp in range(MLA_HEADS // 2):
        sl = slice(hp * LANES, (hp + 1) * LANES)
        om_ref[:, sl] = jnp.where(top, outs[base + 2 * hp], outs[base + 2 * hp + 1]).T.astype(BF16)


def _attention(qd, qm, kd, km, vdt, vmt, W, l, *, batch, lam_init):
    sq = qd.shape[0] // batch
    sk = kd.shape[0] // batch
    tq = TOK_TILE
    nq = sq // tq
    qrow = lambda b, i: (b * nq + i, 0)
    krow = lambda b, i: (b, 0)
    kcol = lambda b, i: (b, 0, 0)
    lead = SCORE_LEAD_CHUNKED if sk > KEY_CHUNK else SCORE_LEAD_SINGLE
    return pl.pallas_call(
        functools.partial(_attn_kernel, lam_init=lam_init),
        grid=(batch, nq),
        in_specs=[
            pl.BlockSpec((tq, DIFF_W), qrow),
            pl.BlockSpec((tq, MLA_PAD), qrow),
            pl.BlockSpec((sk, DIFF_W), krow),
            pl.BlockSpec((sk, MLA_PAD), krow),
            pl.BlockSpec((None, DIFF_W, sk), kcol),
            pl.BlockSpec((None, MLA_W, sk), kcol),
            _layer_spec(W["lam"], l, 2),
            _layer_spec(W["subg"], l, 2),
        ],
        out_specs=[pl.BlockSpec((tq, DIFF_W), qrow), pl.BlockSpec((tq, MLA_W), qrow)],
        out_shape=[jax.ShapeDtypeStruct((batch * sq, DIFF_W), BF16),
                   jax.ShapeDtypeStruct((batch * sq, MLA_W), BF16)],
        scratch_shapes=[pltpu.VMEM((lead + 1, sk, tq), F32), pltpu.VMEM((VALUE_LAG + 1, sk, tq), BF16)],
        compiler_params=pltpu.CompilerParams(
            dimension_semantics=("parallel", "arbitrary"), vmem_limit_bytes=VMEM_LIMIT),
        name="attention",
    )(qd, qm, kd, km, vdt, vmt, W["lam"], W["subg"])


CONV_SPAN = (CONV_K + SUBLANES - 1) // SUBLANES * SUBLANES - SUBLANES


def _depthwise_conv(buf_ref, phase_ref, dw_ref, tm):
    base = HALO - CONV_K // 2
    acc = None
    for r in range(SUBLANES):
        offs = [o for o in range(base, base + CONV_K) if o % SUBLANES == r]
        phase_ref[r] = buf_ref[r:r + tm + CONV_SPAN, :]
        for o in offs:
            term = phase_ref[r, o - r:o - r + tm, :] * dw_ref[o - base:o - base + 1, :]
            acc = term if acc is None else acc + term
    return acc


def _merge_mlp_kernel(*refs, tiles_per_seq, layer):
    it = iter(refs)
    x_ref, mod_ref, g1_ref, g2_ref, u_ref = (next(it) for _ in range(5))
    if tiles_per_seq > 1:
        up_ref, un_ref = next(it), next(it)
    dw_ref, cb_ref, cg_ref, od_ref, om_ref = (next(it) for _ in range(5))
    w_hbm = [next(it) for _ in range(7)]
    y_ref, buf_ref, phase_ref = next(it), next(it), next(it)
    w_vmem = [next(it) for _ in range(7)]
    sem = next(it)
    wg_ref, wc_ref, wd_ref, wm_ref, wo_ref, wup_ref, wdn_ref = w_vmem
    tm = x_ref.shape[0]
    first_step = pl.program_id(0) == 0

    def weight_copy(k):
        return pltpu.make_async_copy(w_hbm[k].at[layer], w_vmem[k], sem.at[k])

    early, late = (0, 2, 3), (1, 4, 5, 6)

    @pl.when(first_step)
    def _():
        for k in early + late:
            weight_copy(k).start()
        for k in early:
            weight_copy(k).wait()

    mod = mod_ref[...]
    x = x_ref[...]
    h = _modulated_norm(x, g1_ref[...], mod[0:1], mod[1:2])
    gates = jax.nn.sigmoid(lax.dot_general(h.astype(BF16), wg_ref[...], (((1,), (1,)), ((), ())),
                                           preferred_element_type=F32))
    o_diff = jnp.dot(od_ref[...], wd_ref[...], preferred_element_type=F32)
    o_mla = jnp.dot(om_ref[...], wm_ref[...], preferred_element_type=F32)

    @pl.when(first_step)
    def _():
        for k in late:
            weight_copy(k).wait()

    zeros = jnp.zeros((HALO, CONV_W), F32)
    if tiles_per_seq > 1:
        j = pl.program_id(0) % tiles_per_seq
        prev = jnp.where(j == 0, zeros, up_ref[...])
        nxt = jnp.where(j == tiles_per_seq - 1, zeros, un_ref[...])
    else:
        prev, nxt = zeros, zeros
    buf_ref[0:HALO, :] = prev
    buf_ref[HALO:HALO + tm, :] = u_ref[...]
    buf_ref[HALO + tm:HALO + tm + HALO, :] = nxt
    conv = _depthwise_conv(buf_ref, phase_ref, dw_ref, tm) + cb_ref[...]
    c = _silu(_rms_full(conv, cg_ref[...])).astype(BF16)
    o_conv = jnp.dot(c, wc_ref[...], preferred_element_type=F32)
    merged = (gates[:, 0:D_MODEL] * o_conv + gates[:, D_MODEL:2 * D_MODEL] * o_diff
              + gates[:, 2 * D_MODEL:3 * D_MODEL] * o_mla)
    x = x + mod[2:3] * jnp.dot(merged.astype(BF16), wo_ref[...], preferred_element_type=F32)
    h2 = _modulated_norm(x, g2_ref[...], mod[3:4], mod[4:5])
    up = jnp.dot(h2.astype(BF16), wup_ref[...], preferred_element_type=F32)
    act = jnp.square(jnp.maximum(up, 0.0)).astype(BF16)
    y_ref[...] = x + mod[5:6] * jnp.dot(act, wdn_ref[...], preferred_element_type=F32)


def _merge_mlp(x, mod_l, u, od, om, W, l, *, tiles_per_seq, row0):
    t = x.shape[0]
    tm = TOK_TILE
    nb = t // HALO
    per = tm // HALO
    row = lambda i: (i, 0)

    in_specs = [
        pl.BlockSpec((tm, D_MODEL), row),
        pl.BlockSpec((None, 6, D_MODEL), lambda i: (row0 + (i // tiles_per_seq if row0 else 0), 0, 0)),
        _layer_spec(W["norm1_g"], l, 1),
        _layer_spec(W["norm2_g"], l, 1),
        pl.BlockSpec((tm, CONV_W), row),
    ]
    args = [x, mod_l, W["norm1_g"], W["norm2_g"], u]
    if tiles_per_seq > 1:
        in_specs += [
            pl.BlockSpec((HALO, CONV_W), lambda i: (jnp.maximum(i * per - 1, 0), 0)),
            pl.BlockSpec((HALO, CONV_W), lambda i: (jnp.minimum((i + 1) * per, nb - 1), 0)),
        ]
        args += [u, u]
    big = ("wg", "w_conv_out", "w_diff_out", "w_mla_out", "w_out", "w_up", "w_down")
    in_specs += [_layer_spec(W[k], l, 1) for k in ("conv_dw", "conv_b", "conv_g")]
    in_specs += [pl.BlockSpec((tm, DIFF_W), row), pl.BlockSpec((tm, MLA_W), row)]
    in_specs += [pl.BlockSpec(memory_space=pl.ANY)] * len(big)
    args += [W["conv_dw"], W["conv_b"], W["conv_g"], od, om] + [W[k] for k in big]
    return pl.pallas_call(
        functools.partial(_merge_mlp_kernel, tiles_per_seq=tiles_per_seq, layer=l),
        grid=(t // tm,),
        in_specs=in_specs,
        out_specs=pl.BlockSpec((tm, D_MODEL), row),
        out_shape=jax.ShapeDtypeStruct((t, D_MODEL), F32),
        scratch_shapes=[pltpu.VMEM((tm + 2 * HALO, CONV_W), F32),
                        pltpu.VMEM((SUBLANES, tm + CONV_SPAN, CONV_W), F32)]
        + [pltpu.VMEM(W[k].shape[1:], BF16) for k in big] + [pltpu.SemaphoreType.DMA((len(big),))],
        compiler_params=pltpu.CompilerParams(
            dimension_semantics=("arbitrary",), vmem_limit_bytes=VMEM_LIMIT),
        name="merge_mlp",
    )(*args)


def _rope_tables(seq_len):
    rows = seq_len // GRID_W
    row = np.repeat(np.arange(rows, dtype=np.float64), GRID_W)
    col = np.tile(np.arange(GRID_W, dtype=np.float64), rows)

    def tables(rot_dim):
        half = rot_dim // 2
        inv = ROPE_BASE ** (-np.arange(0, half, 2, dtype=np.float64) / half)
        a0 = row[:, None] * inv
        a1 = col[:, None] * inv
        cos = np.concatenate([np.cos(a0), np.cos(a0), np.cos(a1), np.cos(a1)], axis=1)
        sin = np.concatenate([-np.sin(a0), np.sin(a0), -np.sin(a1), np.sin(a1)], axis=1)
        return cos, sin

    cd, sd = tables(DIFF_HD)
    cd, sd = np.tile(cd, (1, LANES // DIFF_HD)), np.tile(sd, (1, LANES // DIFF_HD))
    cm, sm = tables(MLA_ROPE)
    pad = LANES - MLA_QK
    cm = np.concatenate([np.ones((seq_len, MLA_NOPE)), cm, np.ones((seq_len, pad))], axis=1)
    sm = np.concatenate([np.zeros((seq_len, MLA_NOPE)), sm, np.zeros((seq_len, pad))], axis=1)
    return tuple(jnp.asarray(t, dtype=F32) for t in (cd, sd, cm, sm))


def _group_matrices():
    lane = np.arange(MXU_W)
    seg_d = (lane[:, None] // DIFF_HD == lane[None, :] // DIFF_HD)
    seg_m = (lane[:, None] // LANES == lane[None, :] // LANES)

    def swap(rotary, half):
        src = np.where(lane % (2 * half) < half, lane + half, lane - half)
        return (lane[:, None] == src[None, :]) & rotary[None, :]

    perm_d = swap(np.ones(MXU_W, bool), DIFF_HD // 4)
    in_head = lane % LANES
    perm_m = swap((in_head >= MLA_NOPE) & (in_head < MLA_QK), MLA_ROPE // 4)
    return tuple(jnp.asarray(m, dtype=BF16) for m in (seg_d, seg_m, perm_d, perm_m))


def _pad_heads(v):
    return jnp.pad(v, [(0, 0)] * (v.ndim - 1) + [(0, LANES - MLA_QK)])


def _prepare_weights(norm1_g, w_in, conv_dw, conv_b, conv_norm_g, w_conv_out, diff_q_norm, diff_k_norm,
                     diff_lambda, diff_subln, w_diff_out, mla_q_a_norm, mla_kv_a_norm, w_uq, w_ukv,
                     mla_q_norm, mla_k_norm, w_mla_out, w_out, norm2_g, w_up, w_down):
    row = lambda v: v[:, None, :]
    kpe_cols = w_in[:, :, N_MAIN:N_MAIN + MLA_ROPE].astype(BF16)
    zc = jnp.zeros_like(kpe_cols)
    w_in_t = jnp.swapaxes(w_in, 1, 2)
    wuq = _pad_heads(w_uq.reshape(DEPTH, MLA_Q_RANK, MLA_HEADS, MLA_QK)).reshape(DEPTH, MLA_Q_RANK, MLA_PAD)
    wukv4 = w_ukv.reshape(DEPTH, MLA_KV_RANK, MLA_HEADS, MLA_NOPE + MLA_V)
    wkn = jnp.pad(wukv4[..., :MLA_NOPE], ((0, 0), (0, 0), (0, 0), (0, LANES - MLA_NOPE)))
    wkn = wkn.reshape(DEPTH, MLA_KV_RANK, MLA_PAD)
    wv = wukv4[..., MLA_NOPE:].reshape(DEPTH, MLA_KV_RANK, MLA_W)
    return dict(
        norm1_g=row(norm1_g), norm2_g=row(norm2_g),
        w_in=w_in_t,
        wkpe=jnp.concatenate([kpe_cols, zc, kpe_cols, zc], axis=-1),
        wg=w_in_t[:, N_MAIN + MLA_ROPE:, :].astype(BF16),
        gq=row(jnp.tile(diff_q_norm, (1, DIFF_W // DIFF_HD))),
        gk=row(jnp.tile(diff_k_norm, (1, DIFF_W // DIFF_HD))),
        gcq=row(mla_q_a_norm), gckv=row(mla_kv_a_norm),
        wuq=wuq.astype(BF16), wukv=jnp.concatenate([wkn, wv], axis=-1).astype(BF16),
        gmq=row(jnp.tile(_pad_heads(mla_q_norm), (1, MLA_HEADS))),
        gmk=row(jnp.tile(_pad_heads(mla_k_norm), (1, MLA_HEADS))),
        conv_dw=jnp.pad(conv_dw, ((0, 0), (0, 1), (0, 0))), conv_b=row(conv_b), conv_g=row(conv_norm_g),
        lam=diff_lambda, subg=diff_subln[:, :, None],
        w_conv_out=w_conv_out.astype(BF16), w_diff_out=w_diff_out.astype(BF16),
        w_mla_out=w_mla_out.astype(BF16), w_out=w_out.astype(BF16),
        w_up=w_up.astype(BF16), w_down=w_down.astype(BF16),
    )


def kernel(x_prompt, x_sample, cache_diff_k, cache_diff_v, cache_mla_ckv, cache_mla_kpe, c, c_ctx, mod_w, mod_b, norm1_g, w_in, conv_dw, conv_b, conv_norm_g, w_conv_out, diff_q_norm, diff_k_norm, diff_lambda, diff_subln, w_diff_out, mla_q_a_norm, mla_kv_a_norm, w_uq, w_ukv, mla_q_norm, mla_k_norm, w_mla_out, w_out, norm2_g, w_up, w_down):
    batch, seq, _ = x_prompt.shape
    dec_batch, dec_seq, _ = x_sample.shape
    past = cache_diff_k.shape[2]
    assert seq == TOK_TILE and past == TOK_TILE and dec_seq % TOK_TILE == 0 and 1 + dec_batch <= 8

    cond_rows = jnp.concatenate([c_ctx[None], c, jnp.zeros((8 - 1 - dec_batch, D_MODEL), F32)], axis=0)
    mod = _modulation(cond_rows, mod_w, mod_b).reshape(DEPTH, 8, 6, D_MODEL)
    tabs = _rope_tables(dec_seq)

    W = _prepare_weights(norm1_g, w_in, conv_dw, conv_b, conv_norm_g, w_conv_out, diff_q_norm, diff_k_norm,
                         diff_lambda, diff_subln, w_diff_out, mla_q_a_norm, mla_kv_a_norm, w_uq, w_ukv,
                         mla_q_norm, mla_k_norm, w_mla_out, w_out, norm2_g, w_up, w_down)
    cache = (cache_diff_k, cache_diff_v, cache_mla_ckv,
             jnp.pad(cache_mla_kpe, ((0, 0), (0, 0), (0, 0), (MLA_NOPE, LANES - MLA_QK))))

    xp = x_prompt.reshape(batch * seq, D_MODEL)
    xs = x_sample.reshape(dec_batch * dec_seq, D_MODEL)
    new_state = None
    for l in range(DEPTH):
        lam_init = 0.8 - 0.6 * math.exp(-0.3 * l)

        u, qd, qm, kd, km, vdt, vmt, *new_state = _inproj(
            xp, mod[l], W, l, None, None, batch=batch, row0=0, states=True, prev_states=new_state)
        od, om = _attention(qd, qm, kd, km, vdt, vmt, W, l, batch=batch, lam_init=lam_init)
        xp = _merge_mlp(xp, mod[l], u, od, om, W, l, tiles_per_seq=1, row0=0)

        u, qd, qm, kd, km, vdt, vmt = _inproj(xs, mod[l], W, l, tabs, cache, batch=dec_batch, row0=1, states=False)
        od, om = _attention(qd, qm, kd, km, vdt, vmt, W, l, batch=dec_batch, lam_init=lam_init)
        xs = _merge_mlp(xs, mod[l], u, od, om, W, l, tiles_per_seq=dec_seq // TOK_TILE, row0=1)

    sk, sv, sckv, skpe = new_state
    return (xp.reshape(batch, seq, D_MODEL), xs.reshape(dec_batch, dec_seq, D_MODEL),
            sk.reshape(batch, DEPTH, seq, DIFF_HEADS, 2 * DIFF_HD),
            sv.reshape(batch, DEPTH, seq, DIFF_HEADS, 2 * DIFF_HD), sckv, skpe)
```
